```python
import math
import jax
import jax.numpy as jnp
from jax import lax
import numpy as np

D_MODEL = 2048
BATCH = 8
SEQ = 4096
DEPTH = 2

GRID_W = 64
CTX_LEN = 256

MLA_HEADS = 8
MLA_NOPE = 128
MLA_ROPE = 64
MLA_V = 128
MLA_QK = MLA_NOPE + MLA_ROPE
MLA_Q_RANK = 768
MLA_KV_RANK = 512
MLA_WIDTH = MLA_HEADS * MLA_V

NA_HEADS = 16
NA_HEAD_DIM = 64
NA_WIDTH = NA_HEADS * NA_HEAD_DIM
NA_WIN_R = 8
NA_WIN_C = 16

SSM_WIDTH = 1024
SSM_GROUP = 16
SSM_GROUPS = SSM_WIDTH // SSM_GROUP
SSM_STATE = 64
DT_MIN = 1e-3
DT_MAX = 1e-1

N_EXPERTS = 32
TOP_K = 4
D_FF = 3 * D_MODEL // 4
SWIGLU_LIMIT = 7.0
SWIGLU_ALPHA = 1.702

ROPE_BASE = 10000.0
EPS = 1e-6
ATTN_BLOCK = 128
MLA_SCALE = MLA_QK ** -0.5
NA_SCALE = NA_HEAD_DIM ** -0.5

OFF_CQ = 0
OFF_CKV = OFF_CQ + MLA_Q_RANK
OFF_KPE = OFF_CKV + MLA_KV_RANK
OFF_NAQ = OFF_KPE + MLA_ROPE
OFF_NAK = OFF_NAQ + NA_WIDTH
OFF_NAV = OFF_NAK + NA_WIDTH
OFF_SSM = OFF_NAV + NA_WIDTH
OFF_GATE = OFF_SSM + SSM_WIDTH
IN_COLS = OFF_GATE + 3 * D_MODEL

kernel_name = 'hybrid_mla_natten_s5_moe_dit_block'


def rms_norm(x, g):
    xf = x.astype(jnp.float32)
    y = xf * lax.rsqrt(jnp.mean(jnp.square(xf), axis=-1, keepdims=True) + EPS)
    return (y * g.astype(jnp.float32)).astype(x.dtype)


def modulate(h, shift, scale):
    return h * (1.0 + scale) + shift


def axial_rope(n_tokens):
    t = jnp.arange(n_tokens, dtype=jnp.int32)
    row = (t // GRID_W).astype(jnp.float32)
    col = (t % GRID_W).astype(jnp.float32)
    n_freq = MLA_ROPE // 4
    inv_freq = ROPE_BASE ** (-jnp.arange(n_freq, dtype=jnp.float32) / n_freq)
    ang = jnp.concatenate([row[:, None] * inv_freq, col[:, None] * inv_freq], axis=-1)
    return jnp.cos(ang), jnp.sin(ang)


def apply_rope(x, cos, sin):
    half = x.shape[-1] // 2
    x1, x2 = x[..., :half], x[..., half:]
    cos = cos.astype(x.dtype)
    sin = sin.astype(x.dtype)
    return jnp.concatenate([x1 * cos - x2 * sin, x1 * sin + x2 * cos], axis=-1)


def dense_attend(q, k, v, scale):
    s = jnp.einsum('bqhd,bkhd->bhqk', q, k).astype(jnp.float32) * scale
    p = jax.nn.softmax(s, axis=-1).astype(v.dtype)
    return jnp.einsum('bhqk,bkhd->bqhd', p, v)


def mla_attend(qn, qr, kn, kr, v):
    s = jnp.einsum('bqhd,bkhd->bhqk', qn, kn) + jnp.einsum('bqhr,bkr->bhqk', qr, kr)
    p = jax.nn.softmax(s.astype(jnp.float32) * MLA_SCALE, axis=-1).astype(v.dtype)
    return jnp.einsum('bhqk,bkhd->bqhd', p, v)


def to_query_blocks(t):
    b, n = t.shape[:2]
    return jnp.moveaxis(t.reshape((b, n // ATTN_BLOCK, ATTN_BLOCK) + t.shape[2:]), 1, 0)


def from_query_blocks(t):
    nb, b, blk = t.shape[:3]
    return jnp.moveaxis(t, 0, 1).reshape((b, nb * blk) + t.shape[3:])


def mla_branch(z, zc, q_norm_g, kv_norm_g, w_uq, w_ukv, q_gain, k_gain, need_ctx):
    b, s = z.shape[:2]

    def project_q(zz):
        q = rms_norm(zz[..., OFF_CQ:OFF_CKV], q_norm_g) @ w_uq
        q = q.reshape(zz.shape[:2] + (MLA_HEADS, MLA_QK))
        return (rms_norm(q[..., :MLA_NOPE], q_gain[:MLA_NOPE]),
                rms_norm(q[..., MLA_NOPE:], q_gain[MLA_NOPE:]))

    def project_kv(zz):
        kv = rms_norm(zz[..., OFF_CKV:OFF_KPE], kv_norm_g) @ w_ukv
        kv = kv.reshape(zz.shape[:2] + (MLA_HEADS, MLA_NOPE + MLA_V))
        k_nope = rms_norm(kv[..., :MLA_NOPE], k_gain[:MLA_NOPE])
        k_pe = rms_norm(zz[..., OFF_KPE:OFF_NAQ], k_gain[MLA_NOPE:])
        return k_nope, k_pe, kv[..., MLA_NOPE:]

    cos, sin = axial_rope(s)
    qn, qr = project_q(z)
    qr = apply_rope(qr, cos[:, None, :], sin[:, None, :])
    kn, kr, v = project_kv(z)
    kr = apply_rope(kr, cos, sin)
    kn_c, kr_c, v_c = project_kv(zc)
    kn_all = jnp.concatenate([kn_c, kn], axis=1)
    kr_all = jnp.concatenate([kr_c, kr], axis=1)
    v_all = jnp.concatenate([v_c, v], axis=1)
    o = lax.map(lambda qb: mla_attend(qb[0], qb[1], kn_all, kr_all, v_all),
                (to_query_blocks(qn), to_query_blocks(qr)))
    o = from_query_blocks(o).reshape(b, s, MLA_WIDTH)
    o_c = None
    if need_ctx:
        qn_c, qr_c = project_q(zc)
        o_c = mla_attend(qn_c, qr_c, kn_c, kr_c, v_c).reshape(zc.shape[:2] + (MLA_WIDTH,))
    return o, o_c


def na_branch(z, zc, q_gain, k_gain, rpb, need_ctx):
    b, s = z.shape[:2]
    rows = s // GRID_W
    wr = min(NA_WIN_R, rows)
    wc = NA_WIN_C

    def heads(zz, off):
        return zz[..., off:off + NA_WIDTH].reshape(zz.shape[:2] + (NA_HEADS, NA_HEAD_DIM))

    grid = (b, rows, GRID_W, NA_HEADS, NA_HEAD_DIM)
    q = rms_norm(heads(z, OFF_NAQ), q_gain).reshape(grid)
    k = rms_norm(heads(z, OFF_NAK), k_gain).reshape(grid)
    v = heads(z, OFF_NAV).reshape(grid)
    k_c = rms_norm(heads(zc, OFF_NAK), k_gain)
    v_c = heads(zc, OFF_NAV)

    cols = jnp.arange(GRID_W)
    col_idx = jnp.clip(cols - wc // 2, 0, GRID_W - wc)[:, None] + jnp.arange(wc)
    dcol = col_idx - cols[:, None] + (NA_WIN_C - 1)
    rpb32 = rpb.astype(jnp.float32)

    def row_block(r):
        r0 = jnp.clip(r - wr // 2, 0, rows - wr)
        k_win = lax.dynamic_slice_in_dim(k, r0, wr, axis=1)[:, :, col_idx]
        v_win = lax.dynamic_slice_in_dim(v, r0, wr, axis=1)[:, :, col_idx]
        q_row = lax.dynamic_index_in_dim(q, r, axis=1, keepdims=False)
        drow = r0 + jnp.arange(wr) - r + (NA_WIN_R - 1)
        bias = rpb32[:, drow[None, :, None], dcol[:, None, :]]
        s_loc = jnp.einsum('bwhd,brwchd->bhwrc', q_row, k_win).astype(jnp.float32) * NA_SCALE + bias
        s_ctx = jnp.einsum('bwhd,bkhd->bhwk', q_row, k_c).astype(jnp.float32) * NA_SCALE
        logits = jnp.concatenate([s_loc.reshape(s_loc.shape[:3] + (wr * wc,)), s_ctx], axis=-1)
        p = jax.nn.softmax(logits, axis=-1).astype(v.dtype)
        p_loc = p[..., :wr * wc].reshape(s_loc.shape)
        p_ctx = p[..., wr * wc:]
        return (jnp.einsum('bhwrc,brwchd->bwhd', p_loc, v_win)
                + jnp.einsum('bhwk,bkhd->bwhd', p_ctx, v_c))

    o = lax.map(row_block, jnp.arange(rows))
    o = jnp.moveaxis(o, 0, 1).reshape(b, s, NA_WIDTH)
    o_c = None
    if need_ctx:
        q_c = rms_norm(heads(zc, OFF_NAQ), q_gain)
        o_c = dense_attend(q_c, k_c, v_c, NA_SCALE).reshape(zc.shape[:2] + (NA_WIDTH,))
    return o, o_c


def ssm_combine(e1, e2):
    a1r, a1i, h1r, h1i = e1
    a2r, a2i, h2r, h2i = e2
    return (a2r * a1r - a2i * a1i, a2r * a1i + a2i * a1r,
            a2r * h1r - a2i * h1i + h2r, a2r * h1i + a2i * h1r + h2i)


def ssm_discretize(lam_re, lam_im, log_dt, b_re, b_im):
    lam_re = lam_re.astype(jnp.float32)
    lam_im = lam_im.astype(jnp.float32)
    b_re = b_re.astype(jnp.float32)
    b_im = b_im.astype(jnp.float32)
    dt = jnp.exp(log_dt.astype(jnp.float32))[:, None]
    mag = jnp.exp(lam_re * dt)
    a_re = mag * jnp.cos(lam_im * dt)
    a_im = mag * jnp.sin(lam_im * dt)
    den = lam_re * lam_re + lam_im * lam_im
    z_re = ((a_re - 1.0) * lam_re + a_im * lam_im) / den
    z_im = (a_im * lam_re - (a_re - 1.0) * lam_im) / den
    bb_re = z_re[..., None] * b_re - z_im[..., None] * b_im
    bb_im = z_re[..., None] * b_im + z_im[..., None] * b_re
    return a_re, a_im, bb_re, bb_im


def ssm_scan(a_re, a_im, u_re, u_im, h0, reverse):
    n = u_re.shape[1]
    if h0 is not None:
        idx = n - 1 if reverse else 0
        h0_re, h0_im = h0
        u_re = u_re.at[:, idx].add(a_re * h0_re - a_im * h0_im)
        u_im = u_im.at[:, idx].add(a_re * h0_im + a_im * h0_re)
    a_re_t = jnp.broadcast_to(a_re, (1, n) + a_re.shape)
    a_im_t = jnp.broadcast_to(a_im, (1, n) + a_im.shape)
    _, _, h_re, h_im = lax.associative_scan(ssm_combine, (a_re_t, a_im_t, u_re, u_im),
                                            reverse=reverse, axis=1)
    return h_re, h_im


def ssm_branch(z, zc, lam_re, lam_im, log_dt, b_re, b_im, c_re, c_im, d_skip, w_glu, b_glu, need_ctx):
    def inputs(zz):
        return zz[..., OFF_SSM:OFF_GATE].reshape(zz.shape[:2] + (SSM_GROUPS, SSM_GROUP))

    u = inputs(z)
    u_c = inputs(zc)
    d = d_skip.reshape(SSM_GROUPS, SSM_GROUP).astype(jnp.float32)
    y = d * u.astype(jnp.float32)
    y_c = d * u_c.astype(jnp.float32) if need_ctx else None
    for direction, reverse in ((0, False), (1, True)):
        a_re, a_im, bb_re, bb_im = ssm_discretize(lam_re[direction], lam_im[direction], log_dt[direction],
                                                  b_re[direction], b_im[direction])
        cr = c_re[direction].astype(jnp.float32)
        ci = c_im[direction].astype(jnp.float32)

        def drive(uu):
            uf = uu.astype(jnp.float32)
            return (jnp.einsum('gpn,btgn->btgp', bb_re, uf), jnp.einsum('gpn,btgn->btgp', bb_im, uf))

        def readout(h_re, h_im):
            return jnp.einsum('gnp,btgp->btgn', cr, h_re) - jnp.einsum('gnp,btgp->btgn', ci, h_im)

        hc_re, hc_im = ssm_scan(a_re, a_im, *drive(u_c), None, reverse)
        last = 0 if reverse else -1
        h_re, h_im = ssm_scan(a_re, a_im, *drive(u), (hc_re[:, last], hc_im[:, last]), reverse)
        y = y + readout(h_re, h_im)
        if need_ctx:
            y_c = y_c + readout(hc_re, hc_im)

    def glu(yy):
        g = jax.nn.gelu(yy.reshape(yy.shape[:2] + (SSM_WIDTH,)).astype(z.dtype))
        return g * jax.nn.sigmoid(g @ w_glu + b_glu)

    return glu(y), (glu(y_c) if need_ctx else None)


def merge_branches(zz, o_a, o_b, o_s, w_pa, w_pb, w_pc, w_out):
    g_a, g_b, g_s = jnp.split(zz[..., OFF_GATE:], 3, axis=-1)
    mix = (jax.nn.sigmoid(g_a) * (o_a @ w_pa)
           + jax.nn.sigmoid(g_b) * (o_b @ w_pb)
           + jax.nn.sigmoid(g_s) * (o_s @ w_pc))
    return mix @ w_out


def moe_ffn(h, w_router, b_router, w_gate_up, b_gate_up, w_down, b_down):
    shp = h.shape
    t = h.reshape(-1, shp[-1])
    logits = (t @ w_router + b_router).astype(jnp.float32)
    top_val, top_idx = lax.top_k(logits, TOP_K)
    top_w = jax.nn.softmax(top_val, axis=-1)
    gates = jnp.einsum('nk,nke->ne', top_w,
                       jax.nn.one_hot(top_idx, N_EXPERTS, dtype=jnp.float32)).astype(t.dtype)
    y = jnp.zeros_like(t)
    for e in range(N_EXPERTS):
        gu = t @ w_gate_up[e] + b_gate_up[e]
        g = jnp.minimum(gu[:, :D_FF], SWIGLU_LIMIT)
        lin = jnp.clip(gu[:, D_FF:], -SWIGLU_LIMIT, SWIGLU_LIMIT)
        act = g * jax.nn.sigmoid(SWIGLU_ALPHA * g) * (lin + 1.0)
        y = y + gates[:, e:e + 1] * (act @ w_down[e] + b_down[e])
    return y.reshape(shp)


def setup_inputs(seed: int = 0) -> dict:
    key = jax.random.key(seed)
    keys = list(jax.random.split(key, 40))
    L = DEPTH
    f32 = jnp.float32

    def normal(shape, scale):
        return scale * jax.random.normal(keys.pop(), shape, f32)

    def gain(shape):
        return 1.0 + normal(shape, 0.02)

    ssm_shape = (L, 2, SSM_GROUPS, SSM_STATE)
    return {
        'x': normal((BATCH, SEQ, D_MODEL), 1.0),
        'c': normal((BATCH, D_MODEL), 1.0),
        'ctx': normal((BATCH, CTX_LEN, D_MODEL), 1.0),
        'c_ctx': normal((D_MODEL,), 1.0),
        'w_ada': normal((L, D_MODEL, 6 * D_MODEL), D_MODEL ** -0.5),
        'b_ada': normal((L, 6 * D_MODEL), 0.02),
        'norm1_g': gain((L, D_MODEL)),
        'w_in': normal((L, D_MODEL, IN_COLS), D_MODEL ** -0.5),
        'mla_q_norm': gain((L, MLA_Q_RANK)),
        'mla_kv_norm': gain((L, MLA_KV_RANK)),
        'mla_w_uq': normal((L, MLA_Q_RANK, MLA_HEADS * MLA_QK), MLA_Q_RANK ** -0.5),
        'mla_w_ukv': normal((L, MLA_KV_RANK, MLA_HEADS * (MLA_NOPE + MLA_V)), MLA_KV_RANK ** -0.5),
        'mla_q_gain': gain((L, MLA_QK)),
        'mla_k_gain': gain((L, MLA_QK)),
        'na_q_gain': gain((L, NA_HEAD_DIM)),
        'na_k_gain': gain((L, NA_HEAD_DIM)),
        'na_rpb': normal((L, NA_HEADS, 2 * NA_WIN_R - 1, 2 * NA_WIN_C - 1), 0.02),
        'ssm_lam_re': -0.5 + normal(ssm_shape, 0.01),
        'ssm_lam_im': math.pi * jnp.arange(SSM_STATE, dtype=f32) + normal(ssm_shape, 0.01),
        'ssm_log_dt': jax.random.uniform(keys.pop(), (L, 2, SSM_GROUPS), f32,
                                         math.log(DT_MIN), math.log(DT_MAX)),
        'ssm_b_re': normal((L, 2, SSM_GROUPS, SSM_STATE, SSM_GROUP), (2 * SSM_GROUP) ** -0.5),
        'ssm_b_im': normal((L, 2, SSM_GROUPS, SSM_STATE, SSM_GROUP), (2 * SSM_GROUP) ** -0.5),
        'ssm_c_re': normal((L, 2, SSM_GROUPS, SSM_GROUP, SSM_STATE), SSM_STATE ** -0.5),
        'ssm_c_im': normal((L, 2, SSM_GROUPS, SSM_GROUP, SSM_STATE), SSM_STATE ** -0.5),
        'ssm_d': normal((L, SSM_WIDTH), 1.0),
        'ssm_w_glu': normal((L, SSM_WIDTH, SSM_WIDTH), SSM_WIDTH ** -0.5),
        'ssm_b_glu': normal((L, SSM_WIDTH), 0.02),
        'w_pa': normal((L, MLA_WIDTH, D_MODEL), MLA_WIDTH ** -0.5),
        'w_pb': normal((L, NA_WIDTH, D_MODEL), NA_WIDTH ** -0.5),
        'w_pc': normal((L, SSM_WIDTH, D_MODEL), SSM_WIDTH ** -0.5),
        'w_out': normal((L, D_MODEL, D_MODEL), D_MODEL ** -0.5),
        'norm2_g': gain((L, D_MODEL)),
        'w_router': normal((L, D_MODEL, N_EXPERTS), D_MODEL ** -0.5),
        'b_router': normal((L, N_EXPERTS), 0.01),
        'w_gate_up': normal((L, N_EXPERTS, D_MODEL, 2 * D_FF), D_MODEL ** -0.5),
        'b_gate_up': normal((L, N_EXPERTS, 2 * D_FF), 0.02),
        'w_down': normal((L, N_EXPERTS, D_FF, D_MODEL), D_FF ** -0.5),
        'b_down': normal((L, N_EXPERTS, D_MODEL), 0.02),
    }


def reference(x, c, ctx, c_ctx, w_ada, b_ada, norm1_g, w_in, mla_q_norm, mla_kv_norm, mla_w_uq, mla_w_ukv,
              mla_q_gain, mla_k_gain, na_q_gain, na_k_gain, na_rpb, ssm_lam_re, ssm_lam_im, ssm_log_dt,
              ssm_b_re, ssm_b_im, ssm_c_re, ssm_c_im, ssm_d, ssm_w_glu, ssm_b_glu, w_pa, w_pb, w_pc, w_out,
              norm2_g, w_router, b_router, w_gate_up, b_gate_up, w_down, b_down):
    cond_lat = jax.nn.silu(c)
    cond_ctx = jax.nn.silu(c_ctx)
    xc = ctx
    for l in range(DEPTH):
        need_ctx = l < DEPTH - 1
        mod = (cond_lat @ w_ada[l] + b_ada[l])[:, None, :]
        mod_c = cond_ctx @ w_ada[l] + b_ada[l]
        shift1, scale1, gate1, shift2, scale2, gate2 = jnp.split(mod, 6, axis=-1)
        shift1_c, scale1_c, gate1_c, shift2_c, scale2_c, gate2_c = jnp.split(mod_c, 6, axis=-1)

        z = modulate(rms_norm(x, norm1_g[l]), shift1, scale1) @ w_in[l]
        z_c = modulate(rms_norm(xc, norm1_g[l]), shift1_c, scale1_c) @ w_in[l]

        o_a, o_a_c = mla_branch(z, z_c, mla_q_norm[l], mla_kv_norm[l], mla_w_uq[l], mla_w_ukv[l],
                                mla_q_gain[l], mla_k_gain[l], need_ctx)
        o_b, o_b_c = na_branch(z, z_c, na_q_gain[l], na_k_gain[l], na_rpb[l], need_ctx)
        o_s, o_s_c = ssm_branch(z, z_c, ssm_lam_re[l], ssm_lam_im[l], ssm_log_dt[l], ssm_b_re[l], ssm_b_im[l],
                                ssm_c_re[l], ssm_c_im[l], ssm_d[l], ssm_w_glu[l], ssm_b_glu[l], need_ctx)

        x = x + gate1 * merge_branches(z, o_a, o_b, o_s, w_pa[l], w_pb[l], w_pc[l], w_out[l])
        x = x + gate2 * moe_ffn(modulate(rms_norm(x, norm2_g[l]), shift2, scale2), w_router[l], b_router[l],
                                w_gate_up[l], b_gate_up[l], w_down[l], b_down[l])
        if need_ctx:
            xc = xc + gate1_c * merge_branches(z_c, o_a_c, o_b_c, o_s_c, w_pa[l], w_pb[l], w_pc[l], w_out[l])
            xc = xc + gate2_c * moe_ffn(modulate(rms_norm(xc, norm2_g[l]), shift2_c, scale2_c), w_router[l],
                                        b_router[l], w_gate_up[l], b_gate_up[l], w_down[l], b_down[l])
    return x
```

```python
import functools
import math

import numpy as np
import jax
import jax.numpy as jnp
from jax import lax
from jax.experimental import pallas as pl
from jax.experimental.pallas import tpu as pltpu

F32 = jnp.float32
BF16 = jnp.bfloat16

GRID_W = 64
MLA_HEADS = 8
MLA_NOPE = 128
MLA_ROPE = 64
MLA_V = 128
MLA_QK = MLA_NOPE + MLA_ROPE
MLA_Q_RANK = 768
MLA_KV_RANK = 512
NA_HEADS = 16
NA_HEAD_DIM = 64
NA_WIDTH = NA_HEADS * NA_HEAD_DIM
NA_WIN_R = 8
NA_WIN_C = 16
SSM_WIDTH = 1024
SSM_GROUP = 16
SSM_GROUPS = SSM_WIDTH // SSM_GROUP
SSM_STATE = 64
TOP_K = 4
SWIGLU_LIMIT = 7.0
SWIGLU_ALPHA = 1.702
ROPE_BASE = 10000.0
EPS = 1e-6
MLA_SCALE = MLA_QK ** -0.5
NA_SCALE = NA_HEAD_DIM ** -0.5
NEG_BIG = -1e30

LANES = 128
TOK = 256
MLA_KPAD = 256
VMEM_LIMIT = 56 * 1024 * 1024

ZC_CQ = 0
ZC_KPE = 768
ZC_CKV = 1024
ZC_NAQ = 1536
ZC_NAK = 2560
ZC_NAV = 3584
ZC_SSM = 4608
ZC_GATE = 5632
ZW = 11776

SSM_CHUNK = 16
NA_QROWS = 4
NA_KROWS = 12


def _cparams(sem, vmem=VMEM_LIMIT):
    return pltpu.CompilerParams(dimension_semantics=sem, vmem_limit_bytes=vmem)


def _rms(x, eps=EPS):
    return x * lax.rsqrt(jnp.mean(x * x, axis=-1, keepdims=True) + eps)


def _seg64_rms(t):
    lane = lax.broadcasted_iota(jnp.int32, t.shape, 1)
    left = lane < 64
    sq = t * t
    s_all = jnp.sum(sq, axis=-1, keepdims=True)
    s_left = jnp.sum(jnp.where(left, sq, 0.0), axis=-1, keepdims=True)
    ms = jnp.where(left, s_left, s_all - s_left) * (1.0 / 64.0)
    return t * lax.rsqrt(ms + EPS)


def _rope64(t, cos2, sin2):
    lane = lax.broadcasted_iota(jnp.int32, t.shape, 1)
    first = (lane % 64) < 32
    swapped = jnp.where(first, pltpu.roll(t, 96, axis=1), pltpu.roll(t, 32, axis=1))
    return t * cos2 + swapped * sin2


def _ada_kernel(c_ref, w_ref, b_ref, o_ref):
    c = c_ref[...]
    a = (c * jax.nn.sigmoid(c)).astype(BF16)
    o_ref[...] = jnp.dot(a, w_ref[...].astype(BF16), preferred_element_type=F32) + b_ref[...]


def _ada(cvec, w, b):
    rows, d = cvec.shape
    n = w.shape[1]
    tn = 1024
    return pl.pallas_call(
        _ada_kernel,
        grid=(n // tn,),
        in_specs=[pl.BlockSpec((rows, d), lambda j: (0, 0)),
                  pl.BlockSpec((d, tn), lambda j: (0, j)),
                  pl.BlockSpec((1, tn), lambda j: (0, j))],
        out_specs=pl.BlockSpec((rows, tn), lambda j: (0, j)),
        out_shape=jax.ShapeDtypeStruct((rows, n), F32),
        compiler_params=_cparams(("arbitrary",)),
        name="ada",
    )(cvec, w, b.reshape(1, n))


def _norm_mod_kernel(x_ref, g_ref, m_ref, o_ref, *, shift_row, scale_row):
    x = x_ref[0]
    m = m_ref[0, 0]
    y = _rms(x) * g_ref[...]
    o_ref[0] = (y * (1.0 + m[scale_row:scale_row + 1]) + m[shift_row:shift_row + 1]).astype(o_ref.dtype)


def _norm_mod(x, g, modsel, nct, shift_row, scale_row, out_dtype):
    b, t, d = x.shape
    return pl.pallas_call(
        functools.partial(_norm_mod_kernel, shift_row=shift_row, scale_row=scale_row),
        grid=(b, t // TOK),
        in_specs=[pl.BlockSpec((1, TOK, d), lambda i, j: (i, j, 0)),
                  pl.BlockSpec((1, d), lambda i, j: (0, 0)),
                  pl.BlockSpec((1, 1, 6, d), lambda i, j: (i, jnp.where(j >= nct, 1, 0), 0, 0))],
        out_specs=pl.BlockSpec((1, TOK, d), lambda i, j: (i, j, 0)),
        out_shape=jax.ShapeDtypeStruct((b, t, d), out_dtype),
        compiler_params=_cparams(("arbitrary", "arbitrary")),
        name="norm_mod",
    )(x, g.reshape(1, d), modsel)


def _mm_kernel(a_ref, w_ref, o_ref):
    o_ref[...] = jnp.dot(a_ref[...], w_ref[...], preferred_element_type=F32).astype(o_ref.dtype)


def _pick(n, cands):
    for c in cands:
        if n % c == 0:
            return c
    return n


def _matmul(a, w, out_dtype, name):
    m, k = a.shape
    n = w.shape[1]
    tm = _pick(m, (512, 256, 128))
    tn = _pick(n, (512, 256, 128))
    return pl.pallas_call(
        _mm_kernel,
        grid=(n // tn, m // tm),
        in_specs=[pl.BlockSpec((tm, k), lambda j, i: (i, 0)),
                  pl.BlockSpec((k, tn), lambda j, i: (0, j))],
        out_specs=pl.BlockSpec((tm, tn), lambda j, i: (i, j)),
        out_shape=jax.ShapeDtypeStruct((m, n), out_dtype),
        compiler_params=_cparams(("arbitrary", "arbitrary")),
        name=name,
    )(a, w)


def _mla_q_kernel(z_ref, g_ref, w_ref, gn_ref, gr_ref, cos_ref, sin_ref, o_ref):
    cq = z_ref[0].astype(F32)
    xn = (_rms(cq) * g_ref[...]).astype(BF16)
    q = jnp.dot(xn, w_ref[...], preferred_element_type=F32)
    lane = lax.broadcasted_iota(jnp.int32, (TOK, LANES), 1)
    left = lane < 64
    for h in range(MLA_HEADS):
        qn = q[:, h * MLA_NOPE:(h + 1) * MLA_NOPE]
        o_ref[0, h, :, 0:LANES] = (_rms(qn) * (gn_ref[...] * MLA_SCALE)).astype(o_ref.dtype)
    base = MLA_HEADS * MLA_NOPE
    for hp in range(MLA_HEADS // 2):
        t = q[:, base + hp * LANES: base + (hp + 1) * LANES]
        t = _seg64_rms(t) * (gr_ref[...] * MLA_SCALE)
        r = _rope64(t, cos_ref[...], sin_ref[...])
        o_ref[0, 2 * hp, :, LANES:2 * LANES] = jnp.where(left, r, 0.0).astype(o_ref.dtype)
        o_ref[0, 2 * hp + 1, :, LANES:2 * LANES] = jnp.where(left, pltpu.roll(r, 64, axis=1), 0.0).astype(o_ref.dtype)


def _mla_q(z3, g, w, gn, gr2, cos2, sin2):
    b, t, _ = z3.shape
    return pl.pallas_call(
        _mla_q_kernel,
        grid=(b, t // TOK),
        in_specs=[pl.BlockSpec((1, TOK, MLA_Q_RANK), lambda i, j: (i, j, ZC_CQ // MLA_Q_RANK)),
                  pl.BlockSpec((1, MLA_Q_RANK), lambda i, j: (0, 0)),
                  pl.BlockSpec(w.shape, lambda i, j: (0, 0)),
                  pl.BlockSpec((1, LANES), lambda i, j: (0, 0)),
                  pl.BlockSpec((1, LANES), lambda i, j: (0, 0)),
                  pl.BlockSpec((TOK, LANES), lambda i, j: (j, 0)),
                  pl.BlockSpec((TOK, LANES), lambda i, j: (j, 0))],
        out_specs=pl.BlockSpec((1, MLA_HEADS, TOK, MLA_KPAD), lambda i, j: (i, 0, j, 0)),
        out_shape=jax.ShapeDtypeStruct((b, MLA_HEADS, t, MLA_KPAD), BF16),
        compiler_params=_cparams(("arbitrary", "arbitrary")),
        name="mla_q",
    )(z3, g, w, gn, gr2, cos2, sin2)


def _mla_kv_kernel(z_ref, zk_ref, g_ref, w_ref, gn_ref, gr_ref, cos_ref, sin_ref, k_ref, v_ref):
    ckv = z_ref[0].astype(F32)
    xn = (_rms(ckv) * g_ref[...]).astype(BF16)
    kv = jnp.dot(xn, w_ref[...], preferred_element_type=F32)
    lane = lax.broadcasted_iota(jnp.int32, (TOK, LANES), 1)
    left = lane < 64
    t = jnp.where(left, zk_ref[0].astype(F32), 0.0)
    ms = jnp.sum(t * t, axis=-1, keepdims=True) * (1.0 / MLA_ROPE)
    kr = t * lax.rsqrt(ms + EPS) * gr_ref[...]
    kr = jnp.where(left, _rope64(kr, cos_ref[...], sin_ref[...]), 0.0).astype(k_ref.dtype)
    hw = MLA_NOPE + MLA_V
    for h in range(MLA_HEADS):
        kn = kv[:, h * hw: h * hw + MLA_NOPE]
        k_ref[0, h, :, 0:LANES] = (_rms(kn) * gn_ref[...]).astype(k_ref.dtype)
        k_ref[0, h, :, LANES:2 * LANES] = kr
        v_ref[0, h] = kv[:, h * hw + MLA_NOPE:(h + 1) * hw].astype(v_ref.dtype)


def _mla_kv(z3, g, w, gn, gr2, cos2, sin2):
    b, t, _ = z3.shape
    return pl.pallas_call(
        _mla_kv_kernel,
        grid=(b, t // TOK),
        in_specs=[pl.BlockSpec((1, TOK, MLA_KV_RANK), lambda i, j: (i, j, ZC_CKV // MLA_KV_RANK)),
                  pl.BlockSpec((1, TOK, LANES), lambda i, j: (i, j, ZC_KPE // LANES)),
                  pl.BlockSpec((1, MLA_KV_RANK), lambda i, j: (0, 0)),
                  pl.BlockSpec(w.shape, lambda i, j: (0, 0)),
                  pl.BlockSpec((1, LANES), lambda i, j: (0, 0)),
                  pl.BlockSpec((1, LANES), lambda i, j: (0, 0)),
                  pl.BlockSpec((TOK, LANES), lambda i, j: (j, 0)),
                  pl.BlockSpec((TOK, LANES), lambda i, j: (j, 0))],
        out_specs=[pl.BlockSpec((1, MLA_HEADS, TOK, MLA_KPAD), lambda i, j: (i, 0, j, 0)),
                   pl.BlockSpec((1, MLA_HEADS, TOK, MLA_V), lambda i, j: (i, 0, j, 0))],
        out_shape=[jax.ShapeDtypeStruct((b, MLA_HEADS, t, MLA_KPAD), BF16),
                   jax.ShapeDtypeStruct((b, MLA_HEADS, t, MLA_V), BF16)],
        compiler_params=_cparams(("arbitrary", "arbitrary")),
        name="mla_kv",
    )(z3, z3, g, w, gn, gr2, cos2, sin2)


def _softmax_pv(s, v):
    m = jnp.max(s, axis=-1, keepdims=True)
    p = jnp.exp(s - m)
    l = jnp.sum(p, axis=-1, keepdims=True)
    return jnp.dot(p.astype(BF16), v, preferred_element_type=F32) / l


def _qkt(q, k):
    return lax.dot_general(q, k, (((1,), (1,)), ((), ())), preferred_element_type=F32)


def _mla_attn_kernel(q_ref, k_ref, v_ref, o_ref, *, nct, ctx):
    j = pl.program_id(2)

    @pl.when(j < nct)
    def _():
        o_ref[0] = _softmax_pv(_qkt(q_ref[0, 0], k_ref[0, 0, 0:ctx, :]), v_ref[0, 0, 0:ctx, :]).astype(o_ref.dtype)

    @pl.when(j >= nct)
    def _():
        o_ref[0] = _softmax_pv(_qkt(q_ref[0, 0], k_ref[0, 0]), v_ref[0, 0]).astype(o_ref.dtype)


def _mla_attn(q, k, v, nct):
    b, h, t, _ = q.shape
    return pl.pallas_call(
        functools.partial(_mla_attn_kernel, nct=nct, ctx=nct * TOK),
        grid=(b, h, t // TOK),
        in_specs=[pl.BlockSpec((1, 1, TOK, MLA_KPAD), lambda i, hh, j: (i, hh, j, 0)),
                  pl.BlockSpec((1, 1, t, MLA_KPAD), lambda i, hh, j: (i, hh, 0, 0)),
                  pl.BlockSpec((1, 1, t, MLA_V), lambda i, hh, j: (i, hh, 0, 0))],
        out_specs=pl.BlockSpec((1, TOK, MLA_V), lambda i, hh, j: (i, j, hh)),
        out_shape=jax.ShapeDtypeStruct((b, t, h * MLA_V), BF16),
        compiler_params=_cparams(("arbitrary", "arbitrary", "arbitrary")),
        name="mla_attn",
    )(q, k, v)


def _na_kernel(q_ref, kc_ref, k0_ref, k1_ref, k2_ref, vc_ref, v0_ref, v1_ref, v2_ref,
               gq_ref, gk_ref, bias_ref, o_ref):
    j = pl.program_id(1)
    lane = lax.broadcasted_iota(jnp.int32, (TOK, LANES), 1)
    left = lane < 64
    qn = _seg64_rms(q_ref[0].astype(F32)) * (gq_ref[...] * NA_SCALE)

    def heads(kn, v, bias_of):
        outs = []
        for hh in range(2):
            mine = left if hh == 0 else jnp.logical_not(left)
            qh = jnp.where(mine, qn, 0.0).astype(BF16)
            s = _qkt(qh, kn)
            b = bias_of(hh)
            if b is not None:
                s = s + b
            outs.append(_softmax_pv(s, v))
        o_ref[0] = jnp.where(left, outs[0], outs[1]).astype(o_ref.dtype)

    @pl.when(j == 0)
    def _():
        kn = (_seg64_rms(kc_ref[0].astype(F32)) * gk_ref[...]).astype(BF16)
        heads(kn, vc_ref[0], lambda hh: None)

    @pl.when(j > 0)
    def _():
        kcat = jnp.concatenate([kc_ref[0], k0_ref[0], k1_ref[0], k2_ref[0]], axis=0).astype(F32)
        kn = (_seg64_rms(kcat) * gk_ref[...]).astype(BF16)
        vcat = jnp.concatenate([vc_ref[0], v0_ref[0], v1_ref[0], v2_ref[0]], axis=0)
        heads(kn, vcat, lambda hh: bias_ref[0, hh])


def _na_attn(z3, gq2, gk2, bias, nct, nblk):
    b, t, _ = z3.shape
    assert nct == 1
    kb = lambda j: jnp.clip(j - 2, 0, nblk - 3) + nct

    def zspec(col0, tok_idx):
        return pl.BlockSpec((1, TOK, LANES), lambda hp, j, i: (i, tok_idx(j), col0 // LANES + hp))

    var = lambda j: jnp.where(j <= 1, 0, jnp.where(j == nblk, 2, 1))
    in_specs = [zspec(ZC_NAQ, lambda j: j)]
    in_specs += [zspec(ZC_NAK, lambda j: 0)] + [zspec(ZC_NAK, (lambda j, d=d: kb(j) + d)) for d in range(3)]
    in_specs += [zspec(ZC_NAV, lambda j: 0)] + [zspec(ZC_NAV, (lambda j, d=d: kb(j) + d)) for d in range(3)]
    in_specs += [pl.BlockSpec((1, LANES), lambda hp, j, i: (0, 0)),
                 pl.BlockSpec((1, LANES), lambda hp, j, i: (0, 0)),
                 pl.BlockSpec((1, 2, TOK, 4 * TOK), lambda hp, j, i: (var(j), hp, 0, 0))]
    return pl.pallas_call(
        _na_kernel,
        grid=(NA_HEADS // 2, nblk + 1, b),
        in_specs=in_specs,
        out_specs=pl.BlockSpec((1, TOK, LANES), lambda hp, j, i: (i, j, hp)),
        out_shape=jax.ShapeDtypeStruct((b, t, NA_WIDTH), BF16),
        compiler_params=_cparams(("arbitrary", "arbitrary", "arbitrary")),
        name="na_attn",
    )(*([z3] * 9), gq2, gk2, bias)


def _na_bias_tables(rpb, rows):
    nblk = rows // NA_QROWS
    tabs = []
    for j in (0, 1, nblk - 1):
        start = int(np.clip(NA_QROWS * j - NA_QROWS, 0, rows - NA_KROWS))
        qr = NA_QROWS * j + np.arange(NA_QROWS)[:, None, None, None]
        qc = np.arange(GRID_W)[None, :, None, None]
        kr = start + np.arange(NA_KROWS)[None, None, :, None]
        kc = np.arange(GRID_W)[None, None, None, :]
        r0 = np.clip(qr - NA_WIN_R // 2, 0, rows - NA_WIN_R)
        c0 = np.clip(qc - NA_WIN_C // 2, 0, GRID_W - NA_WIN_C)
        valid = (kr >= r0) & (kr < r0 + NA_WIN_R) & (kc >= c0) & (kc < c0 + NA_WIN_C)
        drow = np.clip(kr - qr + NA_WIN_R - 1, 0, 2 * NA_WIN_R - 2) + 0 * kc + 0 * qc
        dcol = np.clip(kc - qc + NA_WIN_C - 1, 0, 2 * NA_WIN_C - 2) + 0 * kr + 0 * qr
        valid = np.broadcast_to(valid, drow.shape).reshape(TOK, NA_KROWS * GRID_W)
        drow = drow.reshape(TOK, NA_KROWS * GRID_W)
        dcol = dcol.reshape(TOK, NA_KROWS * GRID_W)
        loc = jnp.where(valid[None], rpb.astype(F32)[:, drow, dcol], NEG_BIG)
        tabs.append(jnp.concatenate([jnp.zeros((rpb.shape[0], TOK, TOK), F32), loc], axis=-1))
    return jnp.stack(tabs)


def _ssm_kernel(u_ref, wb_ref, mt_ref, wc_ref, d_ref, ar_ref, ai_ref, y_ref, sh_ref, *, gp, bsz, nchunk, nct):
    for g in range(gp):
        sh_ref[g] = jnp.dot(u_ref[g], wb_ref[g], preferred_element_type=F32)
    a_r = [ar_ref[g] for g in range(gp)]
    a_i = [ai_ref[g] for g in range(gp)]

    def body(i, carry):
        cr = jnp.where(i < nct, nct - 1 - i, nchunk - 1 - (i - nct))
        rf = pl.multiple_of(i * bsz, bsz)
        rr = pl.multiple_of(cr * bsz, bsz)
        out = []
        for g in range(gp):
            hre_f, him_f, hre_r, him_r = carry[g]
            sre_f = sh_ref[g, pl.ds(rf, bsz), 0:LANES]
            sim_f = sh_ref[g, pl.ds(rf, bsz), LANES:2 * LANES]
            sre_r = sh_ref[g, pl.ds(rr, bsz), 2 * LANES:3 * LANES]
            sim_r = sh_ref[g, pl.ds(rr, bsz), 3 * LANES:4 * LANES]
            sh_ref[g, pl.ds(rf, bsz), 0:LANES] = hre_f
            sh_ref[g, pl.ds(rf, bsz), LANES:2 * LANES] = him_f
            sh_ref[g, pl.ds(rr, bsz), 2 * LANES:3 * LANES] = hre_r
            sh_ref[g, pl.ds(rr, bsz), 3 * LANES:4 * LANES] = him_r
            arf, aif = a_r[g][:, 0:LANES], a_i[g][:, 0:LANES]
            arr, air = a_r[g][:, LANES:], a_i[g][:, LANES:]
            out.append((arf * hre_f - aif * him_f + sre_f, arf * him_f + aif * hre_f + sim_f,
                        arr * hre_r - air * him_r + sre_r, arr * him_r + air * hre_r + sim_r))
        return tuple(out)

    zero = jnp.zeros((bsz, LANES), F32)
    lax.fori_loop(0, nchunk, body, tuple((zero, zero, zero, zero) for _ in range(gp)))
    for g in range(gp):
        u = u_ref[g]
        y = jnp.dot(u, mt_ref[g], preferred_element_type=F32)
        y = y + jnp.dot(sh_ref[g].astype(BF16), wc_ref[g], preferred_element_type=F32)
        y_ref[g] = (y + d_ref[g] * u.astype(F32)).astype(y_ref.dtype)


def _ssm(u2, wb2, mt2, wc2, d2, ar2, ai2, bsz, nchunk, nct):
    npair, r, w = u2.shape
    gp = 2
    spec3 = lambda shp: pl.BlockSpec((gp,) + shp, lambda i: (i, 0, 0))
    return pl.pallas_call(
        functools.partial(_ssm_kernel, gp=gp, bsz=bsz, nchunk=nchunk, nct=nct),
        grid=(npair // gp,),
        in_specs=[spec3((r, w)), spec3((w, w)), spec3((w, w)), spec3((w, w)),
                  spec3((1, w)), spec3((1, 2 * LANES)), spec3((1, 2 * LANES))],
        out_specs=spec3((r, w)),
        out_shape=jax.ShapeDtypeStruct((npair, r, w), BF16),
        scratch_shapes=[pltpu.VMEM((gp, r, w), F32)],
        compiler_params=_cparams(("arbitrary",)),
        name="ssm",
    )(u2, wb2, mt2, wc2, d2, ar2, ai2)


def _ssm_params(lam_re, lam_im, log_dt, b_re, b_im, c_re, c_im, d_skip):
    L = SSM_CHUNK
    lam_re, lam_im = lam_re.astype(F32), lam_im.astype(F32)
    dt = jnp.exp(log_dt.astype(F32))[..., None]
    tau = jnp.arange(L + 1, dtype=F32)[:, None, None, None]
    mag = jnp.exp(lam_re * dt * tau)
    pw_re = mag * jnp.cos(lam_im * dt * tau)
    pw_im = mag * jnp.sin(lam_im * dt * tau)
    a_re, a_im = pw_re[1], pw_im[1]
    den = lam_re * lam_re + lam_im * lam_im
    z_re = ((a_re - 1.0) * lam_re + a_im * lam_im) / den
    z_im = (a_im * lam_re - (a_re - 1.0) * lam_im) / den
    bb_re = z_re[..., None] * b_re.astype(F32) - z_im[..., None] * b_im.astype(F32)
    bb_im = z_re[..., None] * b_im.astype(F32) + z_im[..., None] * b_re.astype(F32)
    cr, ci = c_re.astype(F32), c_im.astype(F32)

    ab_re = pw_re[..., None] * bb_re - pw_im[..., None] * bb_im
    ab_im = pw_re[..., None] * bb_im + pw_im[..., None] * bb_re
    ktap = (jnp.einsum('dgnp,tdgpm->tdgnm', cr, ab_re) - jnp.einsum('dgnp,tdgpm->tdgnm', ci, ab_im))[:L]
    i_idx = np.arange(L)[:, None]
    s_idx = np.arange(L)[None, :]
    lag_f = i_idx - s_idx
    lag_r = s_idx - i_idx
    kf = jnp.where((lag_f >= 0)[:, :, None, None, None], ktap[np.clip(lag_f, 0, L - 1), 0], 0.0)
    kr = jnp.where((lag_r >= 0)[:, :, None, None, None], ktap[np.clip(lag_r, 0, L - 1), 1], 0.0)
    m_tot = kf + kr
    mt = jnp.transpose(m_tot, (2, 1, 4, 0, 3)).reshape(SSM_GROUPS, L * SSM_GROUP, L * SSM_GROUP)

    exp_f = L - 1 - np.arange(L)
    exp_r = np.arange(L)

    def inj(ab, d, exps):
        return jnp.transpose(ab[exps, d], (1, 0, 3, 2)).reshape(SSM_GROUPS, L * SSM_GROUP, SSM_STATE)

    wb_parts = [inj(ab_re, 0, exp_f), inj(ab_im, 0, exp_f), inj(ab_re, 1, exp_r), inj(ab_im, 1, exp_r)]

    def ro(d, exps):
        pr, pi = pw_re[exps, d], pw_im[exps, d]
        e_re = cr[d][None] * pr[:, :, None, :] - ci[d][None] * pi[:, :, None, :]
        e_im = cr[d][None] * pi[:, :, None, :] + ci[d][None] * pr[:, :, None, :]
        f = lambda e: jnp.transpose(e, (1, 3, 0, 2)).reshape(SSM_GROUPS, SSM_STATE, L * SSM_GROUP)
        return f(e_re), f(-e_im)

    wc_parts = list(ro(0, np.arange(L) + 1)) + list(ro(1, L - np.arange(L)))

    npair = SSM_GROUPS // 2
    w1 = L * SSM_GROUP

    def pair_diag(x):
        x = x.reshape(npair, 2, x.shape[1], x.shape[2])
        zed = jnp.zeros_like(x[:, 0])
        return jnp.concatenate([jnp.concatenate([x[:, 0], zed], axis=2),
                                jnp.concatenate([zed, x[:, 1]], axis=2)], axis=1)

    wb2 = jnp.concatenate([pair_diag(p) for p in wb_parts], axis=2)
    wc2 = jnp.concatenate([pair_diag(p) for p in wc_parts], axis=1)
    mt2 = pair_diag(mt)
    d2 = jnp.tile(d_skip.astype(F32).reshape(SSM_GROUPS, 1, SSM_GROUP), (1, L, 1)).reshape(npair, 1, 2 * w1)
    pl_re, pl_im = pw_re[L], pw_im[L]
    packp = lambda x: x.reshape(2, npair, 2 * SSM_STATE)
    ar2 = jnp.concatenate([packp(pl_re)[0], packp(pl_re)[1]], axis=-1)[:, None, :]
    ai2 = jnp.concatenate([packp(pl_im)[0], packp(pl_im)[1]], axis=-1)[:, None, :]
    return wb2.astype(BF16), mt2.astype(BF16), wc2.astype(BF16), d2, ar2, ai2


def _glu_kernel(y_ref, w_ref, b_ref, o_ref):
    y = y_ref[...].astype(F32)
    g = 0.5 * y * (1.0 + jnp.tanh(math.sqrt(2.0 / math.pi) * (y + 0.044715 * (y * y * y))))
    t = jnp.dot(g.astype(BF16), w_ref[...], preferred_element_type=F32) + b_ref[...]
    o_ref[...] = (g * jax.nn.sigmoid(t)).astype(o_ref.dtype)


def _glu(y, w, b):
    m, n = y.shape
    tm = _pick(m, (512, 256, 128))
    return pl.pallas_call(
        _glu_kernel,
        grid=(m // tm,),
        in_specs=[pl.BlockSpec((tm, n), lambda i: (i, 0)),
                  pl.BlockSpec((n, n), lambda i: (0, 0)),
                  pl.BlockSpec((1, n), lambda i: (0, 0))],
        out_specs=pl.BlockSpec((tm, n), lambda i: (i, 0)),
        out_shape=jax.ShapeDtypeStruct((m, n), BF16),
        compiler_params=_cparams(("arbitrary",)),
        name="glu",
    )(y, w, b.reshape(1, n))


def _merge_kernel(oa_ref, ob_ref, os_ref, ga_ref, gb_ref, gs_ref, wa_ref, wb_ref, wc_ref, o_ref):
    acc = jax.nn.sigmoid(ga_ref[...].astype(F32)) * jnp.dot(oa_ref[...], wa_ref[...], preferred_element_type=F32)
    acc += jax.nn.sigmoid(gb_ref[...].astype(F32)) * jnp.dot(ob_ref[...], wb_ref[...], preferred_element_type=F32)
    acc += jax.nn.sigmoid(gs_ref[...].astype(F32)) * jnp.dot(os_ref[...], wc_ref[...], preferred_element_type=F32)
    o_ref[...] = acc.astype(o_ref.dtype)


def _merge(oa, ob, os_, z, wa, wb, wc):
    m, k = oa.shape
    d = wa.shape[1]
    tm = _pick(m, (512, 256, 128))
    tn = 512
    ospec = pl.BlockSpec((tm, k), lambda j, i: (i, 0))
    gspec = lambda br: pl.BlockSpec((tm, tn), lambda j, i: (i, (ZC_GATE + br * d) // tn + j))
    wspec = pl.BlockSpec((k, tn), lambda j, i: (0, j))
    return pl.pallas_call(
        _merge_kernel,
        grid=(d // tn, m // tm),
        in_specs=[ospec, ospec, ospec, gspec(0), gspec(1), gspec(2), wspec, wspec, wspec],
        out_specs=pl.BlockSpec((tm, tn), lambda j, i: (i, j)),
        out_shape=jax.ShapeDtypeStruct((m, d), BF16),
        compiler_params=_cparams(("arbitrary", "arbitrary")),
        name="merge",
    )(oa, ob, os_, z, z, z, wa, wb, wc)


def _outproj_kernel(x_ref, mix_ref, w_ref, m_ref, o_ref):
    gate = m_ref[0, 0][2:3]
    o_ref[0] = x_ref[0] + gate * jnp.dot(mix_ref[0], w_ref[...], preferred_element_type=F32)


def _outproj(x, mix3, w, modsel, nct):
    b, t, d = x.shape
    return pl.pallas_call(
        _outproj_kernel,
        grid=(b, t // TOK),
        in_specs=[pl.BlockSpec((1, TOK, d), lambda i, j: (i, j, 0)),
                  pl.BlockSpec((1, TOK, d), lambda i, j: (i, j, 0)),
                  pl.BlockSpec((d, d), lambda i, j: (0, 0)),
                  pl.BlockSpec((1, 1, 6, d), lambda i, j: (i, jnp.where(j >= nct, 1, 0), 0, 0))],
        out_specs=pl.BlockSpec((1, TOK, d), lambda i, j: (i, j, 0)),
        out_shape=jax.ShapeDtypeStruct((b, t, d), F32),
        compiler_params=_cparams(("arbitrary", "arbitrary")),
        name="outproj",
    )(x, mix3, w, modsel)


def _router_kernel(x_ref, g_ref, m_ref, w_ref, b_ref, h_ref, meta_ref, tw_ref, cnt_ref, run_ref, *, n_exp):
    first = jnp.logical_and(pl.program_id(0) == 0, pl.program_id(1) == 0)

    @pl.when(first)
    def _():
        run_ref[...] = jnp.zeros_like(run_ref)

    m = m_ref[0, 0]
    h = _rms(x_ref[0]) * g_ref[...] * (1.0 + m[4:5]) + m[3:4]
    h_ref[0] = h
    logits = jnp.dot(h, w_ref[...], preferred_element_type=F32, precision=lax.Precision.HIGHEST) + b_ref[...]
    lane = lax.broadcasted_iota(jnp.int32, (TOK, LANES), 1).astype(F32)
    lg = jnp.where(lane < n_exp, logits, -jnp.inf)
    vals, idxs = [], []
    for _ in range(TOP_K):
        mx = jnp.max(lg, axis=-1, keepdims=True)
        idx = jnp.min(jnp.where(lg == mx, lane, float(LANES)), axis=-1, keepdims=True)
        vals.append(mx)
        idxs.append(idx)
        lg = jnp.where(lane == idx, -jnp.inf, lg)
    es = [jnp.exp(v - vals[0]) for v in vals]
    den = es[0] + es[1] + es[2] + es[3]
    onehot = jnp.zeros((TOK, LANES), F32)
    for idx in idxs:
        onehot = onehot + jnp.where(lane == idx, 1.0, 0.0)
    row = lax.broadcasted_iota(jnp.int32, (TOK, TOK), 0)
    col = lax.broadcasted_iota(jnp.int32, (TOK, TOK), 1)
    tri = jnp.where(col < row, 1.0, 0.0).astype(BF16)
    before = jnp.dot(tri, onehot.astype(BF16), preferred_element_type=F32) + run_ref[...]
    meta = jnp.zeros((TOK, LANES), F32)
    tw = jnp.zeros((TOK, LANES), F32)
    for k in range(TOP_K):
        rank = jnp.sum(jnp.where(lane == idxs[k], before, 0.0), axis=-1, keepdims=True)
        meta = meta + jnp.where(lane == float(k), idxs[k], 0.0) + jnp.where(lane == float(TOP_K + k), rank, 0.0)
        tw = tw + jnp.where(lane == float(k), es[k] / den, 0.0)
    meta_ref[0] = meta.astype(jnp.int32)
    tw_ref[0] = tw
    run = run_ref[...] + jnp.sum(onehot, axis=0, keepdims=True)
    run_ref[...] = run
    cnt_ref[...] = run


def _router(x, g, modsel, w, bvec, nct, n_exp):
    b, t, d = x.shape
    tokspec = lambda last: pl.BlockSpec((1, TOK, last), lambda i, j: (i, j, 0))
    return pl.pallas_call(
        functools.partial(_router_kernel, n_exp=n_exp),
        grid=(b, t // TOK),
        in_specs=[tokspec(d),
                  pl.BlockSpec((1, d), lambda i, j: (0, 0)),
                  pl.BlockSpec((1, 1, 6, d), lambda i, j: (i, jnp.where(j >= nct, 1, 0), 0, 0)),
                  pl.BlockSpec((d, LANES), lambda i, j: (0, 0)),
                  pl.BlockSpec((1, LANES), lambda i, j: (0, 0))],
        out_specs=[tokspec(d), tokspec(LANES), tokspec(LANES), pl.BlockSpec((1, LANES), lambda i, j: (0, 0))],
        out_shape=[jax.ShapeDtypeStruct((b, t, d), F32),
                   jax.ShapeDtypeStruct((b, t, LANES), jnp.int32),
                   jax.ShapeDtypeStruct((b, t, LANES), F32),
                   jax.ShapeDtypeStruct((1, LANES), F32)],
        scratch_shapes=[pltpu.VMEM((1, LANES), F32)],
        compiler_params=_cparams(("arbitrary", "arbitrary")),
        name="router",
    )(x, g, modsel, w, bvec)


def _dispatch_kernel(pos_ref, h_ref, xs_in_ref, xs_ref, sem):
    del xs_in_ref

    def row_copy(r, k):
        return pltpu.make_async_copy(h_ref.at[pl.ds(r, 1)], xs_ref.at[pl.ds(pos_ref[r * TOP_K + k], 1)], sem)

    def start(r, c):
        for k in range(TOP_K):
            row_copy(r, k).start()
        return c

    def wait(r, c):
        for k in range(TOP_K):
            row_copy(r, k).wait()
        return c

    lax.fori_loop(0, TOK, start, 0)
    lax.fori_loop(0, TOK, wait, 0)


def _dispatch(pos, h2, xs0):
    m, d = h2.shape
    return pl.pallas_call(
        _dispatch_kernel,
        grid=(m // TOK,),
        in_specs=[pl.BlockSpec((TOK * TOP_K,), lambda i: (i,), memory_space=pltpu.SMEM),
                  pl.BlockSpec((TOK, d), lambda i: (i, 0)),
                  pl.BlockSpec(memory_space=pl.ANY)],
        out_specs=pl.BlockSpec(memory_space=pl.ANY),
        out_shape=jax.ShapeDtypeStruct(xs0.shape, xs0.dtype),
        scratch_shapes=[pltpu.SemaphoreType.DMA(())],
        input_output_aliases={2: 0},
        compiler_params=_cparams(("arbitrary",)),
        name="dispatch",
    )(pos, h2, xs0)


def _experts_kernel(te_ref, tv_ref, x_ref, wgu_ref, bgu_ref, wd_ref, bd_ref, y_ref, *, d_ff, fchunk):
    i = pl.program_id(0)

    @pl.when(tv_ref[i] == 0)
    def _():
        y_ref[...] = jnp.zeros_like(y_ref)

    @pl.when(tv_ref[i] != 0)
    def _():
        x = x_ref[...].astype(BF16)
        acc = jnp.zeros(y_ref.shape, F32)
        for c in range(d_ff // fchunk):
            lo = c * fchunk
            g = jnp.dot(x, wgu_ref[0, :, lo:lo + fchunk], preferred_element_type=F32) + bgu_ref[0, :, lo:lo + fchunk]
            lin = (jnp.dot(x, wgu_ref[0, :, d_ff + lo:d_ff + lo + fchunk], preferred_element_type=F32)
                   + bgu_ref[0, :, d_ff + lo:d_ff + lo + fchunk])
            g = jnp.minimum(g, SWIGLU_LIMIT)
            lin = jnp.clip(lin, -SWIGLU_LIMIT, SWIGLU_LIMIT)
            act = g * jax.nn.sigmoid(SWIGLU_ALPHA * g) * (lin + 1.0)
            acc = acc + jnp.dot(act.astype(BF16), wd_ref[0, lo:lo + fchunk, :], preferred_element_type=F32)
        y_ref[...] = acc + bd_ref[0]


def _experts(te, tv, xs, wgu, bgu, wd, bd):
    rows, d = xs.shape
    n_exp, _, ff2 = wgu.shape
    d_ff = ff2 // 2
    grid_spec = pltpu.PrefetchScalarGridSpec(
        num_scalar_prefetch=2,
        grid=(rows // TOK,),
        in_specs=[pl.BlockSpec((TOK, d), lambda i, te, tv: (i, 0)),
                  pl.BlockSpec((1, d, ff2), lambda i, te, tv: (te[i], 0, 0)),
                  pl.BlockSpec((1, 1, ff2), lambda i, te, tv: (te[i], 0, 0)),
                  pl.BlockSpec((1, d_ff, d), lambda i, te, tv: (te[i], 0, 0)),
                  pl.BlockSpec((1, 1, d), lambda i, te, tv: (te[i], 0, 0))],
        out_specs=pl.BlockSpec((TOK, d), lambda i, te, tv: (i, 0)),
    )
    return pl.pallas_call(
        functools.partial(_experts_kernel, d_ff=d_ff, fchunk=512),
        grid_spec=grid_spec,
        out_shape=jax.ShapeDtypeStruct((rows, d), F32),
        compiler_params=_cparams(("arbitrary",)),
        name="experts",
    )(te, tv, xs, wgu, bgu.reshape(n_exp, 1, ff2), wd, bd.reshape(n_exp, 1, d))


def _combine_kernel(pos_ref, x_ref, tw_ref, m_ref, ys_ref, o_ref, buf_ref, sem):
    def row_copy(r, k):
        return pltpu.make_async_copy(ys_ref.at[pl.ds(pos_ref[r * TOP_K + k], 1)], buf_ref.at[k, pl.ds(r, 1)], sem)

    def start(r, c):
        for k in range(TOP_K):
            row_copy(r, k).start()
        return c

    def wait(r, c):
        for k in range(TOP_K):
            row_copy(r, k).wait()
        return c

    lax.fori_loop(0, TOK, start, 0)
    lax.fori_loop(0, TOK, wait, 0)
    tw = tw_ref[0]
    acc = tw[:, 0:1] * buf_ref[0]
    for k in range(1, TOP_K):
        acc = acc + tw[:, k:k + 1] * buf_ref[k]
    o_ref[0] = x_ref[0] + m_ref[0, 0][5:6] * acc


def _combine(pos, x, tw, modsel, ys, nct):
    b, t, d = x.shape
    nt = t // TOK
    return pl.pallas_call(
        _combine_kernel,
        grid=(b, nt),
        in_specs=[pl.BlockSpec((TOK * TOP_K,), lambda i, j: (i * nt + j,), memory_space=pltpu.SMEM),
                  pl.BlockSpec((1, TOK, d), lambda i, j: (i, j, 0)),
                  pl.BlockSpec((1, TOK, LANES), lambda i, j: (i, j, 0)),
                  pl.BlockSpec((1, 1, 6, d), lambda i, j: (i, jnp.where(j >= nct, 1, 0), 0, 0)),
                  pl.BlockSpec(memory_space=pl.ANY)],
        out_specs=pl.BlockSpec((1, TOK, d), lambda i, j: (i, j, 0)),
        out_shape=jax.ShapeDtypeStruct((b, t, d), F32),
        scratch_shapes=[pltpu.VMEM((TOP_K, TOK, d), F32), pltpu.SemaphoreType.DMA(())],
        compiler_params=_cparams(("arbitrary", "arbitrary")),
        name="combine",
    )(pos, x, tw, modsel, ys)


def _moe(xa, norm_g, modsel, w_router, b_router, wgu, bgu, wd, bd, nct):
    b, t, d = xa.shape
    m = b * t
    n_exp = w_router.shape[1]
    wr = jnp.zeros((d, LANES), F32).at[:, :n_exp].set(w_router.astype(F32))
    br = jnp.zeros((1, LANES), F32).at[0, :n_exp].set(b_router.astype(F32))
    h2, meta, tw, cnt = _router(xa, norm_g.reshape(1, d), modsel, wr, br, nct, n_exp)

    counts = cnt[0, :n_exp].astype(jnp.int32)
    padded = ((counts + TOK - 1) // TOK) * TOK
    gend = jnp.cumsum(padded)
    gstart = gend - padded
    meta2 = meta.reshape(m, LANES)
    pos = (gstart[meta2[:, 0:TOP_K]] + meta2[:, TOP_K:2 * TOP_K]).reshape(m * TOP_K)
    ntiles = (m * TOP_K) // TOK + n_exp
    tile_start = jnp.arange(ntiles, dtype=jnp.int32) * TOK
    te = jnp.minimum(jnp.searchsorted(gend, tile_start, side='right'), n_exp - 1).astype(jnp.int32)
    tv = (tile_start < gend[-1]).astype(jnp.int32)

    xs = _dispatch(pos, h2.reshape(m, d), jnp.zeros((ntiles * TOK, d), F32))
    ys = _experts(te, tv, xs, wgu, bgu, wd, bd)
    return _combine(pos, xa, tw, modsel, ys, nct)


def _rope_tables(seq, ctx):
    t = jnp.arange(seq, dtype=jnp.int32)
    row = (t // GRID_W).astype(F32)
    col = (t % GRID_W).astype(F32)
    n_freq = MLA_ROPE // 4
    inv_freq = ROPE_BASE ** (-jnp.arange(n_freq, dtype=F32) / n_freq)
    ang = jnp.concatenate([row[:, None] * inv_freq, col[:, None] * inv_freq], axis=-1)
    cos, sin = jnp.cos(ang), jnp.sin(ang)
    cos2 = jnp.tile(jnp.concatenate([cos, cos], axis=-1), (1, 2))
    sin2 = jnp.tile(jnp.concatenate([-sin, sin], axis=-1), (1, 2))
    cos2 = jnp.concatenate([jnp.ones((ctx, LANES), F32), cos2], axis=0)
    sin2 = jnp.concatenate([jnp.zeros((ctx, LANES), F32), sin2], axis=0)
    return cos2, sin2


def _pad_cols(w, width):
    return jnp.pad(w, ((0, 0), (0, width - w.shape[1])))


def _layout_w_in(w):
    o_ckv = MLA_Q_RANK
    o_kpe = o_ckv + MLA_KV_RANK
    o_naq = o_kpe + MLA_ROPE
    parts = [w[:, :o_ckv],
             _pad_cols(w[:, o_kpe:o_naq], ZC_CKV - ZC_KPE),
             w[:, o_ckv:o_kpe],
             w[:, o_naq:]]
    out = jnp.concatenate(parts, axis=1)
    assert out.shape[1] == ZW
    return out.astype(BF16)


def kernel(x, c, ctx, c_ctx, w_ada, b_ada, norm1_g, w_in, mla_q_norm, mla_kv_norm, mla_w_uq, mla_w_ukv, mla_q_gain, mla_k_gain, na_q_gain, na_k_gain, na_rpb, ssm_lam_re, ssm_lam_im, ssm_log_dt, ssm_b_re, ssm_b_im, ssm_c_re, ssm_c_im, ssm_d, ssm_w_glu, ssm_b_glu, w_pa, w_pb, w_pc, w_out, norm2_g, w_router, b_router, w_gate_up, b_gate_up, w_down, b_down):
    bsz, seq, d = x.shape
    nctx = ctx.shape[1]
    depth = w_ada.shape[0]
    assert nctx == TOK and seq % (NA_QROWS * GRID_W) == 0 and seq // GRID_W >= NA_KROWS
    nct = nctx // TOK
    t = nctx + seq
    m = bsz * t
    rows = seq // GRID_W
    nblk = rows // NA_QROWS
    nchunk = t // SSM_CHUNK
    npair = SSM_GROUPS // 2

    cos2, sin2 = _rope_tables(seq, nctx)
    cvec = jnp.zeros((16, d), F32).at[:bsz].set(c.astype(F32)).at[bsz].set(c_ctx.astype(F32))
    xa = jnp.concatenate([ctx, x], axis=1).astype(F32)

    for l in range(depth):
        mod = _ada(cvec, w_ada[l], b_ada[l])
        mod_lat = mod[:bsz].reshape(bsz, 1, 6, d)
        mod_ctx = jnp.broadcast_to(mod[bsz].reshape(1, 1, 6, d), (bsz, 1, 6, d))
        modsel = jnp.concatenate([mod_ctx, mod_lat], axis=1)

        h1 = _norm_mod(xa, norm1_g[l], modsel, nct, 0, 1, BF16)
        z = _matmul(h1.reshape(m, d), _layout_w_in(w_in[l]), BF16, "in_proj")
        z3 = z.reshape(bsz, t, ZW)

        wq = mla_w_uq[l].reshape(MLA_Q_RANK, MLA_HEADS, MLA_QK)
        wq = jnp.concatenate([wq[:, :, :MLA_NOPE].reshape(MLA_Q_RANK, -1),
                              wq[:, :, MLA_NOPE:].reshape(MLA_Q_RANK, -1)], axis=1).astype(BF16)
        row128 = lambda v: v.astype(F32).reshape(1, LANES)
        qg_r = jnp.tile(mla_q_gain[l][MLA_NOPE:], 2)
        kg_r = jnp.concatenate([mla_k_gain[l][MLA_NOPE:], jnp.zeros((MLA_ROPE,), mla_k_gain.dtype)])
        qf = _mla_q(z3, mla_q_norm[l].astype(F32).reshape(1, -1), wq, row128(mla_q_gain[l][:MLA_NOPE]),
                    row128(qg_r), cos2, sin2)
        kf, vf = _mla_kv(z3, mla_kv_norm[l].astype(F32).reshape(1, -1), mla_w_ukv[l].astype(BF16),
                         row128(mla_k_gain[l][:MLA_NOPE]), row128(kg_r), cos2, sin2)
        o_a = _mla_attn(qf, kf, vf, nct)

        bias = _na_bias_tables(na_rpb[l], rows)
        o_b = _na_attn(z3, row128(jnp.tile(na_q_gain[l], 2)), row128(jnp.tile(na_k_gain[l], 2)), bias, nct, nblk)

        wb2, mt2, wc2, d2, ar2, ai2 = _ssm_params(ssm_lam_re[l], ssm_lam_im[l], ssm_log_dt[l], ssm_b_re[l],
                                                   ssm_b_im[l], ssm_c_re[l], ssm_c_im[l], ssm_d[l])
        u = z3[:, :, ZC_SSM:ZC_SSM + SSM_WIDTH].reshape(bsz, nchunk, SSM_CHUNK, npair, 2, SSM_GROUP)
        u2 = jnp.transpose(u, (3, 1, 0, 4, 2, 5)).reshape(npair, nchunk * bsz, 2 * SSM_CHUNK * SSM_GROUP)
        y2 = _ssm(u2, wb2, mt2, wc2, d2, ar2, ai2, bsz, nchunk, nct * (TOK // SSM_CHUNK))
        y = jnp.transpose(y2.reshape(npair, nchunk, bsz, 2, SSM_CHUNK, SSM_GROUP), (2, 1, 4, 0, 3, 5))
        o_s = _glu(y.reshape(m, SSM_WIDTH), ssm_w_glu[l].astype(BF16), ssm_b_glu[l].astype(F32))

        mix = _merge(o_a.reshape(m, -1), o_b.reshape(m, -1), o_s, z, w_pa[l].astype(BF16), w_pb[l].astype(BF16),
                     w_pc[l].astype(BF16))
        xa = _outproj(xa, mix.reshape(bsz, t, d), w_out[l].astype(BF16), modsel, nct)
        xa = _moe(xa, norm2_g[l].astype(F32), modsel, w_router[l], b_router[l], w_gate_up[l].astype(BF16),
                  b_gate_up[l].astype(F32), w_down[l].astype(BF16), b_down[l].astype(F32), nct)
    return xa[:, nctx:].astype(x.dtype)
```

```python
import functools
import math

import numpy as np
import jax
import jax.numpy as jnp
from jax import lax
from jax.experimental import pallas as pl
from jax.experimental.pallas import tpu as pltpu

F32 = jnp.float32
BF16 = jnp.bfloat16

GRID_W = 64
MLA_HEADS = 8
MLA_NOPE = 128
MLA_ROPE = 64
MLA_V = 128
MLA_QK = MLA_NOPE + MLA_ROPE
MLA_Q_RANK = 768
MLA_KV_RANK = 512
NA_HEADS = 16
NA_HEAD_DIM = 64
NA_WIDTH = NA_HEADS * NA_HEAD_DIM
NA_WIN_R = 8
NA_WIN_C = 16
SSM_WIDTH = 1024
SSM_GROUP = 16
SSM_GROUPS = SSM_WIDTH // SSM_GROUP
SSM_STATE = 64
TOP_K = 4
SWIGLU_LIMIT = 7.0
SWIGLU_ALPHA = 1.702
ROPE_BASE = 10000.0
EPS = 1e-6
MLA_SCALE = MLA_QK ** -0.5
NA_SCALE = NA_HEAD_DIM ** -0.5
NEG_BIG = -1e30

LANES = 128
TOK = 256
MLA_KPAD = 256
MLA_VPAD = 256
MLA_TQ = 512
MLA_TK = 512
VMEM_LIMIT = 56 * 1024 * 1024

ZC_CQ = 0
ZC_KPE = 768
ZC_CKV = 1024
ZC_NAQ = 1536
ZC_NAK = 2560
ZC_NAV = 3584
ZC_SSM = 4608
ZC_GATE = 5632
ZW = 11776

SSM_CHUNK = 16
SSM_SUPER = LANES // SSM_GROUP
SSM_NSUPER = SSM_GROUPS // SSM_SUPER
SSM_CW = SSM_CHUNK * LANES
SSM_QW = 4 * LANES
NA_QROWS = 4
NA_KROWS = 12


def _cparams(sem, vmem=VMEM_LIMIT):
    return pltpu.CompilerParams(dimension_semantics=sem, vmem_limit_bytes=vmem)


def _rms(x, eps=EPS):
    return x * lax.rsqrt(jnp.mean(x * x, axis=-1, keepdims=True) + eps)


def _seg64_rms(t):
    lane = lax.broadcasted_iota(jnp.int32, t.shape, 1)
    left = lane < 64
    sq = t * t
    s_all = jnp.sum(sq, axis=-1, keepdims=True)
    s_left = jnp.sum(jnp.where(left, sq, 0.0), axis=-1, keepdims=True)
    ms = jnp.where(left, s_left, s_all - s_left) * (1.0 / 64.0)
    return t * lax.rsqrt(ms + EPS)


def _rope64(t, cos2, sin2):
    lane = lax.broadcasted_iota(jnp.int32, t.shape, 1)
    first = (lane % 64) < 32
    swapped = jnp.where(first, pltpu.roll(t, 96, axis=1), pltpu.roll(t, 32, axis=1))
    return t * cos2 + swapped * sin2


def _ones_col_tile(rows, dtype):
    lane = lax.broadcasted_iota(jnp.int32, (rows, LANES), 1)
    return jnp.where(lane == 0, 1.0, 0.0).astype(dtype)


def _qkt(q, k):
    return lax.dot_general(q, k, (((1,), (1,)), ((), ())), preferred_element_type=F32)


def _softmax_pv(s, v_ext):
    m = jnp.max(s, axis=-1, keepdims=True)
    p = jnp.exp((s - m).astype(BF16))
    oe = jnp.dot(p, v_ext, preferred_element_type=F32)
    return oe[:, 0:LANES] / oe[:, LANES:LANES + 1]


def _type_idx(j, nlt):
    return jnp.where(j >= nlt, 1, 0)


def _ada_kernel(c_ref, w_ref, b_ref, o_ref):
    c = c_ref[...]
    a = (c * jax.nn.sigmoid(c)).astype(BF16)
    o_ref[...] = jnp.dot(a, w_ref[...].astype(BF16), preferred_element_type=F32) + b_ref[...]


def _ada(cvec, w, b):
    rows, d = cvec.shape
    n = w.shape[1]
    tn = 1024
    return pl.pallas_call(
        _ada_kernel,
        grid=(n // tn,),
        in_specs=[pl.BlockSpec((rows, d), lambda j: (0, 0)),
                  pl.BlockSpec((d, tn), lambda j: (0, j)),
                  pl.BlockSpec((1, tn), lambda j: (0, j))],
        out_specs=pl.BlockSpec((rows, tn), lambda j: (0, j)),
        out_shape=jax.ShapeDtypeStruct((rows, n), F32),
        compiler_params=_cparams(("arbitrary",)),
        name="ada",
    )(cvec, w, b.reshape(1, n))


def _norm_mod_kernel(x_ref, g_ref, m_ref, o_ref):
    m = m_ref[0, 0]
    y = _rms(x_ref[0]) * g_ref[...]
    o_ref[0] = (y * (1.0 + m[1:2]) + m[0:1]).astype(o_ref.dtype)


def _norm_mod(x, g, modsel, nlt):
    b, t, d = x.shape
    return pl.pallas_call(
        _norm_mod_kernel,
        grid=(b, t // TOK),
        in_specs=[pl.BlockSpec((1, TOK, d), lambda i, j: (i, j, 0)),
                  pl.BlockSpec((1, d), lambda i, j: (0, 0)),
                  pl.BlockSpec((1, 1, 6, d), lambda i, j: (i, _type_idx(j, nlt), 0, 0))],
        out_specs=pl.BlockSpec((1, TOK, d), lambda i, j: (i, j, 0)),
        out_shape=jax.ShapeDtypeStruct((b, t, d), BF16),
        compiler_params=_cparams(("arbitrary", "arbitrary")),
        name="norm_mod",
    )(x, g.reshape(1, d), modsel)


def _mm_kernel(a_ref, w_ref, o_ref):
    o_ref[...] = jnp.dot(a_ref[...], w_ref[...], preferred_element_type=F32).astype(o_ref.dtype)


def _pick(n, cands):
    for c in cands:
        if n % c == 0:
            return c
    return n


def _matmul(a, w, out_dtype, name, tn):
    m, k = a.shape
    n = w.shape[1]
    tm = _pick(m, (512, 256, 128))
    return pl.pallas_call(
        _mm_kernel,
        grid=(n // tn, m // tm),
        in_specs=[pl.BlockSpec((tm, k), lambda j, i: (i, 0)),
                  pl.BlockSpec((k, tn), lambda j, i: (0, j))],
        out_specs=pl.BlockSpec((tm, tn), lambda j, i: (i, j)),
        out_shape=jax.ShapeDtypeStruct((m, n), out_dtype),
        compiler_params=_cparams(("arbitrary", "arbitrary")),
        name=name,
    )(a, w)


def _mla_q_kernel(z_ref, g_ref, w_ref, gn_ref, gr_ref, cos_ref, sin_ref, o_ref):
    cq = z_ref[0].astype(F32)
    xn = (_rms(cq) * g_ref[...]).astype(BF16)
    q = jnp.dot(xn, w_ref[...], preferred_element_type=F32)
    lane = lax.broadcasted_iota(jnp.int32, (TOK, LANES), 1)
    left = lane < 64
    for h in range(MLA_HEADS):
        qn = q[:, h * MLA_NOPE:(h + 1) * MLA_NOPE]
        o_ref[0, h, :, 0:LANES] = (_rms(qn) * (gn_ref[...] * MLA_SCALE)).astype(o_ref.dtype)
    base = MLA_HEADS * MLA_NOPE
    for hp in range(MLA_HEADS // 2):
        t = q[:, base + hp * LANES: base + (hp + 1) * LANES]
        t = _seg64_rms(t) * (gr_ref[...] * MLA_SCALE)
        r = _rope64(t, cos_ref[...], sin_ref[...])
        o_ref[0, 2 * hp, :, LANES:2 * LANES] = jnp.where(left, r, 0.0).astype(o_ref.dtype)
        o_ref[0, 2 * hp + 1, :, LANES:2 * LANES] = jnp.where(left, pltpu.roll(r, 64, axis=1), 0.0).astype(o_ref.dtype)


def _mla_q(z3, g, w, gn, gr2, cos2, sin2, ntile):
    b, t, _ = z3.shape
    return pl.pallas_call(
        _mla_q_kernel,
        grid=(b, ntile),
        in_specs=[pl.BlockSpec((1, TOK, MLA_Q_RANK), lambda i, j: (i, j, ZC_CQ // MLA_Q_RANK)),
                  pl.BlockSpec((1, MLA_Q_RANK), lambda i, j: (0, 0)),
                  pl.BlockSpec(w.shape, lambda i, j: (0, 0)),
                  pl.BlockSpec((1, LANES), lambda i, j: (0, 0)),
                  pl.BlockSpec((1, LANES), lambda i, j: (0, 0)),
                  pl.BlockSpec((TOK, LANES), lambda i, j: (j, 0)),
                  pl.BlockSpec((TOK, LANES), lambda i, j: (j, 0))],
        out_specs=pl.BlockSpec((1, MLA_HEADS, TOK, MLA_KPAD), lambda i, j: (i, 0, j, 0)),
        out_shape=jax.ShapeDtypeStruct((b, MLA_HEADS, ntile * TOK, MLA_KPAD), BF16),
        compiler_params=_cparams(("arbitrary", "arbitrary")),
        name="mla_q",
    )(z3, g, w, gn, gr2, cos2, sin2)


def _mla_kv_kernel(z_ref, zk_ref, g_ref, w_ref, gn_ref, gr_ref, cos_ref, sin_ref, k_ref, v_ref):
    ckv = z_ref[0].astype(F32)
    xn = (_rms(ckv) * g_ref[...]).astype(BF16)
    kv = jnp.dot(xn, w_ref[...], preferred_element_type=F32)
    lane = lax.broadcasted_iota(jnp.int32, (TOK, LANES), 1)
    left = lane < 64
    t = jnp.where(left, zk_ref[0].astype(F32), 0.0)
    ms = jnp.sum(t * t, axis=-1, keepdims=True) * (1.0 / MLA_ROPE)
    kr = t * lax.rsqrt(ms + EPS) * gr_ref[...]
    kr = jnp.where(left, _rope64(kr, cos_ref[...], sin_ref[...]), 0.0).astype(k_ref.dtype)
    ones = _ones_col_tile(TOK, v_ref.dtype)
    hw = MLA_NOPE + MLA_V
    for h in range(MLA_HEADS):
        kn = kv[:, h * hw: h * hw + MLA_NOPE]
        k_ref[0, h, :, 0:LANES] = (_rms(kn) * gn_ref[...]).astype(k_ref.dtype)
        k_ref[0, h, :, LANES:2 * LANES] = kr
        v_ref[0, h, :, 0:LANES] = kv[:, h * hw + MLA_NOPE:(h + 1) * hw].astype(v_ref.dtype)
        v_ref[0, h, :, LANES:2 * LANES] = ones


def _mla_kv(z3, g, w, gn, gr2, cos2, sin2):
    b, t, _ = z3.shape
    return pl.pallas_call(
        _mla_kv_kernel,
        grid=(b, t // TOK),
        in_specs=[pl.BlockSpec((1, TOK, MLA_KV_RANK), lambda i, j: (i, j, ZC_CKV // MLA_KV_RANK)),
                  pl.BlockSpec((1, TOK, LANES), lambda i, j: (i, j, ZC_KPE // LANES)),
                  pl.BlockSpec((1, MLA_KV_RANK), lambda i, j: (0, 0)),
                  pl.BlockSpec(w.shape, lambda i, j: (0, 0)),
                  pl.BlockSpec((1, LANES), lambda i, j: (0, 0)),
                  pl.BlockSpec((1, LANES), lambda i, j: (0, 0)),
                  pl.BlockSpec((TOK, LANES), lambda i, j: (j, 0)),
                  pl.BlockSpec((TOK, LANES), lambda i, j: (j, 0))],
        out_specs=[pl.BlockSpec((1, MLA_HEADS, TOK, MLA_KPAD), lambda i, j: (i, 0, j, 0)),
                   pl.BlockSpec((1, MLA_HEADS, TOK, MLA_VPAD), lambda i, j: (i, 0, j, 0))],
        out_shape=[jax.ShapeDtypeStruct((b, MLA_HEADS, t, MLA_KPAD), BF16),
                   jax.ShapeDtypeStruct((b, MLA_HEADS, t, MLA_VPAD), BF16)],
        compiler_params=_cparams(("arbitrary", "arbitrary")),
        name="mla_kv",
    )(z3, z3, g, w, gn, gr2, cos2, sin2)


def _mla_attn_kernel(q_ref, k_ref, v_ref, o_ref, *, nk, tk):
    q = q_ref[0, 0]
    m = acc = None
    for c0 in range(0, nk, tk):
        sz = min(tk, nk - c0)
        s = _qkt(q, k_ref[0, 0, c0:c0 + sz, :])
        cm = jnp.max(s, axis=-1, keepdims=True)
        m_new = cm if m is None else jnp.maximum(m, cm)
        pv = jnp.dot(jnp.exp((s - m_new).astype(BF16)), v_ref[0, 0, c0:c0 + sz, :], preferred_element_type=F32)
        acc = pv if m is None else acc * jnp.exp(m - m_new) + pv
        m = m_new
    o_ref[0] = (acc[:, 0:LANES] / acc[:, LANES:LANES + 1]).astype(o_ref.dtype)


def _mla_attn_ctx_kernel(q_ref, k_ref, v_ref, o_in_ref, o_ref):
    del o_in_ref
    o_ref[0] = _softmax_pv(_qkt(q_ref[0, 0], k_ref[0, 0]), v_ref[0, 0]).astype(o_ref.dtype)


def _mla_attn(q, k, v, seq, need_ctx):
    b, h, t, _ = k.shape
    tq = _pick(seq, (MLA_TQ, TOK))
    sem = ("arbitrary", "arbitrary", "arbitrary")
    o = pl.pallas_call(
        functools.partial(_mla_attn_kernel, nk=t, tk=MLA_TK),
        grid=(b, h, seq // tq),
        in_specs=[pl.BlockSpec((1, 1, tq, MLA_KPAD), lambda i, hh, j: (i, hh, j, 0)),
                  pl.BlockSpec((1, 1, t, MLA_KPAD), lambda i, hh, j: (i, hh, 0, 0)),
                  pl.BlockSpec((1, 1, t, MLA_VPAD), lambda i, hh, j: (i, hh, 0, 0))],
        out_specs=pl.BlockSpec((1, tq, MLA_V), lambda i, hh, j: (i, j, hh)),
        out_shape=jax.ShapeDtypeStruct((b, t, h * MLA_V), BF16),
        compiler_params=_cparams(sem),
        name="mla_attn",
    )(q, k, v)
    if not need_ctx:
        return o
    nlt = seq // TOK
    nctx = t - seq
    ctx_spec = lambda w: pl.BlockSpec((1, 1, nctx, w), lambda i, hh, j: (i, hh, seq // nctx, 0))
    return pl.pallas_call(
        _mla_attn_ctx_kernel,
        grid=(b, h, 1),
        in_specs=[ctx_spec(MLA_KPAD), ctx_spec(MLA_KPAD), ctx_spec(MLA_VPAD), pl.BlockSpec(memory_space=pl.ANY)],
        out_specs=pl.BlockSpec((1, nctx, MLA_V), lambda i, hh, j: (i, seq // nctx, hh)),
        out_shape=jax.ShapeDtypeStruct((b, t, h * MLA_V), BF16),
        input_output_aliases={3: 0},
        compiler_params=_cparams(sem),
        name="mla_attn_ctx",
    )(q, k, v, o)


def _na_kernel(q_ref, k0_ref, k1_ref, k2_ref, kc_ref, v0_ref, v1_ref, v2_ref, vc_ref,
               gq_ref, gk_ref, bias_ref, o_ref, *, nblk):
    j = pl.program_id(1)
    lane = lax.broadcasted_iota(jnp.int32, (TOK, LANES), 1)
    left = lane < 64
    qn = _seg64_rms(q_ref[0].astype(F32)) * (gq_ref[...] * NA_SCALE)

    def heads(kn, v_ext, bias_of):
        outs = []
        for hh in range(2):
            mine = left if hh == 0 else jnp.logical_not(left)
            s = _qkt(jnp.where(mine, qn, 0.0).astype(BF16), kn)
            b = bias_of(hh)
            if b is not None:
                s = s + b
            outs.append(_softmax_pv(s, v_ext))
        o_ref[0] = jnp.where(left, outs[0], outs[1]).astype(o_ref.dtype)

    @pl.when(j >= nblk)
    def _():
        kn = (_seg64_rms(kc_ref[0].astype(F32)) * gk_ref[...]).astype(BF16)
        v_ext = jnp.concatenate([vc_ref[0], _ones_col_tile(TOK, BF16)], axis=1)
        heads(kn, v_ext, lambda hh: None)

    @pl.when(j < nblk)
    def _():
        kcat = jnp.concatenate([k0_ref[0], k1_ref[0], k2_ref[0], kc_ref[0]], axis=0).astype(F32)
        kn = (_seg64_rms(kcat) * gk_ref[...]).astype(BF16)
        vcat = jnp.concatenate([v0_ref[0], v1_ref[0], v2_ref[0], vc_ref[0]], axis=0)
        v_ext = jnp.concatenate([vcat, _ones_col_tile(4 * TOK, BF16)], axis=1)
        heads(kn, v_ext, lambda hh: bias_ref[0, hh])


def _na_attn(z3, gq2, gk2, bias, nblk, need_ctx):
    b, t, _ = z3.shape
    kb = lambda j: jnp.clip(j - 1, 0, nblk - 3)
    nj = nblk + (1 if need_ctx else 0)

    def zspec(col0, tok_idx):
        return pl.BlockSpec((1, TOK, LANES), lambda hp, j, i: (i, tok_idx(j), col0 // LANES + hp))

    var = lambda j: jnp.where(j == 0, 0, jnp.where(j == nblk - 1, 2, 1))
    kv_specs = lambda col0: ([zspec(col0, (lambda j, d=d: kb(j) + d)) for d in range(3)] + [zspec(col0, lambda j: nblk)])
    in_specs = [zspec(ZC_NAQ, lambda j: j)] + kv_specs(ZC_NAK) + kv_specs(ZC_NAV)
    in_specs += [pl.BlockSpec((1, LANES), lambda hp, j, i: (0, 0)),
                 pl.BlockSpec((1, LANES), lambda hp, j, i: (0, 0)),
                 pl.BlockSpec((1, 2, TOK, 4 * TOK), lambda hp, j, i: (var(j), hp, 0, 0))]
    return pl.pallas_call(
        functools.partial(_na_kernel, nblk=nblk),
        grid=(NA_HEADS // 2, nj, b),
        in_specs=in_specs,
        out_specs=pl.BlockSpec((1, TOK, LANES), lambda hp, j, i: (i, j, hp)),
        out_shape=jax.ShapeDtypeStruct((b, t, NA_WIDTH), BF16),
        compiler_params=_cparams(("arbitrary", "arbitrary", "arbitrary")),
        name="na_attn",
    )(*([z3] * 9), gq2, gk2, bias)


def _na_bias_tables(rpb, rows):
    nblk = rows // NA_QROWS
    nh = rpb.shape[0]
    ncol = 2 * NA_WIN_C - 1
    dd = np.arange(ncol)[:, None, None]
    qc1 = np.arange(GRID_W)[None, :, None]
    kc1 = np.arange(GRID_W)[None, None, :]
    onehot = (kc1 - qc1 + NA_WIN_C - 1 == dd).astype(np.float32)
    toep = jnp.einsum('hrd,dqk->hrqk', rpb.astype(F32), onehot, precision=lax.Precision.HIGHEST)
    c0 = np.clip(np.arange(GRID_W) - NA_WIN_C // 2, 0, GRID_W - NA_WIN_C)
    kcs = np.arange(GRID_W)
    valid_c = (kcs[None, :] >= c0[:, None]) & (kcs[None, :] < c0[:, None] + NA_WIN_C)
    tabs = []
    for j in (0, 1, nblk - 1):
        start = int(np.clip(NA_QROWS * j - NA_QROWS, 0, rows - NA_KROWS))
        qr = NA_QROWS * j + np.arange(NA_QROWS)
        kr = start + np.arange(NA_KROWS)
        r0 = np.clip(qr - NA_WIN_R // 2, 0, rows - NA_WIN_R)
        valid_r = (kr[None, :] >= r0[:, None]) & (kr[None, :] < r0[:, None] + NA_WIN_R)
        drow = np.clip(kr[None, :] - qr[:, None] + NA_WIN_R - 1, 0, 2 * NA_WIN_R - 2).reshape(-1)
        sel = jnp.take(toep, drow, axis=1).reshape(nh, NA_QROWS, NA_KROWS, GRID_W, GRID_W)
        sel = jnp.transpose(sel, (0, 1, 3, 2, 4)).reshape(nh, TOK, NA_KROWS * GRID_W)
        valid = (valid_r[:, None, :, None] & valid_c[None, :, None, :]).reshape(TOK, NA_KROWS * GRID_W)
        loc = jnp.where(valid[None], sel, NEG_BIG)
        tabs.append(jnp.concatenate([loc, jnp.zeros((nh, TOK, TOK), F32)], axis=-1))
    return jnp.stack(tabs)


def _ssm_state_kernel(x_ref, ws_ref, ar_ref, ai_ref, h_ref, sh_ref, *, bsz, nchunk, ncc):
    sh_ref[...] = jnp.dot(x_ref[0], ws_ref[0], preferred_element_type=F32)
    a_r = ar_ref[0, 0]
    a_i = ai_ref[0, 0]
    arf, aif, arr, air = a_r[:, 0:LANES], a_i[:, 0:LANES], a_r[:, LANES:], a_i[:, LANES:]
    nlc = nchunk - ncc

    def body(i, carry):
        hre_f, him_f, hre_r, him_r = carry
        cf = jnp.where(i < ncc, nlc + i, i - ncc)
        cr = nchunk - 1 - i
        rf = pl.multiple_of(cf * bsz, bsz)
        rr = pl.multiple_of(cr * bsz, bsz)
        sre_f = sh_ref[pl.ds(rf, bsz), 0:LANES]
        sim_f = sh_ref[pl.ds(rf, bsz), LANES:2 * LANES]
        sre_r = sh_ref[pl.ds(rr, bsz), 2 * LANES:3 * LANES]
        sim_r = sh_ref[pl.ds(rr, bsz), 3 * LANES:4 * LANES]
        sh_ref[pl.ds(rf, bsz), 0:LANES] = hre_f
        sh_ref[pl.ds(rf, bsz), LANES:2 * LANES] = him_f
        sh_ref[pl.ds(rr, bsz), 2 * LANES:3 * LANES] = hre_r
        sh_ref[pl.ds(rr, bsz), 3 * LANES:4 * LANES] = him_r
        return (arf * hre_f - aif * him_f + sre_f, arf * him_f + aif * hre_f + sim_f,
                arr * hre_r - air * him_r + sre_r, arr * him_r + air * hre_r + sim_r)

    zero = jnp.zeros((bsz, LANES), F32)
    lax.fori_loop(0, nchunk, body, (zero, zero, zero, zero))
    h_ref[0] = sh_ref[...].astype(h_ref.dtype)


def _ssm_state(x8, ws8, ar8, ai8, bsz, nchunk, ncc):
    ns, r, cw = x8.shape
    nq = ws8.shape[2] // SSM_QW
    return pl.pallas_call(
        functools.partial(_ssm_state_kernel, bsz=bsz, nchunk=nchunk, ncc=ncc),
        grid=(ns, nq),
        in_specs=[pl.BlockSpec((1, r, cw), lambda g, q: (g, 0, 0)),
                  pl.BlockSpec((1, cw, SSM_QW), lambda g, q: (g, 0, q)),
                  pl.BlockSpec((1, 1, 1, 2 * LANES), lambda g, q: (g, q, 0, 0)),
                  pl.BlockSpec((1, 1, 1, 2 * LANES), lambda g, q: (g, q, 0, 0))],
        out_specs=pl.BlockSpec((1, r, SSM_QW), lambda g, q: (g, 0, q)),
        out_shape=jax.ShapeDtypeStruct((ns, r, nq * SSM_QW), BF16),
        scratch_shapes=[pltpu.VMEM((r, SSM_QW), F32)],
        compiler_params=_cparams(("arbitrary", "arbitrary")),
        name="ssm_state",
    )(x8, ws8, ar8, ai8)


def _ssm_out_kernel(x_ref, xn_ref, h_ref, wy_ref, wc_ref, d_ref, y_ref):
    y = jnp.dot(x_ref[0], wy_ref[0], preferred_element_type=F32)
    y = y + jnp.dot(h_ref[0], wc_ref[0], preferred_element_type=F32)
    y_ref[0] = (y + d_ref[0] * xn_ref[0].astype(F32)).astype(y_ref.dtype)


def _ssm_out(x8, h8, wy8, wc8, d8):
    ns, r, cw = x8.shape
    hw = h8.shape[2]
    tn = 512
    tr = r // 2 if r % 32 == 0 else r
    return pl.pallas_call(
        _ssm_out_kernel,
        grid=(ns, cw // tn, r // tr),
        in_specs=[pl.BlockSpec((1, tr, cw), lambda g, n, i: (g, i, 0)),
                  pl.BlockSpec((1, tr, tn), lambda g, n, i: (g, i, n)),
                  pl.BlockSpec((1, tr, hw), lambda g, n, i: (g, i, 0)),
                  pl.BlockSpec((1, cw, tn), lambda g, n, i: (g, 0, n)),
                  pl.BlockSpec((1, hw, tn), lambda g, n, i: (g, 0, n)),
                  pl.BlockSpec((1, 1, tn), lambda g, n, i: (g, 0, n))],
        out_specs=pl.BlockSpec((1, tr, tn), lambda g, n, i: (g, i, n)),
        out_shape=jax.ShapeDtypeStruct((ns, r, cw), BF16),
        compiler_params=_cparams(("arbitrary", "arbitrary", "arbitrary")),
        name="ssm_out",
    )(x8, x8, h8, wy8, wc8, d8)


def _ssm_params(lam_re, lam_im, log_dt, b_re, b_im, c_re, c_im, d_skip):
    L = SSM_CHUNK
    lam_re, lam_im = lam_re.astype(F32), lam_im.astype(F32)
    dt = jnp.exp(log_dt.astype(F32))[..., None]
    tau = jnp.arange(L + 1, dtype=F32)[:, None, None, None]
    mag = jnp.exp(lam_re * dt * tau)
    pw_re = mag * jnp.cos(lam_im * dt * tau)
    pw_im = mag * jnp.sin(lam_im * dt * tau)
    a_re, a_im = pw_re[1], pw_im[1]
    den = lam_re * lam_re + lam_im * lam_im
    z_re = ((a_re - 1.0) * lam_re + a_im * lam_im) / den
    z_im = (a_im * lam_re - (a_re - 1.0) * lam_im) / den
    bb_re = z_re[..., None] * b_re.astype(F32) - z_im[..., None] * b_im.astype(F32)
    bb_im = z_re[..., None] * b_im.astype(F32) + z_im[..., None] * b_re.astype(F32)
    cr, ci = c_re.astype(F32), c_im.astype(F32)

    ab_re = pw_re[..., None] * bb_re - pw_im[..., None] * bb_im
    ab_im = pw_re[..., None] * bb_im + pw_im[..., None] * bb_re
    ktap = (jnp.einsum('dgnp,tdgpm->tdgnm', cr, ab_re) - jnp.einsum('dgnp,tdgpm->tdgnm', ci, ab_im))[:L]
    i_idx = np.arange(L)[:, None]
    s_idx = np.arange(L)[None, :]
    lag_f = i_idx - s_idx
    lag_r = s_idx - i_idx
    kf = jnp.where((lag_f >= 0)[:, :, None, None, None], ktap[np.clip(lag_f, 0, L - 1), 0], 0.0)
    kr = jnp.where((lag_r >= 0)[:, :, None, None, None], ktap[np.clip(lag_r, 0, L - 1), 1], 0.0)
    mt = jnp.transpose(kf + kr, (2, 1, 4, 0, 3))

    exp_f = L - 1 - np.arange(L)
    exp_r = np.arange(L)
    inj = lambda ab, d, exps: jnp.transpose(ab[exps, d], (1, 0, 3, 2))
    wb = jnp.stack([inj(ab_re, 0, exp_f), inj(ab_im, 0, exp_f), inj(ab_re, 1, exp_r), inj(ab_im, 1, exp_r)], axis=1)

    def ro(d, exps):
        pr, pi = pw_re[exps, d], pw_im[exps, d]
        e_re = cr[d][None] * pr[:, :, None, :] - ci[d][None] * pi[:, :, None, :]
        e_im = cr[d][None] * pi[:, :, None, :] + ci[d][None] * pr[:, :, None, :]
        f = lambda e: jnp.transpose(e, (1, 3, 0, 2))
        return [f(e_re), f(-e_im)]

    wc = jnp.stack(ro(0, np.arange(L) + 1) + ro(1, L - np.arange(L)), axis=1)

    ns, sg = SSM_NSUPER, SSM_SUPER
    nq = sg // 2
    eye_qt = jnp.eye(sg, dtype=F32).reshape(sg, nq, 2)
    eye = jnp.eye(sg, dtype=F32)
    wb = wb.reshape(ns, sg, 4, L, SSM_GROUP, SSM_STATE)
    ws8 = jnp.einsum('Ggasmp,gqt->Gsgmqatp', wb, eye_qt).reshape(ns, SSM_CW, nq * SSM_QW)
    wy8 = jnp.einsum('Ggsmin,gh->Gsgmihn', mt.reshape(ns, sg, L, SSM_GROUP, L, SSM_GROUP), eye)
    wy8 = wy8.reshape(ns, SSM_CW, SSM_CW)
    wc = wc.reshape(ns, nq, 2, 4, SSM_STATE, L, SSM_GROUP)
    wc8 = jnp.einsum('Gqtapin,hqt->Gqatpihn', wc, eye_qt).reshape(ns, nq * SSM_QW, SSM_CW)
    d8 = jnp.broadcast_to(d_skip.astype(F32).reshape(ns, 1, sg, SSM_GROUP), (ns, L, sg, SSM_GROUP))
    d8 = d8.reshape(ns, 1, SSM_CW)
    packa = lambda x: jnp.transpose(x.reshape(2, ns, nq, 2 * SSM_STATE), (1, 2, 0, 3)).reshape(ns, nq, 1, 2 * LANES)
    return ws8.astype(BF16), wy8.astype(BF16), wc8.astype(BF16), d8, packa(pw_re[L]), packa(pw_im[L])


def _glu_kernel(y_ref, w_ref, b_ref, o_ref):
    y = y_ref[0].astype(F32)
    g = 0.5 * y * (1.0 + jnp.tanh(math.sqrt(2.0 / math.pi) * (y + 0.044715 * (y * y * y))))
    t = jnp.dot(g.astype(BF16), w_ref[...], preferred_element_type=F32) + b_ref[...]
    o_ref[0] = (g * jax.nn.sigmoid(t)).astype(o_ref.dtype)


def _glu(y3, w, b, ntile):
    bsz, t, n = y3.shape
    return pl.pallas_call(
        _glu_kernel,
        grid=(bsz, ntile),
        in_specs=[pl.BlockSpec((1, TOK, n), lambda i, j: (i, j, 0)),
                  pl.BlockSpec((n, n), lambda i, j: (0, 0)),
                  pl.BlockSpec((1, n), lambda i, j: (0, 0))],
        out_specs=pl.BlockSpec((1, TOK, n), lambda i, j: (i, j, 0)),
        out_shape=jax.ShapeDtypeStruct((bsz, t, n), BF16),
        compiler_params=_cparams(("arbitrary", "arbitrary")),
        name="glu",
    )(y3, w, b.reshape(1, n))


def _merge_kernel(oa_ref, ob_ref, os_ref, ga_ref, gb_ref, gs_ref, wa_ref, wb_ref, wc_ref, o_ref):
    acc = jax.nn.sigmoid(ga_ref[0].astype(F32)) * jnp.dot(oa_ref[0], wa_ref[...], preferred_element_type=F32)
    acc += jax.nn.sigmoid(gb_ref[0].astype(F32)) * jnp.dot(ob_ref[0], wb_ref[...], preferred_element_type=F32)
    acc += jax.nn.sigmoid(gs_ref[0].astype(F32)) * jnp.dot(os_ref[0], wc_ref[...], preferred_element_type=F32)
    o_ref[0] = acc.astype(o_ref.dtype)


def _merge(oa, ob, os_, z3, wa, wb, wc, ntile):
    bsz, t, k = oa.shape
    d = wa.shape[1]
    tn = 512
    ospec = pl.BlockSpec((1, TOK, k), lambda n, i, j: (i, j, 0))
    gspec = lambda br: pl.BlockSpec((1, TOK, tn), lambda n, i, j: (i, j, (ZC_GATE + br * d) // tn + n))
    wspec = pl.BlockSpec((k, tn), lambda n, i, j: (0, n))
    return pl.pallas_call(
        _merge_kernel,
        grid=(d // tn, bsz, ntile),
        in_specs=[ospec, ospec, ospec, gspec(0), gspec(1), gspec(2), wspec, wspec, wspec],
        out_specs=pl.BlockSpec((1, TOK, tn), lambda n, i, j: (i, j, n)),
        out_shape=jax.ShapeDtypeStruct((bsz, t, d), BF16),
        compiler_params=_cparams(("arbitrary", "arbitrary", "arbitrary")),
        name="merge",
    )(oa, ob, os_, z3, z3, z3, wa, wb, wc)


def _outproj_kernel(x_ref, mix_ref, w_ref, m_ref, o_ref):
    gate = m_ref[0, 0][2:3]
    o_ref[0] = x_ref[0] + gate * jnp.dot(mix_ref[0], w_ref[...], preferred_element_type=F32)


def _outproj(x, mix3, w, modsel, nlt, ntile):
    b, t, d = x.shape
    return pl.pallas_call(
        _outproj_kernel,
        grid=(b, ntile),
        in_specs=[pl.BlockSpec((1, TOK, d), lambda i, j: (i, j, 0)),
                  pl.BlockSpec((1, TOK, d), lambda i, j: (i, j, 0)),
                  pl.BlockSpec((d, d), lambda i, j: (0, 0)),
                  pl.BlockSpec((1, 1, 6, d), lambda i, j: (i, _type_idx(j, nlt), 0, 0))],
        out_specs=pl.BlockSpec((1, TOK, d), lambda i, j: (i, j, 0)),
        out_shape=jax.ShapeDtypeStruct((b, ntile * TOK, d), F32),
        compiler_params=_cparams(("arbitrary", "arbitrary")),
        name="outproj",
    )(x, mix3, w, modsel)


def _router_kernel(x_ref, g_ref, m_ref, w_ref, b_ref, h_ref, meta_ref, tw_ref, cnt_ref, run_ref, *, n_exp):
    first = jnp.logical_and(pl.program_id(0) == 0, pl.program_id(1) == 0)

    @pl.when(first)
    def _():
        run_ref[...] = jnp.zeros_like(run_ref)

    m = m_ref[0, 0]
    h = _rms(x_ref[0]) * g_ref[...] * (1.0 + m[4:5]) + m[3:4]
    h_ref[0] = h
    logits = jnp.dot(h, w_ref[...], preferred_element_type=F32, precision=lax.Precision.HIGHEST) + b_ref[...]
    lane = lax.broadcasted_iota(jnp.int32, (TOK, LANES), 1).astype(F32)
    lg = jnp.where(lane < n_exp, logits, -jnp.inf)
    vals, idxs = [], []
    for _ in range(TOP_K):
        mx = jnp.max(lg, axis=-1, keepdims=True)
        idx = jnp.min(jnp.where(lg == mx, lane, float(LANES)), axis=-1, keepdims=True)
        vals.append(mx)
        idxs.append(idx)
        lg = jnp.where(lane == idx, -jnp.inf, lg)
    es = [jnp.exp(v - vals[0]) for v in vals]
    den = es[0] + es[1] + es[2] + es[3]
    onehot = jnp.zeros((TOK, LANES), F32)
    for idx in idxs:
        onehot = onehot + jnp.where(lane == idx, 1.0, 0.0)
    row = lax.broadcasted_iota(jnp.int32, (TOK, TOK), 0)
    col = lax.broadcasted_iota(jnp.int32, (TOK, TOK), 1)
    tri = jnp.where(col < row, 1.0, 0.0).astype(BF16)
    before = jnp.dot(tri, onehot.astype(BF16), preferred_element_type=F32) + run_ref[...]
    meta = jnp.zeros((TOK, LANES), F32)
    tw = jnp.zeros((TOK, LANES), F32)
    for k in range(TOP_K):
        rank = jnp.sum(jnp.where(lane == idxs[k], before, 0.0), axis=-1, keepdims=True)
        meta = meta + jnp.where(lane == float(k), idxs[k], 0.0) + jnp.where(lane == float(TOP_K + k), rank, 0.0)
        tw = tw + jnp.where(lane == float(k), es[k] / den, 0.0)
    meta_ref[0] = meta.astype(jnp.int32)
    tw_ref[0] = tw
    run = run_ref[...] + jnp.sum(onehot, axis=0, keepdims=True)
    run_ref[...] = run
    cnt_ref[...] = run


def _router(x, g, modsel, w, bvec, nlt, n_exp):
    b, t, d = x.shape
    tokspec = lambda last: pl.BlockSpec((1, TOK, last), lambda i, j: (i, j, 0))
    return pl.pallas_call(
        functools.partial(_router_kernel, n_exp=n_exp),
        grid=(b, t // TOK),
        in_specs=[tokspec(d),
                  pl.BlockSpec((1, d), lambda i, j: (0, 0)),
                  pl.BlockSpec((1, 1, 6, d), lambda i, j: (i, _type_idx(j, nlt), 0, 0)),
                  pl.BlockSpec((d, LANES), lambda i, j: (0, 0)),
                  pl.BlockSpec((1, LANES), lambda i, j: (0, 0))],
        out_specs=[tokspec(d), tokspec(LANES), tokspec(LANES), pl.BlockSpec((1, LANES), lambda i, j: (0, 0))],
        out_shape=[jax.ShapeDtypeStruct((b, t, d), F32),
                   jax.ShapeDtypeStruct((b, t, LANES), jnp.int32),
                   jax.ShapeDtypeStruct((b, t, LANES), F32),
                   jax.ShapeDtypeStruct((1, LANES), F32)],
        scratch_shapes=[pltpu.VMEM((1, LANES), F32)],
        compiler_params=_cparams(("arbitrary", "arbitrary")),
        name="router",
    )(x, g, modsel, w, bvec)


def _dispatch_kernel(pos_ref, h_ref, xs_in_ref, xs_ref, sem):
    del xs_in_ref

    def row_copy(r, k):
        return pltpu.make_async_copy(h_ref.at[pl.ds(r, 1)], xs_ref.at[pl.ds(pos_ref[r * TOP_K + k], 1)], sem)

    def start(r, c):
        for k in range(TOP_K):
            row_copy(r, k).start()
        return c

    def wait(r, c):
        for k in range(TOP_K):
            row_copy(r, k).wait()
        return c

    lax.fori_loop(0, TOK, start, 0)
    lax.fori_loop(0, TOK, wait, 0)


def _dispatch(pos, h2, xs0):
    m, d = h2.shape
    return pl.pallas_call(
        _dispatch_kernel,
        grid=(m // TOK,),
        in_specs=[pl.BlockSpec((TOK * TOP_K,), lambda i: (i,), memory_space=pltpu.SMEM),
                  pl.BlockSpec((TOK, d), lambda i: (i, 0)),
                  pl.BlockSpec(memory_space=pl.ANY)],
        out_specs=pl.BlockSpec(memory_space=pl.ANY),
        out_shape=jax.ShapeDtypeStruct(xs0.shape, xs0.dtype),
        scratch_shapes=[pltpu.SemaphoreType.DMA(())],
        input_output_aliases={2: 0},
        compiler_params=_cparams(("arbitrary",)),
        name="dispatch",
    )(pos, h2, xs0)


def _experts_kernel(te_ref, tv_ref, x_ref, wgu_ref, bgu_ref, wd_ref, bd_ref, y_ref, *, d_ff, fchunk):
    i = pl.program_id(0)

    @pl.when(tv_ref[i] == 0)
    def _():
        y_ref[...] = jnp.zeros_like(y_ref)

    @pl.when(tv_ref[i] != 0)
    def _():
        x = x_ref[...].astype(BF16)
        acc = jnp.zeros(y_ref.shape, F32)
        for c in range(d_ff // fchunk):
            lo = c * fchunk
            g = jnp.dot(x, wgu_ref[0, :, lo:lo + fchunk], preferred_element_type=F32) + bgu_ref[0, :, lo:lo + fchunk]
            lin = (jnp.dot(x, wgu_ref[0, :, d_ff + lo:d_ff + lo + fchunk], preferred_element_type=F32)
                   + bgu_ref[0, :, d_ff + lo:d_ff + lo + fchunk])
            g = jnp.minimum(g, SWIGLU_LIMIT)
            lin = jnp.clip(lin, -SWIGLU_LIMIT, SWIGLU_LIMIT)
            act = g * jax.nn.sigmoid(SWIGLU_ALPHA * g) * (lin + 1.0)
            acc = acc + jnp.dot(act.astype(BF16), wd_ref[0, lo:lo + fchunk, :], preferred_element_type=F32)
        y_ref[...] = acc + bd_ref[0]


def _experts(te, tv, xs, wgu, bgu, wd, bd):
    rows, d = xs.shape
    n_exp, _, ff2 = wgu.shape
    d_ff = ff2 // 2
    grid_spec = pltpu.PrefetchScalarGridSpec(
        num_scalar_prefetch=2,
        grid=(rows // TOK,),
        in_specs=[pl.BlockSpec((TOK, d), lambda i, te, tv: (i, 0)),
                  pl.BlockSpec((1, d, ff2), lambda i, te, tv: (te[i], 0, 0)),
                  pl.BlockSpec((1, 1, ff2), lambda i, te, tv: (te[i], 0, 0)),
                  pl.BlockSpec((1, d_ff, d), lambda i, te, tv: (te[i], 0, 0)),
                  pl.BlockSpec((1, 1, d), lambda i, te, tv: (te[i], 0, 0))],
        out_specs=pl.BlockSpec((TOK, d), lambda i, te, tv: (i, 0)),
    )
    return pl.pallas_call(
        functools.partial(_experts_kernel, d_ff=d_ff, fchunk=512),
        grid_spec=grid_spec,
        out_shape=jax.ShapeDtypeStruct((rows, d), F32),
        compiler_params=_cparams(("arbitrary",)),
        name="experts",
    )(te, tv, xs, wgu, bgu.reshape(n_exp, 1, ff2), wd, bd.reshape(n_exp, 1, d))


def _combine_kernel(pos_ref, x_ref, tw_ref, m_ref, ys_ref, o_ref, buf_ref, sem):
    def row_copy(r, k):
        return pltpu.make_async_copy(ys_ref.at[pl.ds(pos_ref[r * TOP_K + k], 1)], buf_ref.at[k, pl.ds(r, 1)], sem)

    def start(r, c):
        for k in range(TOP_K):
            row_copy(r, k).start()
        return c

    def wait(r, c):
        for k in range(TOP_K):
            row_copy(r, k).wait()
        return c

    lax.fori_loop(0, TOK, start, 0)
    lax.fori_loop(0, TOK, wait, 0)
    tw = tw_ref[0]
    acc = tw[:, 0:1] * buf_ref[0]
    for k in range(1, TOP_K):
        acc = acc + tw[:, k:k + 1] * buf_ref[k]
    o_ref[0] = x_ref[0] + m_ref[0, 0][5:6] * acc


def _combine(pos, x, tw, modsel, ys, nlt):
    b, t, d = x.shape
    nt = t // TOK
    return pl.pallas_call(
        _combine_kernel,
        grid=(b, nt),
        in_specs=[pl.BlockSpec((TOK * TOP_K,), lambda i, j: (i * nt + j,), memory_space=pltpu.SMEM),
                  pl.BlockSpec((1, TOK, d), lambda i, j: (i, j, 0)),
                  pl.BlockSpec((1, TOK, LANES), lambda i, j: (i, j, 0)),
                  pl.BlockSpec((1, 1, 6, d), lambda i, j: (i, _type_idx(j, nlt), 0, 0)),
                  pl.BlockSpec(memory_space=pl.ANY)],
        out_specs=pl.BlockSpec((1, TOK, d), lambda i, j: (i, j, 0)),
        out_shape=jax.ShapeDtypeStruct((b, t, d), F32),
        scratch_shapes=[pltpu.VMEM((TOP_K, TOK, d), F32), pltpu.SemaphoreType.DMA(())],
        compiler_params=_cparams(("arbitrary", "arbitrary")),
        name="combine",
    )(pos, x, tw, modsel, ys)


def _moe(xa, norm_g, modsel, w_router, b_router, wgu, bgu, wd, bd, nlt):
    b, t, d = xa.shape
    m = b * t
    n_exp = w_router.shape[1]
    wr = jnp.zeros((d, LANES), F32).at[:, :n_exp].set(w_router.astype(F32))
    br = jnp.zeros((1, LANES), F32).at[0, :n_exp].set(b_router.astype(F32))
    h2, meta, tw, cnt = _router(xa, norm_g.reshape(1, d), modsel, wr, br, nlt, n_exp)

    counts = cnt[0, :n_exp].astype(jnp.int32)
    padded = ((counts + TOK - 1) // TOK) * TOK
    gend = jnp.cumsum(padded)
    gstart = gend - padded
    meta2 = meta.reshape(m, LANES)
    pos = (gstart[meta2[:, 0:TOP_K]] + meta2[:, TOP_K:2 * TOP_K]).reshape(m * TOP_K)
    ntiles = (m * TOP_K) // TOK + n_exp
    tile_start = jnp.arange(ntiles, dtype=jnp.int32) * TOK
    te = jnp.sum((tile_start[:, None] >= gend[None, :]).astype(jnp.int32), axis=1)
    te = jnp.minimum(te, n_exp - 1)
    tv = (tile_start < gend[-1]).astype(jnp.int32)

    xs = _dispatch(pos, h2.reshape(m, d), jnp.zeros((ntiles * TOK, d), F32))
    ys = _experts(te, tv, xs, wgu, bgu, wd, bd)
    return _combine(pos, xa, tw, modsel, ys, nlt)


def _rope_tables(seq, ctx):
    t = jnp.arange(seq, dtype=jnp.int32)
    row = (t // GRID_W).astype(F32)
    col = (t % GRID_W).astype(F32)
    n_freq = MLA_ROPE // 4
    inv_freq = ROPE_BASE ** (-jnp.arange(n_freq, dtype=F32) / n_freq)
    ang = jnp.concatenate([row[:, None] * inv_freq, col[:, None] * inv_freq], axis=-1)
    cos, sin = jnp.cos(ang), jnp.sin(ang)
    cos2 = jnp.tile(jnp.concatenate([cos, cos], axis=-1), (1, 2))
    sin2 = jnp.tile(jnp.concatenate([-sin, sin], axis=-1), (1, 2))
    cos2 = jnp.concatenate([cos2, jnp.ones((ctx, LANES), F32)], axis=0)
    sin2 = jnp.concatenate([sin2, jnp.zeros((ctx, LANES), F32)], axis=0)
    return cos2, sin2


def _pad_cols(w, width):
    return jnp.pad(w, ((0, 0), (0, width - w.shape[1])))


def _layout_w_in(w):
    o_ckv = MLA_Q_RANK
    o_kpe = o_ckv + MLA_KV_RANK
    o_naq = o_kpe + MLA_ROPE
    parts = [w[:, :o_ckv],
             _pad_cols(w[:, o_kpe:o_naq], ZC_CKV - ZC_KPE),
             w[:, o_ckv:o_kpe],
             w[:, o_naq:]]
    out = jnp.concatenate(parts, axis=1)
    assert out.shape[1] == ZW
    return out.astype(BF16)


def kernel(x, c, ctx, c_ctx, w_ada, b_ada, norm1_g, w_in, mla_q_norm, mla_kv_norm, mla_w_uq, mla_w_ukv, mla_q_gain, mla_k_gain, na_q_gain, na_k_gain, na_rpb, ssm_lam_re, ssm_lam_im, ssm_log_dt, ssm_b_re, ssm_b_im, ssm_c_re, ssm_c_im, ssm_d, ssm_w_glu, ssm_b_glu, w_pa, w_pb, w_pc, w_out, norm2_g, w_router, b_router, w_gate_up, b_gate_up, w_down, b_down):
    bsz, seq, d = x.shape
    nctx = ctx.shape[1]
    depth = w_ada.shape[0]
    assert nctx == TOK and seq % (NA_QROWS * GRID_W) == 0 and seq // GRID_W >= NA_KROWS
    nlt = seq // TOK
    t = seq + nctx
    m = bsz * t
    rows = seq // GRID_W
    nblk = rows // NA_QROWS
    nchunk = t // SSM_CHUNK

    cos2, sin2 = _rope_tables(seq, nctx)
    cvec = jnp.zeros((16, d), F32).at[:bsz].set(c.astype(F32)).at[bsz].set(c_ctx.astype(F32))
    xa = jnp.concatenate([x, ctx], axis=1).astype(F32)
    row128 = lambda v: v.astype(F32).reshape(1, LANES)

    for l in range(depth):
        need_ctx = l < depth - 1
        ntile = t // TOK if need_ctx else nlt
        mod = _ada(cvec, w_ada[l], b_ada[l])
        mod_lat = mod[:bsz].reshape(bsz, 1, 6, d)
        mod_ctx = jnp.broadcast_to(mod[bsz].reshape(1, 1, 6, d), (bsz, 1, 6, d))
        modsel = jnp.concatenate([mod_lat, mod_ctx], axis=1)

        h1 = _norm_mod(xa, norm1_g[l], modsel, nlt)
        z = _matmul(h1.reshape(m, d), _layout_w_in(w_in[l]), BF16, "in_proj", ZW // 4)
        z3 = z.reshape(bsz, t, ZW)

        wq = mla_w_uq[l].reshape(MLA_Q_RANK, MLA_HEADS, MLA_QK)
        wq = jnp.concatenate([wq[:, :, :MLA_NOPE].reshape(MLA_Q_RANK, -1),
                              wq[:, :, MLA_NOPE:].reshape(MLA_Q_RANK, -1)], axis=1).astype(BF16)
        qg_r = jnp.tile(mla_q_gain[l][MLA_NOPE:], 2)
        kg_r = jnp.concatenate([mla_k_gain[l][MLA_NOPE:], jnp.zeros((MLA_ROPE,), mla_k_gain.dtype)])
        qf = _mla_q(z3, mla_q_norm[l].astype(F32).reshape(1, -1), wq, row128(mla_q_gain[l][:MLA_NOPE]),
                    row128(qg_r), cos2, sin2, ntile)
        kf, vf = _mla_kv(z3, mla_kv_norm[l].astype(F32).reshape(1, -1), mla_w_ukv[l].astype(BF16),
                         row128(mla_k_gain[l][:MLA_NOPE]), row128(kg_r), cos2, sin2)
        o_a = _mla_attn(qf, kf, vf, seq, need_ctx)

        bias = _na_bias_tables(na_rpb[l], rows)
        o_b = _na_attn(z3, row128(jnp.tile(na_q_gain[l], 2)), row128(jnp.tile(na_k_gain[l], 2)), bias, nblk, need_ctx)

        ws8, wy8, wc8, d8, ar8, ai8 = _ssm_params(ssm_lam_re[l], ssm_lam_im[l], ssm_log_dt[l], ssm_b_re[l],
                                                   ssm_b_im[l], ssm_c_re[l], ssm_c_im[l], ssm_d[l])
        u = z3[:, :, ZC_SSM:ZC_SSM + SSM_WIDTH].reshape(bsz, nchunk, SSM_CHUNK, SSM_NSUPER, LANES)
        x8 = jnp.transpose(u, (3, 1, 0, 2, 4)).reshape(SSM_NSUPER, nchunk * bsz, SSM_CW)
        h8 = _ssm_state(x8, ws8, ar8, ai8, bsz, nchunk, nctx // SSM_CHUNK)
        y8 = _ssm_out(x8, h8, wy8, wc8, d8)
        y = jnp.transpose(y8.reshape(SSM_NSUPER, nchunk, bsz, SSM_CHUNK, LANES), (2, 1, 3, 0, 4))
        o_s = _glu(y.reshape(bsz, t, SSM_WIDTH), ssm_w_glu[l].astype(BF16), ssm_b_glu[l].astype(F32), ntile)

        mix = _merge(o_a, o_b, o_s, z3, w_pa[l].astype(BF16), w_pb[l].astype(BF16), w_pc[l].astype(BF16), ntile)
        xa = _outproj(xa, mix, w_out[l].astype(BF16), modsel, nlt, ntile)
        xa = _moe(xa, norm2_g[l].astype(F32), modsel, w_router[l], b_router[l], w_gate_up[l].astype(BF16),
                  b_gate_up[l].astype(F32), w_down[l].astype(BF16), b_down[l].astype(F32), nlt)
    return xa[:, :seq].astype(x.dtype)
```

```python
import functools
import math

import numpy as np
import jax
import jax.numpy as jnp
from jax import lax
from jax.experimental import pallas as pl
from jax.experimental.pallas import tpu as pltpu

F32 = jnp.float32
BF16 = jnp.bfloat16

GRID_W = 64
MLA_HEADS = 8
MLA_NOPE = 128
MLA_ROPE = 64
MLA_V = 128
MLA_QK = MLA_NOPE + MLA_ROPE
MLA_Q_RANK = 768
MLA_KV_RANK = 512
NA_HEADS = 16
NA_HEAD_DIM = 64
NA_WIDTH = NA_HEADS * NA_HEAD_DIM
NA_WIN_R = 8
NA_WIN_C = 16
SSM_WIDTH = 1024
SSM_GROUP = 16
SSM_GROUPS = SSM_WIDTH // SSM_GROUP
SSM_STATE = 64
TOP_K = 4
SWIGLU_LIMIT = 7.0
SWIGLU_ALPHA = 1.702
ROPE_BASE = 10000.0
EPS = 1e-6
MLA_SCALE = MLA_QK ** -0.5
NA_SCALE = NA_HEAD_DIM ** -0.5
NEG_BIG = -1e30

LANES = 128
TOK = 256
MLA_KPAD = 256
MLA_VPAD = 256
MLA_TQ = 512
MLA_TK = 512
VMEM_LIMIT = 56 * 1024 * 1024

ZC_GATE = 0
ZC_CQ = 6144
ZC_KPE = 6912
ZC_CKV = 7168
ZC_NAQ = 7680
ZC_NAK = 8704
ZC_NAV = 9728
ZC_SSM = 10752
ZW = 11776

SSM_CHUNK = 16
SSM_NSUPER = SSM_WIDTH // LANES
SSM_CW = SSM_CHUNK * LANES
SSM_PW = 2 * SSM_CHUNK * SSM_GROUP
NA_QROWS = 4
NA_KROWS = 12


def _cparams(sem, vmem=VMEM_LIMIT):
    return pltpu.CompilerParams(dimension_semantics=sem, vmem_limit_bytes=vmem)


def _rms(x, eps=EPS):
    return x * lax.rsqrt(jnp.mean(x * x, axis=-1, keepdims=True) + eps)


def _seg64_rms(t):
    lane = lax.broadcasted_iota(jnp.int32, t.shape, 1)
    left = lane < 64
    sq = t * t
    s_all = jnp.sum(sq, axis=-1, keepdims=True)
    s_left = jnp.sum(jnp.where(left, sq, 0.0), axis=-1, keepdims=True)
    ms = jnp.where(left, s_left, s_all - s_left) * (1.0 / 64.0)
    return t * lax.rsqrt(ms + EPS)


def _rope64(t, cos2, sin2):
    lane = lax.broadcasted_iota(jnp.int32, t.shape, 1)
    first = (lane % 64) < 32
    swapped = jnp.where(first, pltpu.roll(t, 96, axis=1), pltpu.roll(t, 32, axis=1))
    return t * cos2 + swapped * sin2


def _ones_col_tile(rows, dtype):
    lane = lax.broadcasted_iota(jnp.int32, (rows, LANES), 1)
    return jnp.where(lane == 0, 1.0, 0.0).astype(dtype)


def _qkt(q, k):
    return lax.dot_general(q, k, (((1,), (1,)), ((), ())), preferred_element_type=F32)


def _softmax_pv(s, v_ext):
    m = jnp.max(s, axis=-1, keepdims=True)
    p = jnp.exp((s - m).astype(BF16))
    oe = jnp.dot(p, v_ext, preferred_element_type=F32)
    return oe[:, 0:LANES] / oe[:, LANES:LANES + 1]


def _pack_bf16_pairs(x):
    n = x.shape[1] // 2
    r = x.astype(BF16).astype(F32)
    lo = lax.bitcast_convert_type(r[:, :n], jnp.uint32) >> 16
    hi = lax.bitcast_convert_type(r[:, n:], jnp.uint32) & jnp.uint32(0xFFFF0000)
    return hi | lo


def _unpack_bf16_pairs(w):
    lo = lax.bitcast_convert_type(w << 16, F32)
    hi = lax.bitcast_convert_type(w & jnp.uint32(0xFFFF0000), F32)
    return lo, hi


def _type_idx(j, nlt):
    return jnp.where(j >= nlt, 1, 0)


def _ada_kernel(c_ref, w_ref, b_ref, o_ref):
    c = c_ref[...]
    a = (c * jax.nn.sigmoid(c)).astype(BF16)
    o_ref[...] = jnp.dot(a, w_ref[...].astype(BF16), preferred_element_type=F32) + b_ref[...]


def _ada(cvec, w, b):
    rows, d = cvec.shape
    n = w.shape[1]
    tn = 1024
    return pl.pallas_call(
        _ada_kernel,
        grid=(n // tn,),
        in_specs=[pl.BlockSpec((rows, d), lambda j: (0, 0)),
                  pl.BlockSpec((d, tn), lambda j: (0, j)),
                  pl.BlockSpec((1, tn), lambda j: (0, j))],
        out_specs=pl.BlockSpec((rows, tn), lambda j: (0, j)),
        out_shape=jax.ShapeDtypeStruct((rows, n), F32),
        compiler_params=_cparams(("arbitrary",)),
        name="ada",
    )(cvec, w, b.reshape(1, n))


def _norm_mod_kernel(x_ref, g_ref, m_ref, o_ref):
    m = m_ref[0, 0]
    y = _rms(x_ref[0]) * g_ref[...]
    o_ref[0] = (y * (1.0 + m[1:2]) + m[0:1]).astype(o_ref.dtype)


def _norm_mod(x, g, modsel, nlt):
    b, t, d = x.shape
    return pl.pallas_call(
        _norm_mod_kernel,
        grid=(b, t // TOK),
        in_specs=[pl.BlockSpec((1, TOK, d), lambda i, j: (i, j, 0)),
                  pl.BlockSpec((1, d), lambda i, j: (0, 0)),
                  pl.BlockSpec((1, 1, 6, d), lambda i, j: (i, _type_idx(j, nlt), 0, 0))],
        out_specs=pl.BlockSpec((1, TOK, d), lambda i, j: (i, j, 0)),
        out_shape=jax.ShapeDtypeStruct((b, t, d), BF16),
        compiler_params=_cparams(("arbitrary", "arbitrary")),
        name="norm_mod",
    )(x, g.reshape(1, d), modsel)


def _mm_kernel(a_ref, w_ref, o_ref):
    o_ref[...] = jnp.dot(a_ref[...], w_ref[...], preferred_element_type=F32).astype(o_ref.dtype)


def _pick(n, cands):
    for c in cands:
        if n % c == 0:
            return c
    return n


def _matmul(a, w, out_dtype, name, tn):
    m, k = a.shape
    n = w.shape[1]
    tm = _pick(m, (512, 256, 128))
    return pl.pallas_call(
        _mm_kernel,
        grid=(n // tn, m // tm),
        in_specs=[pl.BlockSpec((tm, k), lambda j, i: (i, 0)),
                  pl.BlockSpec((k, tn), lambda j, i: (0, j))],
        out_specs=pl.BlockSpec((tm, tn), lambda j, i: (i, j)),
        out_shape=jax.ShapeDtypeStruct((m, n), out_dtype),
        compiler_params=_cparams(("arbitrary", "arbitrary")),
        name=name,
    )(a, w)


def _mla_q_kernel(z_ref, g_ref, w_ref, gn_ref, gr_ref, cos_ref, sin_ref, o_ref):
    cq = z_ref[0].astype(F32)
    xn = (_rms(cq) * g_ref[...]).astype(BF16)
    q = jnp.dot(xn, w_ref[...], preferred_element_type=F32)
    lane = lax.broadcasted_iota(jnp.int32, (TOK, LANES), 1)
    left = lane < 64
    for h in range(MLA_HEADS):
        qn = q[:, h * MLA_NOPE:(h + 1) * MLA_NOPE]
        o_ref[0, h, :, 0:LANES] = (_rms(qn) * (gn_ref[...] * MLA_SCALE)).astype(o_ref.dtype)
    base = MLA_HEADS * MLA_NOPE
    for hp in range(MLA_HEADS // 2):
        t = q[:, base + hp * LANES: base + (hp + 1) * LANES]
        t = _seg64_rms(t) * (gr_ref[...] * MLA_SCALE)
        r = _rope64(t, cos_ref[...], sin_ref[...])
        o_ref[0, 2 * hp, :, LANES:2 * LANES] = jnp.where(left, r, 0.0).astype(o_ref.dtype)
        o_ref[0, 2 * hp + 1, :, LANES:2 * LANES] = jnp.where(left, pltpu.roll(r, 64, axis=1), 0.0).astype(o_ref.dtype)


def _mla_q(z3, g, w, gn, gr2, cos2, sin2, ntile):
    b, t, _ = z3.shape
    return pl.pallas_call(
        _mla_q_kernel,
        grid=(b, ntile),
        in_specs=[pl.BlockSpec((1, TOK, MLA_Q_RANK), lambda i, j: (i, j, ZC_CQ // MLA_Q_RANK)),
                  pl.BlockSpec((1, MLA_Q_RANK), lambda i, j: (0, 0)),
                  pl.BlockSpec(w.shape, lambda i, j: (0, 0)),
                  pl.BlockSpec((1, LANES), lambda i, j: (0, 0)),
                  pl.BlockSpec((1, LANES), lambda i, j: (0, 0)),
                  pl.BlockSpec((TOK, LANES), lambda i, j: (j, 0)),
                  pl.BlockSpec((TOK, LANES), lambda i, j: (j, 0))],
        out_specs=pl.BlockSpec((1, MLA_HEADS, TOK, MLA_KPAD), lambda i, j: (i, 0, j, 0)),
        out_shape=jax.ShapeDtypeStruct((b, MLA_HEADS, ntile * TOK, MLA_KPAD), BF16),
        compiler_params=_cparams(("arbitrary", "arbitrary")),
        name="mla_q",
    )(z3, g, w, gn, gr2, cos2, sin2)


def _mla_kv_kernel(z_ref, zk_ref, g_ref, w_ref, gn_ref, gr_ref, cos_ref, sin_ref, k_ref, v_ref):
    ckv = z_ref[0].astype(F32)
    xn = (_rms(ckv) * g_ref[...]).astype(BF16)
    kv = jnp.dot(xn, w_ref[...], preferred_element_type=F32)
    lane = lax.broadcasted_iota(jnp.int32, (TOK, LANES), 1)
    left = lane < 64
    t = jnp.where(left, zk_ref[0].astype(F32), 0.0)
    ms = jnp.sum(t * t, axis=-1, keepdims=True) * (1.0 / MLA_ROPE)
    kr = t * lax.rsqrt(ms + EPS) * gr_ref[...]
    kr = jnp.where(left, _rope64(kr, cos_ref[...], sin_ref[...]), 0.0).astype(k_ref.dtype)
    ones = _ones_col_tile(TOK, v_ref.dtype)
    hw = MLA_NOPE + MLA_V
    for h in range(MLA_HEADS):
        kn = kv[:, h * hw: h * hw + MLA_NOPE]
        k_ref[0, h, :, 0:LANES] = (_rms(kn) * gn_ref[...]).astype(k_ref.dtype)
        k_ref[0, h, :, LANES:2 * LANES] = kr
        v_ref[0, h, :, 0:LANES] = kv[:, h * hw + MLA_NOPE:(h + 1) * hw].astype(v_ref.dtype)
        v_ref[0, h, :, LANES:2 * LANES] = ones


def _mla_kv(z3, g, w, gn, gr2, cos2, sin2):
    b, t, _ = z3.shape
    return pl.pallas_call(
        _mla_kv_kernel,
        grid=(b, t // TOK),
        in_specs=[pl.BlockSpec((1, TOK, MLA_KV_RANK), lambda i, j: (i, j, ZC_CKV // MLA_KV_RANK)),
                  pl.BlockSpec((1, TOK, LANES), lambda i, j: (i, j, ZC_KPE // LANES)),
                  pl.BlockSpec((1, MLA_KV_RANK), lambda i, j: (0, 0)),
                  pl.BlockSpec(w.shape, lambda i, j: (0, 0)),
                  pl.BlockSpec((1, LANES), lambda i, j: (0, 0)),
                  pl.BlockSpec((1, LANES), lambda i, j: (0, 0)),
                  pl.BlockSpec((TOK, LANES), lambda i, j: (j, 0)),
                  pl.BlockSpec((TOK, LANES), lambda i, j: (j, 0))],
        out_specs=[pl.BlockSpec((1, MLA_HEADS, TOK, MLA_KPAD), lambda i, j: (i, 0, j, 0)),
                   pl.BlockSpec((1, MLA_HEADS, TOK, MLA_VPAD), lambda i, j: (i, 0, j, 0))],
        out_shape=[jax.ShapeDtypeStruct((b, MLA_HEADS, t, MLA_KPAD), BF16),
                   jax.ShapeDtypeStruct((b, MLA_HEADS, t, MLA_VPAD), BF16)],
        compiler_params=_cparams(("arbitrary", "arbitrary")),
        name="mla_kv",
    )(z3, z3, g, w, gn, gr2, cos2, sin2)


def _mla_attn_kernel(q_ref, k_ref, v_ref, o_ref, *, nk, tk):
    q = q_ref[0, 0]
    m = acc = None
    for c0 in range(0, nk, tk):
        sz = min(tk, nk - c0)
        s = _qkt(q, k_ref[0, 0, c0:c0 + sz, :])
        cm = jnp.max(s, axis=-1, keepdims=True)
        m_new = cm if m is None else jnp.maximum(m, cm)
        pv = jnp.dot(jnp.exp((s - m_new).astype(BF16)), v_ref[0, 0, c0:c0 + sz, :], preferred_element_type=F32)
        acc = pv if m is None else acc * jnp.exp(m - m_new) + pv
        m = m_new
    o_ref[0] = (acc[:, 0:LANES] / acc[:, LANES:LANES + 1]).astype(o_ref.dtype)


def _mla_attn_ctx_kernel(q_ref, k_ref, v_ref, o_ref):
    o_ref[0] = _softmax_pv(_qkt(q_ref[0, 0], k_ref[0, 0]), v_ref[0, 0]).astype(o_ref.dtype)


def _mla_attn(q, k, v, seq, need_ctx):
    b, h, t, _ = k.shape
    tq = _pick(seq, (MLA_TQ, TOK))
    sem = ("arbitrary", "arbitrary", "arbitrary")
    o = pl.pallas_call(
        functools.partial(_mla_attn_kernel, nk=t, tk=MLA_TK),
        grid=(b, h, seq // tq),
        in_specs=[pl.BlockSpec((1, 1, tq, MLA_KPAD), lambda i, hh, j: (i, hh, j, 0)),
                  pl.BlockSpec((1, 1, t, MLA_KPAD), lambda i, hh, j: (i, hh, 0, 0)),
                  pl.BlockSpec((1, 1, t, MLA_VPAD), lambda i, hh, j: (i, hh, 0, 0))],
        out_specs=pl.BlockSpec((1, tq, MLA_V), lambda i, hh, j: (i, j, hh)),
        out_shape=jax.ShapeDtypeStruct((b, seq, h * MLA_V), BF16),
        compiler_params=_cparams(sem),
        name="mla_attn",
    )(q, k, v)
    if not need_ctx:
        return o
    nctx = t - seq
    ctx_spec = lambda w: pl.BlockSpec((1, 1, nctx, w), lambda i, hh, j: (i, hh, seq // nctx, 0))
    o_c = pl.pallas_call(
        _mla_attn_ctx_kernel,
        grid=(b, h, 1),
        in_specs=[ctx_spec(MLA_KPAD), ctx_spec(MLA_KPAD), ctx_spec(MLA_VPAD)],
        out_specs=pl.BlockSpec((1, nctx, MLA_V), lambda i, hh, j: (i, 0, hh)),
        out_shape=jax.ShapeDtypeStruct((b, nctx, h * MLA_V), BF16),
        compiler_params=_cparams(sem),
        name="mla_attn_ctx",
    )(q, k, v)
    return jnp.concatenate([o, o_c], axis=1)


def _na_norm_kernel(q_ref, k_ref, gq_ref, gk_ref, qn_ref, kn_ref):
    for c in range(q_ref.shape[2] // LANES):
        sl = slice(c * LANES, (c + 1) * LANES)
        qn_ref[0, :, sl] = (_seg64_rms(q_ref[0, :, sl].astype(F32)) * (gq_ref[...] * NA_SCALE)).astype(qn_ref.dtype)
        kn_ref[0, :, sl] = (_seg64_rms(k_ref[0, :, sl].astype(F32)) * gk_ref[...]).astype(kn_ref.dtype)


def _na_norm(z3, gq2, gk2):
    b, t, _ = z3.shape
    w = 512
    zspec = lambda col0: pl.BlockSpec((1, TOK, w), lambda i, j, c: (i, j, col0 // w + c))
    ospec = pl.BlockSpec((1, TOK, w), lambda i, j, c: (i, j, c))
    gspec = pl.BlockSpec((1, LANES), lambda i, j, c: (0, 0))
    oshape = jax.ShapeDtypeStruct((b, t, NA_WIDTH), BF16)
    return pl.pallas_call(
        _na_norm_kernel,
        grid=(b, t // TOK, NA_WIDTH // w),
        in_specs=[zspec(ZC_NAQ), zspec(ZC_NAK), gspec, gspec],
        out_specs=[ospec, ospec],
        out_shape=[oshape, oshape],
        compiler_params=_cparams(("arbitrary", "arbitrary", "arbitrary")),
        name="na_norm",
    )(z3, z3, gq2, gk2)


def _na_kernel(q_ref, k0_ref, k1_ref, k2_ref, kc_ref, v0_ref, v1_ref, v2_ref, vc_ref, bias_ref, o_ref, *, nblk):
    j = pl.program_id(1)
    lane = lax.broadcasted_iota(jnp.int32, (TOK, LANES), 1)
    left = lane < 64
    qn = q_ref[0]

    def heads(kn, v_ext, bias_of):
        outs = []
        for hh in range(2):
            mine = left if hh == 0 else jnp.logical_not(left)
            s = _qkt(jnp.where(mine, qn, jnp.zeros_like(qn)), kn)
            b = bias_of(hh)
            if b is not None:
                s = s + b
            outs.append(_softmax_pv(s, v_ext))
        o_ref[0] = jnp.where(left, outs[0], outs[1]).astype(o_ref.dtype)

    @pl.when(j >= nblk)
    def _():
        v_ext = jnp.concatenate([vc_ref[0], _ones_col_tile(TOK, BF16)], axis=1)
        heads(kc_ref[0], v_ext, lambda hh: None)

    @pl.when(j < nblk)
    def _():
        kn = jnp.concatenate([k0_ref[0], k1_ref[0], k2_ref[0], kc_ref[0]], axis=0)
        vcat = jnp.concatenate([v0_ref[0], v1_ref[0], v2_ref[0], vc_ref[0]], axis=0)
        v_ext = jnp.concatenate([vcat, _ones_col_tile(4 * TOK, BF16)], axis=1)
        heads(kn, v_ext, lambda hh: bias_ref[0, hh])


def _na_attn(qn, kn, z3, bias, nblk, need_ctx):
    b = z3.shape[0]
    kb = lambda j: jnp.clip(j - 1, 0, nblk - 3)
    nj = nblk + (1 if need_ctx else 0)

    def tspec(col_blk0, tok_idx):
        return pl.BlockSpec((1, TOK, LANES), lambda hp, j, i: (i, tok_idx(j), col_blk0 + hp))

    var = lambda j: jnp.where(j == 0, 0, jnp.where(j == nblk - 1, 2, 1))
    window = lambda c0: [tspec(c0, (lambda j, d=d: kb(j) + d)) for d in range(3)] + [tspec(c0, lambda j: nblk)]
    in_specs = [tspec(0, lambda j: j)] + window(0) + window(ZC_NAV // LANES)
    in_specs += [pl.BlockSpec((1, 2, TOK, 4 * TOK), lambda hp, j, i: (var(j), hp, 0, 0))]
    return pl.pallas_call(
        functools.partial(_na_kernel, nblk=nblk),
        grid=(NA_HEADS // 2, nj, b),
        in_specs=in_specs,
        out_specs=pl.BlockSpec((1, TOK, LANES), lambda hp, j, i: (i, j, hp)),
        out_shape=jax.ShapeDtypeStruct((b, nj * TOK, NA_WIDTH), BF16),
        compiler_params=_cparams(("arbitrary", "arbitrary", "arbitrary")),
        name="na_attn",
    )(qn, kn, kn, kn, kn, z3, z3, z3, z3, bias)


def _na_bias_tables(rpb, rows):
    nblk = rows // NA_QROWS
    nh = rpb.shape[0]
    ncol = 2 * NA_WIN_C - 1
    dd = np.arange(ncol)[:, None, None]
    qc1 = np.arange(GRID_W)[None, :, None]
    kc1 = np.arange(GRID_W)[None, None, :]
    onehot = (kc1 - qc1 + NA_WIN_C - 1 == dd).astype(np.float32)
    toep = jnp.einsum('hrd,dqk->hrqk', rpb.astype(F32), onehot, precision=lax.Precision.HIGHEST)
    c0 = np.clip(np.arange(GRID_W) - NA_WIN_C // 2, 0, GRID_W - NA_WIN_C)
    kcs = np.arange(GRID_W)
    valid_c = (kcs[None, :] >= c0[:, None]) & (kcs[None, :] < c0[:, None] + NA_WIN_C)
    tabs = []
    for j in (0, 1, nblk - 1):
        start = int(np.clip(NA_QROWS * j - NA_QROWS, 0, rows - NA_KROWS))
        qr = NA_QROWS * j + np.arange(NA_QROWS)
        kr = start + np.arange(NA_KROWS)
        r0 = np.clip(qr - NA_WIN_R // 2, 0, rows - NA_WIN_R)
        valid_r = (kr[None, :] >= r0[:, None]) & (kr[None, :] < r0[:, None] + NA_WIN_R)
        drow = np.clip(kr[None, :] - qr[:, None] + NA_WIN_R - 1, 0, 2 * NA_WIN_R - 2).reshape(-1)
        sel = jnp.take(toep, drow, axis=1).reshape(nh, NA_QROWS, NA_KROWS, GRID_W, GRID_W)
        sel = jnp.transpose(sel, (0, 1, 3, 2, 4)).reshape(nh, TOK, NA_KROWS * GRID_W)
        valid = (valid_r[:, None, :, None] & valid_c[None, :, None, :]).reshape(TOK, NA_KROWS * GRID_W)
        loc = jnp.where(valid[None], sel, NEG_BIG)
        tabs.append(jnp.concatenate([loc, jnp.zeros((nh, TOK, TOK), F32)], axis=-1))
    return jnp.stack(tabs)


def _ssm_kernel(u_ref, wb_ref, mt_ref, wc_ref, d_ref, ar_ref, ai_ref, y_ref, sh_ref, *, bsz, nchunk, ncc):
    u = u_ref[0]
    sh_ref[...] = jnp.dot(u, wb_ref[0], preferred_element_type=F32)
    a_r = ar_ref[0]
    a_i = ai_ref[0]
    arf, aif, arr, air = a_r[:, 0:LANES], a_i[:, 0:LANES], a_r[:, LANES:], a_i[:, LANES:]
    nlc = nchunk - ncc

    def body(i, carry):
        hre_f, him_f, hre_r, him_r = carry
        cf = jnp.where(i < ncc, nlc + i, i - ncc)
        cr = nchunk - 1 - i
        rf = pl.multiple_of(cf * bsz, bsz)
        rr = pl.multiple_of(cr * bsz, bsz)
        sre_f = sh_ref[pl.ds(rf, bsz), 0:LANES]
        sim_f = sh_ref[pl.ds(rf, bsz), LANES:2 * LANES]
        sre_r = sh_ref[pl.ds(rr, bsz), 2 * LANES:3 * LANES]
        sim_r = sh_ref[pl.ds(rr, bsz), 3 * LANES:4 * LANES]
        sh_ref[pl.ds(rf, bsz), 0:LANES] = hre_f
        sh_ref[pl.ds(rf, bsz), LANES:2 * LANES] = him_f
        sh_ref[pl.ds(rr, bsz), 2 * LANES:3 * LANES] = hre_r
        sh_ref[pl.ds(rr, bsz), 3 * LANES:4 * LANES] = him_r
        return (arf * hre_f - aif * him_f + sre_f, arf * him_f + aif * hre_f + sim_f,
                arr * hre_r - air * him_r + sre_r, arr * him_r + air * hre_r + sim_r)

    zero = jnp.zeros((bsz, LANES), F32)
    lax.fori_loop(0, nchunk, body, (zero, zero, zero, zero))
    y = jnp.dot(u, mt_ref[0], preferred_element_type=F32)
    y = y + jnp.dot(sh_ref[...].astype(BF16), wc_ref[0], preferred_element_type=F32)
    y_ref[0] = (y + d_ref[0] * u.astype(F32)).astype(y_ref.dtype)


def _ssm(xp, wb2, mt2, wc2, d2, ar2, ai2, bsz, nchunk, ncc):
    ns, r, cw = xp.shape
    pw = SSM_PW
    nq = cw // pw
    wspec = lambda shp: pl.BlockSpec((1,) + shp, lambda g, q: (g * nq + q, 0, 0))
    return pl.pallas_call(
        functools.partial(_ssm_kernel, bsz=bsz, nchunk=nchunk, ncc=ncc),
        grid=(ns, nq),
        in_specs=[pl.BlockSpec((1, r, pw), lambda g, q: (g, 0, q)),
                  wspec((pw, pw)), wspec((pw, pw)), wspec((pw, pw)),
                  wspec((1, pw)), wspec((1, 2 * LANES)), wspec((1, 2 * LANES))],
        out_specs=pl.BlockSpec((1, r, pw), lambda g, q: (g, 0, q)),
        out_shape=jax.ShapeDtypeStruct((ns, r, cw), BF16),
        scratch_shapes=[pltpu.VMEM((r, pw), F32)],
        compiler_params=_cparams(("arbitrary", "arbitrary")),
        name="ssm",
    )(xp, wb2, mt2, wc2, d2, ar2, ai2)


def _ssm_perm():
    r = np.arange(SSM_CW)
    s, g, mm = r // LANES, (r % LANES) // SSM_GROUP, r % SSM_GROUP
    dst = g * (SSM_CHUNK * SSM_GROUP) + s * SSM_GROUP + mm
    return (dst[:, None] == np.arange(SSM_CW)[None, :]).astype(np.float32)


def _ssm_params(lam_re, lam_im, log_dt, b_re, b_im, c_re, c_im, d_skip):
    L = SSM_CHUNK
    lam_re, lam_im = lam_re.astype(F32), lam_im.astype(F32)
    dt = jnp.exp(log_dt.astype(F32))[..., None]
    tau = jnp.arange(L + 1, dtype=F32)[:, None, None, None]
    mag = jnp.exp(lam_re * dt * tau)
    pw_re = mag * jnp.cos(lam_im * dt * tau)
    pw_im = mag * jnp.sin(lam_im * dt * tau)
    a_re, a_im = pw_re[1], pw_im[1]
    den = lam_re * lam_re + lam_im * lam_im
    z_re = ((a_re - 1.0) * lam_re + a_im * lam_im) / den
    z_im = (a_im * lam_re - (a_re - 1.0) * lam_im) / den
    bb_re = z_re[..., None] * b_re.astype(F32) - z_im[..., None] * b_im.astype(F32)
    bb_im = z_re[..., None] * b_im.astype(F32) + z_im[..., None] * b_re.astype(F32)
    cr, ci = c_re.astype(F32), c_im.astype(F32)

    ab_re = pw_re[..., None] * bb_re - pw_im[..., None] * bb_im
    ab_im = pw_re[..., None] * bb_im + pw_im[..., None] * bb_re
    ktap = (jnp.einsum('dgnp,tdgpm->tdgnm', cr, ab_re) - jnp.einsum('dgnp,tdgpm->tdgnm', ci, ab_im))[:L]
    i_idx = np.arange(L)[:, None]
    s_idx = np.arange(L)[None, :]
    lag_f = i_idx - s_idx
    lag_r = s_idx - i_idx
    kf = jnp.where((lag_f >= 0)[:, :, None, None, None], ktap[np.clip(lag_f, 0, L - 1), 0], 0.0)
    kr = jnp.where((lag_r >= 0)[:, :, None, None, None], ktap[np.clip(lag_r, 0, L - 1), 1], 0.0)
    w1 = L * SSM_GROUP
    mt = jnp.transpose(kf + kr, (2, 1, 4, 0, 3)).reshape(SSM_GROUPS, w1, w1)

    exp_f = L - 1 - np.arange(L)
    exp_r = np.arange(L)
    inj = lambda ab, d, exps: jnp.transpose(ab[exps, d], (1, 0, 3, 2)).reshape(SSM_GROUPS, w1, SSM_STATE)
    wb_parts = [inj(ab_re, 0, exp_f), inj(ab_im, 0, exp_f), inj(ab_re, 1, exp_r), inj(ab_im, 1, exp_r)]

    def ro(d, exps):
        pr, pi = pw_re[exps, d], pw_im[exps, d]
        e_re = cr[d][None] * pr[:, :, None, :] - ci[d][None] * pi[:, :, None, :]
        e_im = cr[d][None] * pi[:, :, None, :] + ci[d][None] * pr[:, :, None, :]
        f = lambda e: jnp.transpose(e, (1, 3, 0, 2)).reshape(SSM_GROUPS, SSM_STATE, w1)
        return [f(e_re), f(-e_im)]

    wc_parts = ro(0, np.arange(L) + 1) + ro(1, L - np.arange(L))
    npair = SSM_GROUPS // 2

    def pair_diag(x):
        x = x.reshape(npair, 2, x.shape[1], x.shape[2])
        zed = jnp.zeros_like(x[:, 0])
        return jnp.concatenate([jnp.concatenate([x[:, 0], zed], axis=2),
                                jnp.concatenate([zed, x[:, 1]], axis=2)], axis=1)

    wb2 = jnp.concatenate([pair_diag(p) for p in wb_parts], axis=2)
    wc2 = jnp.concatenate([pair_diag(p) for p in wc_parts], axis=1)
    mt2 = pair_diag(mt)
    d2 = jnp.tile(d_skip.astype(F32).reshape(SSM_GROUPS, 1, SSM_GROUP), (1, L, 1)).reshape(npair, 1, 2 * w1)
    packp = lambda x: x.reshape(2, npair, 2 * SSM_STATE)
    ar2 = jnp.concatenate([packp(pw_re[L])[0], packp(pw_re[L])[1]], axis=-1)[:, None, :]
    ai2 = jnp.concatenate([packp(pw_im[L])[0], packp(pw_im[L])[1]], axis=-1)[:, None, :]
    return wb2.astype(BF16), mt2.astype(BF16), wc2.astype(BF16), d2, ar2, ai2


def _glu_kernel(y_ref, w_ref, b_ref, o_ref):
    y = y_ref[0].astype(F32)
    g = 0.5 * y * (1.0 + jnp.tanh(math.sqrt(2.0 / math.pi) * (y + 0.044715 * (y * y * y))))
    t = jnp.dot(g.astype(BF16), w_ref[...], preferred_element_type=F32) + b_ref[...]
    o_ref[0] = (g * jax.nn.sigmoid(t)).astype(o_ref.dtype)


def _glu(y3, w, b, ntile):
    bsz, t, n = y3.shape
    return pl.pallas_call(
        _glu_kernel,
        grid=(bsz, ntile),
        in_specs=[pl.BlockSpec((1, TOK, n), lambda i, j: (i, j, 0)),
                  pl.BlockSpec((n, n), lambda i, j: (0, 0)),
                  pl.BlockSpec((1, n), lambda i, j: (0, 0))],
        out_specs=pl.BlockSpec((1, TOK, n), lambda i, j: (i, j, 0)),
        out_shape=jax.ShapeDtypeStruct((bsz, ntile * TOK, n), BF16),
        compiler_params=_cparams(("arbitrary", "arbitrary")),
        name="glu",
    )(y3, w, b.reshape(1, n))


def _merge_kernel(oa_ref, ob_ref, os_ref, ga_ref, gb_ref, gs_ref, wa_ref, wb_ref, wc_ref, o_ref):
    acc = jax.nn.sigmoid(ga_ref[0].astype(F32)) * jnp.dot(oa_ref[0], wa_ref[...], preferred_element_type=F32)
    acc += jax.nn.sigmoid(gb_ref[0].astype(F32)) * jnp.dot(ob_ref[0], wb_ref[...], preferred_element_type=F32)
    acc += jax.nn.sigmoid(gs_ref[0].astype(F32)) * jnp.dot(os_ref[0], wc_ref[...], preferred_element_type=F32)
    o_ref[0] = acc.astype(o_ref.dtype)


def _merge(oa, ob, os_, z3, wa, wb, wc, ntile):
    bsz, _, k = oa.shape
    d = wa.shape[1]
    ospec = pl.BlockSpec((1, TOK, k), lambda i, j: (i, j, 0))
    gspec = lambda br: pl.BlockSpec((1, TOK, d), lambda i, j: (i, j, ZC_GATE // d + br))
    wspec = pl.BlockSpec((k, d), lambda i, j: (0, 0))
    return pl.pallas_call(
        _merge_kernel,
        grid=(bsz, ntile),
        in_specs=[ospec, ospec, ospec, gspec(0), gspec(1), gspec(2), wspec, wspec, wspec],
        out_specs=pl.BlockSpec((1, TOK, d), lambda i, j: (i, j, 0)),
        out_shape=jax.ShapeDtypeStruct((bsz, ntile * TOK, d), BF16),
        compiler_params=_cparams(("arbitrary", "arbitrary")),
        name="merge",
    )(oa, ob, os_, z3, z3, z3, wa, wb, wc)


def _outproj_kernel(x_ref, mix_ref, w_ref, m_ref, o_ref):
    gate = m_ref[0, 0][2:3]
    o_ref[0] = x_ref[0] + gate * jnp.dot(mix_ref[0], w_ref[...], preferred_element_type=F32)


def _outproj(x, mix3, w, modsel, nlt, ntile):
    b, t, d = x.shape
    return pl.pallas_call(
        _outproj_kernel,
        grid=(b, ntile),
        in_specs=[pl.BlockSpec((1, TOK, d), lambda i, j: (i, j, 0)),
                  pl.BlockSpec((1, TOK, d), lambda i, j: (i, j, 0)),
                  pl.BlockSpec((d, d), lambda i, j: (0, 0)),
                  pl.BlockSpec((1, 1, 6, d), lambda i, j: (i, _type_idx(j, nlt), 0, 0))],
        out_specs=pl.BlockSpec((1, TOK, d), lambda i, j: (i, j, 0)),
        out_shape=jax.ShapeDtypeStruct((b, ntile * TOK, d), F32),
        compiler_params=_cparams(("arbitrary", "arbitrary")),
        name="outproj",
    )(x, mix3, w, modsel)


def _router_kernel(x_ref, g_ref, m_ref, w_ref, b_ref, h_ref, meta_ref, tw_ref, cnt_ref, run_ref, *, n_exp):
    first = jnp.logical_and(pl.program_id(0) == 0, pl.program_id(1) == 0)

    @pl.when(first)
    def _():
        run_ref[...] = jnp.zeros_like(run_ref)

    m = m_ref[0, 0]
    h = _rms(x_ref[0]) * g_ref[...] * (1.0 + m[4:5]) + m[3:4]
    h_ref[0] = _pack_bf16_pairs(h)
    logits = jnp.dot(h, w_ref[...], preferred_element_type=F32, precision=lax.Precision.HIGHEST) + b_ref[...]
    lane = lax.broadcasted_iota(jnp.int32, (TOK, LANES), 1).astype(F32)
    lg = jnp.where(lane < n_exp, logits, -jnp.inf)
    vals, idxs = [], []
    for _ in range(TOP_K):
        mx = jnp.max(lg, axis=-1, keepdims=True)
        idx = jnp.min(jnp.where(lg == mx, lane, float(LANES)), axis=-1, keepdims=True)
        vals.append(mx)
        idxs.append(idx)
        lg = jnp.where(lane == idx, -jnp.inf, lg)
    es = [jnp.exp(v - vals[0]) for v in vals]
    den = es[0] + es[1] + es[2] + es[3]
    onehot = jnp.zeros((TOK, LANES), F32)
    for idx in idxs:
        onehot = onehot + jnp.where(lane == idx, 1.0, 0.0)
    row = lax.broadcasted_iota(jnp.int32, (TOK, TOK), 0)
    col = lax.broadcasted_iota(jnp.int32, (TOK, TOK), 1)
    tri = jnp.where(col < row, 1.0, 0.0).astype(BF16)
    before = jnp.dot(tri, onehot.astype(BF16), preferred_element_type=F32) + run_ref[...]
    meta = jnp.zeros((TOK, LANES), F32)
    tw = jnp.zeros((TOK, LANES), F32)
    for k in range(TOP_K):
        rank = jnp.sum(jnp.where(lane == idxs[k], before, 0.0), axis=-1, keepdims=True)
        meta = meta + jnp.where(lane == float(k), idxs[k], 0.0) + jnp.where(lane == float(TOP_K + k), rank, 0.0)
        tw = tw + jnp.where(lane == float(k), es[k] / den, 0.0)
    meta_ref[0] = meta.astype(jnp.int32)
    tw_ref[0] = tw
    run = run_ref[...] + jnp.sum(onehot, axis=0, keepdims=True)
    run_ref[...] = run
    cnt_ref[...] = run


def _router(x, g, modsel, w, bvec, nlt, n_exp):
    b, t, d = x.shape
    tokspec = lambda last: pl.BlockSpec((1, TOK, last), lambda i, j: (i, j, 0))
    return pl.pallas_call(
        functools.partial(_router_kernel, n_exp=n_exp),
        grid=(b, t // TOK),
        in_specs=[tokspec(d),
                  pl.BlockSpec((1, d), lambda i, j: (0, 0)),
                  pl.BlockSpec((1, 1, 6, d), lambda i, j: (i, _type_idx(j, nlt), 0, 0)),
                  pl.BlockSpec((d, LANES), lambda i, j: (0, 0)),
                  pl.BlockSpec((1, LANES), lambda i, j: (0, 0))],
        out_specs=[tokspec(d // 2), tokspec(LANES), tokspec(LANES), pl.BlockSpec((1, LANES), lambda i, j: (0, 0))],
        out_shape=[jax.ShapeDtypeStruct((b, t, d // 2), jnp.uint32),
                   jax.ShapeDtypeStruct((b, t, LANES), jnp.int32),
                   jax.ShapeDtypeStruct((b, t, LANES), F32),
                   jax.ShapeDtypeStruct((1, LANES), F32)],
        scratch_shapes=[pltpu.VMEM((1, LANES), F32)],
        compiler_params=_cparams(("arbitrary", "arbitrary")),
        name="router",
    )(x, g, modsel, w, bvec)


def _dispatch_kernel(pos_ref, h_ref, xs_in_ref, xs_ref, sem):
    del xs_in_ref

    def row_copy(r, k):
        return pltpu.make_async_copy(h_ref.at[pl.ds(r, 1)], xs_ref.at[pl.ds(pos_ref[r * TOP_K + k], 1)], sem)

    def start(r, c):
        for k in range(TOP_K):
            row_copy(r, k).start()
        return c

    def wait(r, c):
        for k in range(TOP_K):
            row_copy(r, k).wait()
        return c

    lax.fori_loop(0, TOK, start, 0)
    lax.fori_loop(0, TOK, wait, 0)


def _dispatch(pos, h2, xs0):
    m, d = h2.shape
    return pl.pallas_call(
        _dispatch_kernel,
        grid=(m // TOK,),
        in_specs=[pl.BlockSpec((TOK * TOP_K,), lambda i: (i,), memory_space=pltpu.SMEM),
                  pl.BlockSpec((TOK, d), lambda i: (i, 0)),
                  pl.BlockSpec(memory_space=pl.ANY)],
        out_specs=pl.BlockSpec(memory_space=pl.ANY),
        out_shape=jax.ShapeDtypeStruct(xs0.shape, xs0.dtype),
        scratch_shapes=[pltpu.SemaphoreType.DMA(())],
        input_output_aliases={2: 0},
        compiler_params=_cparams(("arbitrary",)),
        name="dispatch",
    )(pos, h2, xs0)


def _experts_kernel(te_ref, tv_ref, x_ref, wgu_ref, bgu_ref, wd_ref, bd_ref, y_ref, *, d_ff, fchunk):
    i = pl.program_id(0)

    @pl.when(tv_ref[i] == 0)
    def _():
        y_ref[...] = jnp.zeros_like(y_ref)

    @pl.when(tv_ref[i] != 0)
    def _():
        x = jnp.concatenate(_unpack_bf16_pairs(x_ref[...]), axis=1).astype(BF16)
        acc = jnp.zeros(x.shape, F32)
        for c in range(d_ff // fchunk):
            lo = c * fchunk
            g = jnp.dot(x, wgu_ref[0, 0, :, lo:lo + fchunk], preferred_element_type=F32) + bgu_ref[0, :, lo:lo + fchunk]
            lin = (jnp.dot(x, wgu_ref[0, 0, :, d_ff + lo:d_ff + lo + fchunk], preferred_element_type=F32)
                   + bgu_ref[0, :, d_ff + lo:d_ff + lo + fchunk])
            g = jnp.minimum(g, SWIGLU_LIMIT)
            lin = jnp.clip(lin, -SWIGLU_LIMIT, SWIGLU_LIMIT)
            act = g * jax.nn.sigmoid(SWIGLU_ALPHA * g) * (lin + 1.0)
            acc = acc + jnp.dot(act.astype(BF16), wd_ref[0, 0, lo:lo + fchunk, :], preferred_element_type=F32)
        y_ref[...] = _pack_bf16_pairs(acc + bd_ref[0])


def _experts(te, tv, xs, wgu, bgu, wd, bd, layer):
    rows, dh = xs.shape
    _, n_exp, d, ff2 = wgu.shape
    d_ff = ff2 // 2
    grid_spec = pltpu.PrefetchScalarGridSpec(
        num_scalar_prefetch=2,
        grid=(rows // TOK,),
        in_specs=[pl.BlockSpec((TOK, dh), lambda i, te, tv: (i, 0)),
                  pl.BlockSpec((1, 1, d, ff2), lambda i, te, tv: (layer, te[i], 0, 0)),
                  pl.BlockSpec((1, 1, ff2), lambda i, te, tv: (te[i], 0, 0)),
                  pl.BlockSpec((1, 1, d_ff, d), lambda i, te, tv: (layer, te[i], 0, 0)),
                  pl.BlockSpec((1, 1, d), lambda i, te, tv: (te[i], 0, 0))],
        out_specs=pl.BlockSpec((TOK, dh), lambda i, te, tv: (i, 0)),
    )
    return pl.pallas_call(
        functools.partial(_experts_kernel, d_ff=d_ff, fchunk=512),
        grid_spec=grid_spec,
        out_shape=jax.ShapeDtypeStruct((rows, dh), jnp.uint32),
        compiler_params=_cparams(("arbitrary",)),
        name="experts",
    )(te, tv, xs, wgu, bgu.reshape(n_exp, 1, ff2), wd, bd.reshape(n_exp, 1, d))


def _combine_kernel(pos_ref, x_ref, tw_ref, m_ref, ys_ref, o_ref, buf_ref, sem):
    def row_copy(r, k):
        return pltpu.make_async_copy(ys_ref.at[pl.ds(pos_ref[r * TOP_K + k], 1)], buf_ref.at[k, pl.ds(r, 1)], sem)

    def start(r, c):
        for k in range(TOP_K):
            row_copy(r, k).start()
        return c

    def wait(r, c):
        for k in range(TOP_K):
            row_copy(r, k).wait()
        return c

    lax.fori_loop(0, TOK, start, 0)
    lax.fori_loop(0, TOK, wait, 0)
    tw = tw_ref[0]
    acc_lo = acc_hi = None
    for k in range(TOP_K):
        lo, hi = _unpack_bf16_pairs(buf_ref[k])
        wk = tw[:, k:k + 1]
        acc_lo = wk * lo if acc_lo is None else acc_lo + wk * lo
        acc_hi = wk * hi if acc_hi is None else acc_hi + wk * hi
    o_ref[0] = x_ref[0] + m_ref[0, 0][5:6] * jnp.concatenate([acc_lo, acc_hi], axis=1)


def _combine(pos, x, tw, modsel, ys, nlt):
    b, t, d = x.shape
    nt = t // TOK
    return pl.pallas_call(
        _combine_kernel,
        grid=(b, nt),
        in_specs=[pl.BlockSpec((TOK * TOP_K,), lambda i, j: (i * nt + j,), memory_space=pltpu.SMEM),
                  pl.BlockSpec((1, TOK, d), lambda i, j: (i, j, 0)),
                  pl.BlockSpec((1, TOK, LANES), lambda i, j: (i, j, 0)),
                  pl.BlockSpec((1, 1, 6, d), lambda i, j: (i, _type_idx(j, nlt), 0, 0)),
                  pl.BlockSpec(memory_space=pl.ANY)],
        out_specs=pl.BlockSpec((1, TOK, d), lambda i, j: (i, j, 0)),
        out_shape=jax.ShapeDtypeStruct((b, t, d), F32),
        scratch_shapes=[pltpu.VMEM((TOP_K, TOK, d // 2), jnp.uint32), pltpu.SemaphoreType.DMA(())],
        compiler_params=_cparams(("arbitrary", "arbitrary")),
        name="combine",
    )(pos, x, tw, modsel, ys)


def _moe(xa, norm_g, modsel, w_router, b_router, wgu, bgu, wd, bd, nlt, layer):
    b, t, d = xa.shape
    m = b * t
    n_exp = w_router.shape[1]
    wr = jnp.zeros((d, LANES), F32).at[:, :n_exp].set(w_router.astype(F32))
    br = jnp.zeros((1, LANES), F32).at[0, :n_exp].set(b_router.astype(F32))
    h2, meta, tw, cnt = _router(xa, norm_g.reshape(1, d), modsel, wr, br, nlt, n_exp)

    counts = cnt[0, :n_exp].astype(jnp.int32)
    padded = ((counts + TOK - 1) // TOK) * TOK
    gend = jnp.cumsum(padded)
    gstart = gend - padded
    meta2 = meta.reshape(m, LANES)
    pos = (gstart[meta2[:, 0:TOP_K]] + meta2[:, TOP_K:2 * TOP_K]).reshape(m * TOP_K)
    ntiles = (m * TOP_K) // TOK + n_exp
    tile_start = jnp.arange(ntiles, dtype=jnp.int32) * TOK
    te = jnp.sum((tile_start[:, None] >= gend[None, :]).astype(jnp.int32), axis=1)
    te = jnp.minimum(te, n_exp - 1)
    tv = (tile_start < gend[-1]).astype(jnp.int32)

    xs = _dispatch(pos, h2.reshape(m, d // 2), jnp.zeros((ntiles * TOK, d // 2), jnp.uint32))
    ys = _experts(te, tv, xs, wgu, bgu, wd, bd, layer)
    return _combine(pos, xa, tw, modsel, ys, nlt)


def _rope_tables(seq, ctx):
    t = jnp.arange(seq, dtype=jnp.int32)
    row = (t // GRID_W).astype(F32)
    col = (t % GRID_W).astype(F32)
    n_freq = MLA_ROPE // 4
    inv_freq = ROPE_BASE ** (-jnp.arange(n_freq, dtype=F32) / n_freq)
    ang = jnp.concatenate([row[:, None] * inv_freq, col[:, None] * inv_freq], axis=-1)
    cos, sin = jnp.cos(ang), jnp.sin(ang)
    cos2 = jnp.tile(jnp.concatenate([cos, cos], axis=-1), (1, 2))
    sin2 = jnp.tile(jnp.concatenate([-sin, sin], axis=-1), (1, 2))
    cos2 = jnp.concatenate([cos2, jnp.ones((ctx, LANES), F32)], axis=0)
    sin2 = jnp.concatenate([sin2, jnp.zeros((ctx, LANES), F32)], axis=0)
    return cos2, sin2


def _pad_cols(w, width):
    return jnp.pad(w, ((0, 0), (0, width - w.shape[1])))


def _layout_w_in(w):
    o_ckv = MLA_Q_RANK
    o_kpe = o_ckv + MLA_KV_RANK
    o_naq = o_kpe + MLA_ROPE
    o_gate = o_naq + 3 * NA_WIDTH + SSM_WIDTH
    parts = [w[:, o_gate:],
             w[:, :o_ckv],
             _pad_cols(w[:, o_kpe:o_naq], ZC_CKV - ZC_KPE),
             w[:, o_ckv:o_kpe],
             w[:, o_naq:o_gate]]
    out = jnp.concatenate(parts, axis=1)
    assert out.shape[1] == ZW
    return out.astype(BF16)


def kernel(x, c, ctx, c_ctx, w_ada, b_ada, norm1_g, w_in, mla_q_norm, mla_kv_norm, mla_w_uq, mla_w_ukv, mla_q_gain, mla_k_gain, na_q_gain, na_k_gain, na_rpb, ssm_lam_re, ssm_lam_im, ssm_log_dt, ssm_b_re, ssm_b_im, ssm_c_re, ssm_c_im, ssm_d, ssm_w_glu, ssm_b_glu, w_pa, w_pb, w_pc, w_out, norm2_g, w_router, b_router, w_gate_up, b_gate_up, w_down, b_down):
    bsz, seq, d = x.shape
    nctx = ctx.shape[1]
    depth = w_ada.shape[0]
    assert nctx == TOK and seq % (NA_QROWS * GRID_W) == 0 and seq // GRID_W >= NA_KROWS
    nlt = seq // TOK
    t = seq + nctx
    m = bsz * t
    rows = seq // GRID_W
    nblk = rows // NA_QROWS
    nchunk = t // SSM_CHUNK

    cos2, sin2 = _rope_tables(seq, nctx)
    cvec = jnp.zeros((16, d), F32).at[:bsz].set(c.astype(F32)).at[bsz].set(c_ctx.astype(F32))
    xa = jnp.concatenate([x, ctx], axis=1).astype(F32)
    row128 = lambda v: v.astype(F32).reshape(1, LANES)
    perm = jnp.asarray(_ssm_perm(), BF16)
    wgu_all = w_gate_up.astype(BF16)
    wd_all = w_down.astype(BF16)

    for l in range(depth):
        need_ctx = l < depth - 1
        ntile = t // TOK if need_ctx else nlt
        mod = _ada(cvec, w_ada[l], b_ada[l])
        mod_lat = mod[:bsz].reshape(bsz, 1, 6, d)
        mod_ctx = jnp.broadcast_to(mod[bsz].reshape(1, 1, 6, d), (bsz, 1, 6, d))
        modsel = jnp.concatenate([mod_lat, mod_ctx], axis=1)

        h1 = _norm_mod(xa, norm1_g[l], modsel, nlt)
        z = _matmul(h1.reshape(m, d), _layout_w_in(w_in[l]), BF16, "in_proj", ZW // 4)
        z3 = z.reshape(bsz, t, ZW)

        wq = mla_w_uq[l].reshape(MLA_Q_RANK, MLA_HEADS, MLA_QK)
        wq = jnp.concatenate([wq[:, :, :MLA_NOPE].reshape(MLA_Q_RANK, -1),
                              wq[:, :, MLA_NOPE:].reshape(MLA_Q_RANK, -1)], axis=1).astype(BF16)
        qg_r = jnp.tile(mla_q_gain[l][MLA_NOPE:], 2)
        kg_r = jnp.concatenate([mla_k_gain[l][MLA_NOPE:], jnp.zeros((MLA_ROPE,), mla_k_gain.dtype)])
        qf = _mla_q(z3, mla_q_norm[l].astype(F32).reshape(1, -1), wq, row128(mla_q_gain[l][:MLA_NOPE]),
                    row128(qg_r), cos2, sin2, ntile)
        kf, vf = _mla_kv(z3, mla_kv_norm[l].astype(F32).reshape(1, -1), mla_w_ukv[l].astype(BF16),
                         row128(mla_k_gain[l][:MLA_NOPE]), row128(kg_r), cos2, sin2)
        o_a = _mla_attn(qf, kf, vf, seq, need_ctx)

        bias = _na_bias_tables(na_rpb[l], rows)
        qn, kn = _na_norm(z3, row128(jnp.tile(na_q_gain[l], 2)), row128(jnp.tile(na_k_gain[l], 2)))
        o_b = _na_attn(qn, kn, z3, bias, nblk, need_ctx)

        wb2, mt2, wc2, d2, ar2, ai2 = _ssm_params(ssm_lam_re[l], ssm_lam_im[l], ssm_log_dt[l], ssm_b_re[l],
                                                   ssm_b_im[l], ssm_c_re[l], ssm_c_im[l], ssm_d[l])
        r = nchunk * bsz
        u = z3[:, :, ZC_SSM:ZC_SSM + SSM_WIDTH].reshape(bsz, nchunk, SSM_CHUNK, SSM_NSUPER, LANES)
        x8 = jnp.transpose(u, (3, 1, 0, 2, 4)).reshape(SSM_NSUPER * r, SSM_CW)
        xp = _matmul(x8, perm, BF16, "ssm_perm_in", SSM_CW)
        yp = _ssm(xp.reshape(SSM_NSUPER, r, SSM_CW), wb2, mt2, wc2, d2, ar2, ai2, bsz, nchunk, nctx // SSM_CHUNK)
        y8 = _matmul(yp.reshape(SSM_NSUPER * r, SSM_CW), perm.T, BF16, "ssm_perm_out", SSM_CW)
        y = jnp.transpose(y8.reshape(SSM_NSUPER, nchunk, bsz, SSM_CHUNK, LANES), (2, 1, 3, 0, 4))
        o_s = _glu(y.reshape(bsz, t, SSM_WIDTH), ssm_w_glu[l].astype(BF16), ssm_b_glu[l].astype(F32), ntile)

        mix = _merge(o_a, o_b, o_s, z3, w_pa[l].astype(BF16), w_pb[l].astype(BF16), w_pc[l].astype(BF16), ntile)
        xa = _outproj(xa, mix, w_out[l].astype(BF16), modsel, nlt, ntile)
        xa = _moe(xa, norm2_g[l].astype(F32), modsel, w_router[l], b_router[l], wgu_all, b_gate_up[l].astype(F32),
                  wd_all, b_down[l].astype(F32), nlt, l)
    return xa[:, :seq].astype(x.dtype)
```

```python
import functools
import math

import numpy as np
import jax
import jax.numpy as jnp
from jax import lax
from jax.experimental import pallas as pl
from jax.experimental.pallas import tpu as pltpu

F32 = jnp.float32
BF16 = jnp.bfloat16

GRID_W = 64
MLA_HEADS = 8
MLA_NOPE = 128
MLA_ROPE = 64
MLA_V = 128
MLA_QK = MLA_NOPE + MLA_ROPE
MLA_Q_RANK = 768
MLA_KV_RANK = 512
NA_HEADS = 16
NA_HEAD_DIM = 64
NA_WIDTH = NA_HEADS * NA_HEAD_DIM
NA_WIN_R = 8
NA_WIN_C = 16
SSM_WIDTH = 1024
SSM_GROUP = 16
SSM_GROUPS = SSM_WIDTH // SSM_GROUP
SSM_STATE = 64
TOP_K = 4
SWIGLU_LIMIT = 7.0
SWIGLU_ALPHA = 1.702
ROPE_BASE = 10000.0
EPS = 1e-6
MLA_SCALE = MLA_QK ** -0.5
NA_SCALE = NA_HEAD_DIM ** -0.5
NEG_BIG = -1e30

LANES = 128
TOK = 256
MLA_KPAD = 256
MLA_VPAD = 256
MLA_TQ = 1024
MLA_TK = 512
VMEM_LIMIT = 56 * 1024 * 1024
DMA_UNROLL = 4

ZC_GATE = 0
ZC_CQ = 6144
ZC_KPE = 6912
ZC_CKV = 7168
ZC_NAQ = 7680
ZC_NAK = 8704
ZC_NAV = 9728
ZC_SSM = 10752
ZW = 11776

SSM_CHUNK = 16
SSM_NSUPER = SSM_WIDTH // LANES
SSM_CW = SSM_CHUNK * LANES
SSM_PW = 2 * SSM_CHUNK * SSM_GROUP
NA_QROWS = 4
NA_KROWS = 12
NA_BB = 2


def _cparams(sem, vmem=VMEM_LIMIT):
    return pltpu.CompilerParams(dimension_semantics=sem, vmem_limit_bytes=vmem)


def _rms(x, eps=EPS):
    return x * lax.rsqrt(jnp.mean(x * x, axis=-1, keepdims=True) + eps)


def _seg64_rms(t):
    lane = lax.broadcasted_iota(jnp.int32, t.shape, 1)
    left = lane < 64
    sq = t * t
    s_all = jnp.sum(sq, axis=-1, keepdims=True)
    s_left = jnp.sum(jnp.where(left, sq, 0.0), axis=-1, keepdims=True)
    ms = jnp.where(left, s_left, s_all - s_left) * (1.0 / 64.0)
    return t * lax.rsqrt(ms + EPS)


def _rope64(t, cos2, sin2):
    lane = lax.broadcasted_iota(jnp.int32, t.shape, 1)
    first = (lane % 64) < 32
    swapped = jnp.where(first, pltpu.roll(t, 96, axis=1), pltpu.roll(t, 32, axis=1))
    return t * cos2 + swapped * sin2


def _ones_col_tile(rows, dtype):
    lane = lax.broadcasted_iota(jnp.int32, (rows, LANES), 1)
    return jnp.where(lane == 0, 1.0, 0.0).astype(dtype)


def _qkt(q, k):
    return lax.dot_general(q, k, (((1,), (1,)), ((), ())), preferred_element_type=F32)


def _softmax_acc(s, v_ext):
    m = jnp.max(s, axis=-1, keepdims=True)
    return jnp.dot(jnp.exp((s - m).astype(BF16)), v_ext, preferred_element_type=F32)


def _softmax_pv(s, v_ext):
    oe = _softmax_acc(s, v_ext)
    return oe[:, 0:LANES] / oe[:, LANES:LANES + 1]


def _pack_bf16_pairs(x):
    n = x.shape[1] // 2
    r = x.astype(BF16).astype(F32)
    lo = lax.bitcast_convert_type(r[:, :n], jnp.uint32) >> 16
    hi = lax.bitcast_convert_type(r[:, n:], jnp.uint32) & jnp.uint32(0xFFFF0000)
    return hi | lo


def _unpack_bf16_pairs(w):
    lo = lax.bitcast_convert_type(w << 16, F32)
    hi = lax.bitcast_convert_type(w & jnp.uint32(0xFFFF0000), F32)
    return lo, hi


def _type_idx(j, nlt):
    return jnp.where(j >= nlt, 1, 0)


def _ada_kernel(c_ref, w_ref, b_ref, o_ref):
    c = c_ref[...]
    a = (c * jax.nn.sigmoid(c)).astype(BF16)
    o_ref[...] = jnp.dot(a, w_ref[...].astype(BF16), preferred_element_type=F32) + b_ref[...]


def _ada(cvec, w, b):
    rows, d = cvec.shape
    n = w.shape[1]
    tn = 1024
    return pl.pallas_call(
        _ada_kernel,
        grid=(n // tn,),
        in_specs=[pl.BlockSpec((rows, d), lambda j: (0, 0)),
                  pl.BlockSpec((d, tn), lambda j: (0, j)),
                  pl.BlockSpec((1, tn), lambda j: (0, j))],
        out_specs=pl.BlockSpec((rows, tn), lambda j: (0, j)),
        out_shape=jax.ShapeDtypeStruct((rows, n), F32),
        compiler_params=_cparams(("arbitrary",)),
        name="ada",
    )(cvec, w, b.reshape(1, n))


def _norm_mod_kernel(x_ref, g_ref, m_ref, o_ref):
    m = m_ref[0, 0]
    y = _rms(x_ref[0]) * g_ref[...]
    o_ref[0] = (y * (1.0 + m[1:2]) + m[0:1]).astype(o_ref.dtype)


def _norm_mod(x, g, modsel, nlt):
    b, t, d = x.shape
    return pl.pallas_call(
        _norm_mod_kernel,
        grid=(b, t // TOK),
        in_specs=[pl.BlockSpec((1, TOK, d), lambda i, j: (i, j, 0)),
                  pl.BlockSpec((1, d), lambda i, j: (0, 0)),
                  pl.BlockSpec((1, 1, 6, d), lambda i, j: (i, _type_idx(j, nlt), 0, 0))],
        out_specs=pl.BlockSpec((1, TOK, d), lambda i, j: (i, j, 0)),
        out_shape=jax.ShapeDtypeStruct((b, t, d), BF16),
        compiler_params=_cparams(("arbitrary", "arbitrary")),
        name="norm_mod",
    )(x, g.reshape(1, d), modsel)


def _mm_kernel(a_ref, w_ref, o_ref):
    o_ref[...] = jnp.dot(a_ref[...], w_ref[...], preferred_element_type=F32).astype(o_ref.dtype)


def _pick(n, cands):
    for c in cands:
        if n % c == 0:
            return c
    return n


def _matmul(a, w, out_dtype, name, tn):
    m, k = a.shape
    n = w.shape[1]
    tm = _pick(m, (512, 256, 128))
    return pl.pallas_call(
        _mm_kernel,
        grid=(n // tn, m // tm),
        in_specs=[pl.BlockSpec((tm, k), lambda j, i: (i, 0)),
                  pl.BlockSpec((k, tn), lambda j, i: (0, j))],
        out_specs=pl.BlockSpec((tm, tn), lambda j, i: (i, j)),
        out_shape=jax.ShapeDtypeStruct((m, n), out_dtype),
        compiler_params=_cparams(("arbitrary", "arbitrary")),
        name=name,
    )(a, w)


def _mla_q_kernel(z_ref, g_ref, w_ref, gn_ref, gr_ref, cos_ref, sin_ref, o_ref):
    cq = z_ref[0].astype(F32)
    xn = (_rms(cq) * g_ref[...]).astype(BF16)
    q = jnp.dot(xn, w_ref[...], preferred_element_type=F32)
    lane = lax.broadcasted_iota(jnp.int32, (TOK, LANES), 1)
    left = lane < 64
    for h in range(MLA_HEADS):
        qn = q[:, h * MLA_NOPE:(h + 1) * MLA_NOPE]
        o_ref[0, h, :, 0:LANES] = (_rms(qn) * (gn_ref[...] * MLA_SCALE)).astype(o_ref.dtype)
    base = MLA_HEADS * MLA_NOPE
    for hp in range(MLA_HEADS // 2):
        t = q[:, base + hp * LANES: base + (hp + 1) * LANES]
        t = _seg64_rms(t) * (gr_ref[...] * MLA_SCALE)
        r = _rope64(t, cos_ref[...], sin_ref[...])
        o_ref[0, 2 * hp, :, LANES:2 * LANES] = jnp.where(left, r, 0.0).astype(o_ref.dtype)
        o_ref[0, 2 * hp + 1, :, LANES:2 * LANES] = jnp.where(left, pltpu.roll(r, 64, axis=1), 0.0).astype(o_ref.dtype)


def _mla_q(z3, g, w, gn, gr2, cos2, sin2, ntile):
    b, t, _ = z3.shape
    return pl.pallas_call(
        _mla_q_kernel,
        grid=(b, ntile),
        in_specs=[pl.BlockSpec((1, TOK, MLA_Q_RANK), lambda i, j: (i, j, ZC_CQ // MLA_Q_RANK)),
                  pl.BlockSpec((1, MLA_Q_RANK), lambda i, j: (0, 0)),
                  pl.BlockSpec(w.shape, lambda i, j: (0, 0)),
                  pl.BlockSpec((1, LANES), lambda i, j: (0, 0)),
                  pl.BlockSpec((1, LANES), lambda i, j: (0, 0)),
                  pl.BlockSpec((TOK, LANES), lambda i, j: (j, 0)),
                  pl.BlockSpec((TOK, LANES), lambda i, j: (j, 0))],
        out_specs=pl.BlockSpec((1, MLA_HEADS, TOK, MLA_KPAD), lambda i, j: (i, 0, j, 0)),
        out_shape=jax.ShapeDtypeStruct((b, MLA_HEADS, ntile * TOK, MLA_KPAD), BF16),
        compiler_params=_cparams(("arbitrary", "arbitrary")),
        name="mla_q",
    )(z3, g, w, gn, gr2, cos2, sin2)


def _mla_kv_kernel(z_ref, zk_ref, g_ref, w_ref, gn_ref, gr_ref, cos_ref, sin_ref, k_ref, v_ref):
    ckv = z_ref[0].astype(F32)
    xn = (_rms(ckv) * g_ref[...]).astype(BF16)
    kv = jnp.dot(xn, w_ref[...], preferred_element_type=F32)
    lane = lax.broadcasted_iota(jnp.int32, (TOK, LANES), 1)
    left = lane < 64
    t = jnp.where(left, zk_ref[0].astype(F32), 0.0)
    ms = jnp.sum(t * t, axis=-1, keepdims=True) * (1.0 / MLA_ROPE)
    kr = t * lax.rsqrt(ms + EPS) * gr_ref[...]
    kr = jnp.where(left, _rope64(kr, cos_ref[...], sin_ref[...]), 0.0).astype(k_ref.dtype)
    ones = _ones_col_tile(TOK, v_ref.dtype)
    hw = MLA_NOPE + MLA_V
    for h in range(MLA_HEADS):
        kn = kv[:, h * hw: h * hw + MLA_NOPE]
        k_ref[0, h, :, 0:LANES] = (_rms(kn) * gn_ref[...]).astype(k_ref.dtype)
        k_ref[0, h, :, LANES:2 * LANES] = kr
        v_ref[0, h, :, 0:LANES] = kv[:, h * hw + MLA_NOPE:(h + 1) * hw].astype(v_ref.dtype)
        v_ref[0, h, :, LANES:2 * LANES] = ones


def _mla_kv(z3, g, w, gn, gr2, cos2, sin2):
    b, t, _ = z3.shape
    return pl.pallas_call(
        _mla_kv_kernel,
        grid=(b, t // TOK),
        in_specs=[pl.BlockSpec((1, TOK, MLA_KV_RANK), lambda i, j: (i, j, ZC_CKV // MLA_KV_RANK)),
                  pl.BlockSpec((1, TOK, LANES), lambda i, j: (i, j, ZC_KPE // LANES)),
                  pl.BlockSpec((1, MLA_KV_RANK), lambda i, j: (0, 0)),
                  pl.BlockSpec(w.shape, lambda i, j: (0, 0)),
                  pl.BlockSpec((1, LANES), lambda i, j: (0, 0)),
                  pl.BlockSpec((1, LANES), lambda i, j: (0, 0)),
                  pl.BlockSpec((TOK, LANES), lambda i, j: (j, 0)),
                  pl.BlockSpec((TOK, LANES), lambda i, j: (j, 0))],
        out_specs=[pl.BlockSpec((1, MLA_HEADS, TOK, MLA_KPAD), lambda i, j: (i, 0, j, 0)),
                   pl.BlockSpec((1, MLA_HEADS, TOK, MLA_VPAD), lambda i, j: (i, 0, j, 0))],
        out_shape=[jax.ShapeDtypeStruct((b, MLA_HEADS, t, MLA_KPAD), BF16),
                   jax.ShapeDtypeStruct((b, MLA_HEADS, t, MLA_VPAD), BF16)],
        compiler_params=_cparams(("arbitrary", "arbitrary")),
        name="mla_kv",
    )(z3, z3, g, w, gn, gr2, cos2, sin2)


def _mla_attn_kernel(q_ref, k_ref, v_ref, o_ref, *, nk, tk):
    q = q_ref[0, 0]
    m = acc = None
    for c0 in range(0, nk, tk):
        sz = min(tk, nk - c0)
        s = _qkt(q, k_ref[0, 0, c0:c0 + sz, :])
        cm = jnp.max(s, axis=-1, keepdims=True)
        m_new = cm if m is None else jnp.maximum(m, cm)
        pv = jnp.dot(jnp.exp((s - m_new).astype(BF16)), v_ref[0, 0, c0:c0 + sz, :], preferred_element_type=F32)
        acc = pv if m is None else acc * jnp.exp(m - m_new) + pv
        m = m_new
    o_ref[0] = (acc[:, 0:LANES] / acc[:, LANES:LANES + 1]).astype(o_ref.dtype)


def _mla_attn_ctx_kernel(q_ref, k_ref, v_ref, o_ref):
    o_ref[0] = _softmax_pv(_qkt(q_ref[0, 0], k_ref[0, 0]), v_ref[0, 0]).astype(o_ref.dtype)


def _mla_attn(q, k, v, seq, need_ctx):
    b, h, t, _ = k.shape
    tq = _pick(seq, (MLA_TQ, TOK))
    sem = ("arbitrary", "arbitrary", "arbitrary")
    o = pl.pallas_call(
        functools.partial(_mla_attn_kernel, nk=t, tk=MLA_TK),
        grid=(b, h, seq // tq),
        in_specs=[pl.BlockSpec((1, 1, tq, MLA_KPAD), lambda i, hh, j: (i, hh, j, 0)),
                  pl.BlockSpec((1, 1, t, MLA_KPAD), lambda i, hh, j: (i, hh, 0, 0)),
                  pl.BlockSpec((1, 1, t, MLA_VPAD), lambda i, hh, j: (i, hh, 0, 0))],
        out_specs=pl.BlockSpec((1, tq, MLA_V), lambda i, hh, j: (i, j, hh)),
        out_shape=jax.ShapeDtypeStruct((b, seq, h * MLA_V), BF16),
        compiler_params=_cparams(sem),
        name="mla_attn",
    )(q, k, v)
    if not need_ctx:
        return o
    nctx = t - seq
    ctx_spec = lambda w: pl.BlockSpec((1, 1, nctx, w), lambda i, hh, j: (i, hh, seq // nctx, 0))
    o_c = pl.pallas_call(
        _mla_attn_ctx_kernel,
        grid=(b, h, 1),
        in_specs=[ctx_spec(MLA_KPAD), ctx_spec(MLA_KPAD), ctx_spec(MLA_VPAD)],
        out_specs=pl.BlockSpec((1, nctx, MLA_V), lambda i, hh, j: (i, 0, hh)),
        out_shape=jax.ShapeDtypeStruct((b, nctx, h * MLA_V), BF16),
        compiler_params=_cparams(sem),
        name="mla_attn_ctx",
    )(q, k, v)
    return jnp.concatenate([o, o_c], axis=1)


def _na_norm_kernel(q_ref, k_ref, gq_ref, gk_ref, qn_ref, kn_ref):
    for c in range(q_ref.shape[2] // LANES):
        sl = slice(c * LANES, (c + 1) * LANES)
        qn_ref[0, :, sl] = (_seg64_rms(q_ref[0, :, sl].astype(F32)) * (gq_ref[...] * NA_SCALE)).astype(qn_ref.dtype)
        kn_ref[0, :, sl] = (_seg64_rms(k_ref[0, :, sl].astype(F32)) * gk_ref[...]).astype(kn_ref.dtype)


def _na_norm(z3, gq2, gk2):
    b, t, _ = z3.shape
    w = 512
    zspec = lambda col0: pl.BlockSpec((1, TOK, w), lambda i, j, c: (i, j, col0 // w + c))
    ospec = pl.BlockSpec((1, TOK, w), lambda i, j, c: (i, j, c))
    gspec = pl.BlockSpec((1, LANES), lambda i, j, c: (0, 0))
    oshape = jax.ShapeDtypeStruct((b, t, NA_WIDTH), BF16)
    return pl.pallas_call(
        _na_norm_kernel,
        grid=(b, t // TOK, NA_WIDTH // w),
        in_specs=[zspec(ZC_NAQ), zspec(ZC_NAK), gspec, gspec],
        out_specs=[ospec, ospec],
        out_shape=[oshape, oshape],
        compiler_params=_cparams(("arbitrary", "arbitrary", "arbitrary")),
        name="na_norm",
    )(z3, z3, gq2, gk2)


def _na_kernel(q_ref, k0_ref, k1_ref, k2_ref, kc_ref, v0_ref, v1_ref, v2_ref, vc_ref, bias_ref, o_ref, *, nblk):
    j = pl.program_id(1)
    lane = lax.broadcasted_iota(jnp.int32, (TOK, LANES), 1)
    left = lane < 64
    ones = _ones_col_tile(TOK, BF16)
    nb = q_ref.shape[0]

    def stacked_q(bi):
        qn = q_ref[bi]
        zero = jnp.zeros_like(qn)
        return jnp.concatenate([jnp.where(left, qn, zero), jnp.where(left, zero, qn)], axis=0)

    def finish(bi, acc):
        out = acc[:, 0:LANES] / acc[:, LANES:LANES + 1]
        o_ref[bi] = jnp.where(left, out[0:TOK], out[TOK:2 * TOK]).astype(o_ref.dtype)

    @pl.when(j >= nblk)
    def _():
        for bi in range(nb):
            v_ext = jnp.concatenate([vc_ref[bi], ones], axis=1)
            finish(bi, _softmax_acc(_qkt(stacked_q(bi), kc_ref[bi]), v_ext))

    @pl.when(j < nblk)
    def _():
        q2 = [stacked_q(bi) for bi in range(nb)]
        m = [None] * nb
        acc = [None] * nb
        for c, (k_ref, v_ref) in enumerate(((k0_ref, v0_ref), (k1_ref, v1_ref), (k2_ref, v2_ref), (kc_ref, vc_ref))):
            bias = bias_ref[0, :, :, c * TOK:(c + 1) * TOK].reshape(2 * TOK, TOK)
            for bi in range(nb):
                v_ext = jnp.concatenate([v_ref[bi], ones], axis=1)
                s = _qkt(q2[bi], k_ref[bi]) + bias
                cm = jnp.max(s, axis=-1, keepdims=True)
                m_new = cm if m[bi] is None else jnp.maximum(m[bi], cm)
                pv = jnp.dot(jnp.exp((s - m_new).astype(BF16)), v_ext, preferred_element_type=F32)
                acc[bi] = pv if m[bi] is None else acc[bi] * jnp.exp(m[bi] - m_new) + pv
                m[bi] = m_new
        for bi in range(nb):
            finish(bi, acc[bi])


def _na_attn(qn, kn, z3, bias, nblk, need_ctx):
    b = z3.shape[0]
    bb = _pick(b, (NA_BB, 1))
    kb = lambda j: jnp.clip(j - 1, 0, nblk - 3)
    nj = nblk + (1 if need_ctx else 0)

    def tspec(col_blk0, tok_idx):
        return pl.BlockSpec((bb, TOK, LANES), lambda hp, j, i: (i, tok_idx(j), col_blk0 + hp))

    var = lambda j: jnp.where(j == 0, 0, jnp.where(j == nblk - 1, 2, 1))
    window = lambda c0: [tspec(c0, (lambda j, d=d: kb(j) + d)) for d in range(3)] + [tspec(c0, lambda j: nblk)]
    in_specs = [tspec(0, lambda j: j)] + window(0) + window(ZC_NAV // LANES)
    in_specs += [pl.BlockSpec((1, 2, TOK, 4 * TOK), lambda hp, j, i: (var(j), hp, 0, 0))]
    return pl.pallas_call(
        functools.partial(_na_kernel, nblk=nblk),
        grid=(NA_HEADS // 2, nj, b // bb),
        in_specs=in_specs,
        out_specs=pl.BlockSpec((bb, TOK, LANES), lambda hp, j, i: (i, j, hp)),
        out_shape=jax.ShapeDtypeStruct((b, nj * TOK, NA_WIDTH), BF16),
        compiler_params=_cparams(("arbitrary", "arbitrary", "arbitrary")),
        name="na_attn",
    )(qn, kn, kn, kn, kn, z3, z3, z3, z3, bias)


def _na_bias_tables(rpb, rows):
    nblk = rows // NA_QROWS
    nh = rpb.shape[0]
    ncol = 2 * NA_WIN_C - 1
    dd = np.arange(ncol)[:, None, None]
    qc1 = np.arange(GRID_W)[None, :, None]
    kc1 = np.arange(GRID_W)[None, None, :]
    onehot = (kc1 - qc1 + NA_WIN_C - 1 == dd).astype(np.float32)
    toep = jnp.einsum('hrd,dqk->hrqk', rpb.astype(F32), onehot, precision=lax.Precision.HIGHEST)
    c0 = np.clip(np.arange(GRID_W) - NA_WIN_C // 2, 0, GRID_W - NA_WIN_C)
    kcs = np.arange(GRID_W)
    valid_c = (kcs[None, :] >= c0[:, None]) & (kcs[None, :] < c0[:, None] + NA_WIN_C)
    tabs = []
    for j in (0, 1, nblk - 1):
        start = int(np.clip(NA_QROWS * j - NA_QROWS, 0, rows - NA_KROWS))
        qr = NA_QROWS * j + np.arange(NA_QROWS)
        kr = start + np.arange(NA_KROWS)
        r0 = np.clip(qr - NA_WIN_R // 2, 0, rows - NA_WIN_R)
        valid_r = (kr[None, :] >= r0[:, None]) & (kr[None, :] < r0[:, None] + NA_WIN_R)
        drow = np.clip(kr[None, :] - qr[:, None] + NA_WIN_R - 1, 0, 2 * NA_WIN_R - 2).reshape(-1)
        sel = jnp.take(toep, drow, axis=1).reshape(nh, NA_QROWS, NA_KROWS, GRID_W, GRID_W)
        sel = jnp.transpose(sel, (0, 1, 3, 2, 4)).reshape(nh, TOK, NA_KROWS * GRID_W)
        valid = (valid_r[:, None, :, None] & valid_c[None, :, None, :]).reshape(TOK, NA_KROWS * GRID_W)
        loc = jnp.where(valid[None], sel, NEG_BIG)
        tabs.append(jnp.concatenate([loc, jnp.zeros((nh, TOK, TOK), F32)], axis=-1))
    return jnp.stack(tabs)


def _ssm_kernel(u_ref, wb_ref, mt_ref, wc_ref, d_ref, ar_ref, ai_ref, y_ref, sh_ref, *, bsz, nchunk, ncc):
    u = u_ref[0]
    sh_ref[...] = jnp.dot(u, wb_ref[0], preferred_element_type=F32)
    a_r = ar_ref[0]
    a_i = ai_ref[0]
    arf, aif, arr, air = a_r[:, 0:LANES], a_i[:, 0:LANES], a_r[:, LANES:], a_i[:, LANES:]
    nlc = nchunk - ncc

    def body(i, carry):
        hre_f, him_f, hre_r, him_r = carry
        cf = jnp.where(i < ncc, nlc + i, i - ncc)
        cr = nchunk - 1 - i
        rf = pl.multiple_of(cf * bsz, bsz)
        rr = pl.multiple_of(cr * bsz, bsz)
        sre_f = sh_ref[pl.ds(rf, bsz), 0:LANES]
        sim_f = sh_ref[pl.ds(rf, bsz), LANES:2 * LANES]
        sre_r = sh_ref[pl.ds(rr, bsz), 2 * LANES:3 * LANES]
        sim_r = sh_ref[pl.ds(rr, bsz), 3 * LANES:4 * LANES]
        sh_ref[pl.ds(rf, bsz), 0:LANES] = hre_f
        sh_ref[pl.ds(rf, bsz), LANES:2 * LANES] = him_f
        sh_ref[pl.ds(rr, bsz), 2 * LANES:3 * LANES] = hre_r
        sh_ref[pl.ds(rr, bsz), 3 * LANES:4 * LANES] = him_r
        return (arf * hre_f - aif * him_f + sre_f, arf * him_f + aif * hre_f + sim_f,
                arr * hre_r - air * him_r + sre_r, arr * him_r + air * hre_r + sim_r)

    zero = jnp.zeros((bsz, LANES), F32)
    lax.fori_loop(0, nchunk, body, (zero, zero, zero, zero))
    y = jnp.dot(u, mt_ref[0], preferred_element_type=F32)
    y = y + jnp.dot(sh_ref[...].astype(BF16), wc_ref[0], preferred_element_type=F32)
    y_ref[0] = (y + d_ref[0] * u.astype(F32)).astype(y_ref.dtype)


def _ssm(xp, wb2, mt2, wc2, d2, ar2, ai2, bsz, nchunk, ncc):
    ns, r, cw = xp.shape
    pw = SSM_PW
    nq = cw // pw
    wspec = lambda shp: pl.BlockSpec((1,) + shp, lambda g, q: (g * nq + q, 0, 0))
    return pl.pallas_call(
        functools.partial(_ssm_kernel, bsz=bsz, nchunk=nchunk, ncc=ncc),
        grid=(ns, nq),
        in_specs=[pl.BlockSpec((1, r, pw), lambda g, q: (g, 0, q)),
                  wspec((pw, pw)), wspec((pw, pw)), wspec((pw, pw)),
                  wspec((1, pw)), wspec((1, 2 * LANES)), wspec((1, 2 * LANES))],
        out_specs=pl.BlockSpec((1, r, pw), lambda g, q: (g, 0, q)),
        out_shape=jax.ShapeDtypeStruct((ns, r, cw), BF16),
        scratch_shapes=[pltpu.VMEM((r, pw), F32)],
        compiler_params=_cparams(("arbitrary", "arbitrary")),
        name="ssm",
    )(xp, wb2, mt2, wc2, d2, ar2, ai2)


def _ssm_perm():
    r = np.arange(SSM_CW)
    s, g, mm = r // LANES, (r % LANES) // SSM_GROUP, r % SSM_GROUP
    dst = g * (SSM_CHUNK * SSM_GROUP) + s * SSM_GROUP + mm
    return (dst[:, None] == np.arange(SSM_CW)[None, :]).astype(np.float32)


def _ssm_params(lam_re, lam_im, log_dt, b_re, b_im, c_re, c_im, d_skip):
    L = SSM_CHUNK
    lam_re, lam_im = lam_re.astype(F32), lam_im.astype(F32)
    dt = jnp.exp(log_dt.astype(F32))[..., None]
    tau = jnp.arange(L + 1, dtype=F32)[:, None, None, None]
    mag = jnp.exp(lam_re * dt * tau)
    pw_re = mag * jnp.cos(lam_im * dt * tau)
    pw_im = mag * jnp.sin(lam_im * dt * tau)
    a_re, a_im = pw_re[1], pw_im[1]
    den = lam_re * lam_re + lam_im * lam_im
    z_re = ((a_re - 1.0) * lam_re + a_im * lam_im) / den
    z_im = (a_im * lam_re - (a_re - 1.0) * lam_im) / den
    bb_re = z_re[..., None] * b_re.astype(F32) - z_im[..., None] * b_im.astype(F32)
    bb_im = z_re[..., None] * b_im.astype(F32) + z_im[..., None] * b_re.astype(F32)
    cr, ci = c_re.astype(F32), c_im.astype(F32)

    ab_re = pw_re[..., None] * bb_re - pw_im[..., None] * bb_im
    ab_im = pw_re[..., None] * bb_im + pw_im[..., None] * bb_re
    ktap = (jnp.einsum('dgnp,tdgpm->tdgnm', cr, ab_re) - jnp.einsum('dgnp,tdgpm->tdgnm', ci, ab_im))[:L]
    i_idx = np.arange(L)[:, None]
    s_idx = np.arange(L)[None, :]
    lag_f = i_idx - s_idx
    lag_r = s_idx - i_idx
    kf = jnp.where((lag_f >= 0)[:, :, None, None, None], ktap[np.clip(lag_f, 0, L - 1), 0], 0.0)
    kr = jnp.where((lag_r >= 0)[:, :, None, None, None], ktap[np.clip(lag_r, 0, L - 1), 1], 0.0)
    w1 = L * SSM_GROUP
    mt = jnp.transpose(kf + kr, (2, 1, 4, 0, 3)).reshape(SSM_GROUPS, w1, w1)

    exp_f = L - 1 - np.arange(L)
    exp_r = np.arange(L)
    inj = lambda ab, d, exps: jnp.transpose(ab[exps, d], (1, 0, 3, 2)).reshape(SSM_GROUPS, w1, SSM_STATE)
    wb_parts = [inj(ab_re, 0, exp_f), inj(ab_im, 0, exp_f), inj(ab_re, 1, exp_r), inj(ab_im, 1, exp_r)]

    def ro(d, exps):
        pr, pi = pw_re[exps, d], pw_im[exps, d]
        e_re = cr[d][None] * pr[:, :, None, :] - ci[d][None] * pi[:, :, None, :]
        e_im = cr[d][None] * pi[:, :, None, :] + ci[d][None] * pr[:, :, None, :]
        f = lambda e: jnp.transpose(e, (1, 3, 0, 2)).reshape(SSM_GROUPS, SSM_STATE, w1)
        return [f(e_re), f(-e_im)]

    wc_parts = ro(0, np.arange(L) + 1) + ro(1, L - np.arange(L))
    npair = SSM_GROUPS // 2

    def pair_diag(x):
        x = x.reshape(npair, 2, x.shape[1], x.shape[2])
        zed = jnp.zeros_like(x[:, 0])
        return jnp.concatenate([jnp.concatenate([x[:, 0], zed], axis=2),
                                jnp.concatenate([zed, x[:, 1]], axis=2)], axis=1)

    wb2 = jnp.concatenate([pair_diag(p) for p in wb_parts], axis=2)
    wc2 = jnp.concatenate([pair_diag(p) for p in wc_parts], axis=1)
    mt2 = pair_diag(mt)
    d2 = jnp.tile(d_skip.astype(F32).reshape(SSM_GROUPS, 1, SSM_GROUP), (1, L, 1)).reshape(npair, 1, 2 * w1)
    packp = lambda x: x.reshape(2, npair, 2 * SSM_STATE)
    ar2 = jnp.concatenate([packp(pw_re[L])[0], packp(pw_re[L])[1]], axis=-1)[:, None, :]
    ai2 = jnp.concatenate([packp(pw_im[L])[0], packp(pw_im[L])[1]], axis=-1)[:, None, :]
    return wb2.astype(BF16), mt2.astype(BF16), wc2.astype(BF16), d2, ar2, ai2


def _glu_kernel(y_ref, w_ref, b_ref, o_ref):
    y = y_ref[0].astype(F32)
    g = 0.5 * y * (1.0 + jnp.tanh(math.sqrt(2.0 / math.pi) * (y + 0.044715 * (y * y * y))))
    t = jnp.dot(g.astype(BF16), w_ref[...], preferred_element_type=F32) + b_ref[...]
    o_ref[0] = (g * jax.nn.sigmoid(t)).astype(o_ref.dtype)


def _glu(y3, w, b, ntile):
    bsz, t, n = y3.shape
    return pl.pallas_call(
        _glu_kernel,
        grid=(bsz, ntile),
        in_specs=[pl.BlockSpec((1, TOK, n), lambda i, j: (i, j, 0)),
                  pl.BlockSpec((n, n), lambda i, j: (0, 0)),
                  pl.BlockSpec((1, n), lambda i, j: (0, 0))],
        out_specs=pl.BlockSpec((1, TOK, n), lambda i, j: (i, j, 0)),
        out_shape=jax.ShapeDtypeStruct((bsz, ntile * TOK, n), BF16),
        compiler_params=_cparams(("arbitrary", "arbitrary")),
        name="glu",
    )(y3, w, b.reshape(1, n))


def _merge_kernel(oa_ref, ob_ref, os_ref, ga_ref, gb_ref, gs_ref, wa_ref, wb_ref, wc_ref, o_ref):
    acc = jax.nn.sigmoid(ga_ref[0].astype(F32)) * jnp.dot(oa_ref[0], wa_ref[...], preferred_element_type=F32)
    acc += jax.nn.sigmoid(gb_ref[0].astype(F32)) * jnp.dot(ob_ref[0], wb_ref[...], preferred_element_type=F32)
    acc += jax.nn.sigmoid(gs_ref[0].astype(F32)) * jnp.dot(os_ref[0], wc_ref[...], preferred_element_type=F32)
    o_ref[0] = acc.astype(o_ref.dtype)


def _merge(oa, ob, os_, z3, wa, wb, wc, ntile):
    bsz, _, k = oa.shape
    d = wa.shape[1]
    ospec = pl.BlockSpec((1, TOK, k), lambda i, j: (i, j, 0))
    gspec = lambda br: pl.BlockSpec((1, TOK, d), lambda i, j: (i, j, ZC_GATE // d + br))
    wspec = pl.BlockSpec((k, d), lambda i, j: (0, 0))
    return pl.pallas_call(
        _merge_kernel,
        grid=(bsz, ntile),
        in_specs=[ospec, ospec, ospec, gspec(0), gspec(1), gspec(2), wspec, wspec, wspec],
        out_specs=pl.BlockSpec((1, TOK, d), lambda i, j: (i, j, 0)),
        out_shape=jax.ShapeDtypeStruct((bsz, ntile * TOK, d), BF16),
        compiler_params=_cparams(("arbitrary", "arbitrary")),
        name="merge",
    )(oa, ob, os_, z3, z3, z3, wa, wb, wc)


def _outproj_kernel(x_ref, mix_ref, w_ref, m_ref, o_ref):
    gate = m_ref[0, 0][2:3]
    o_ref[0] = x_ref[0] + gate * jnp.dot(mix_ref[0], w_ref[...], preferred_element_type=F32)


def _outproj(x, mix3, w, modsel, nlt, ntile):
    b, t, d = x.shape
    return pl.pallas_call(
        _outproj_kernel,
        grid=(b, ntile),
        in_specs=[pl.BlockSpec((1, TOK, d), lambda i, j: (i, j, 0)),
                  pl.BlockSpec((1, TOK, d), lambda i, j: (i, j, 0)),
                  pl.BlockSpec((d, d), lambda i, j: (0, 0)),
                  pl.BlockSpec((1, 1, 6, d), lambda i, j: (i, _type_idx(j, nlt), 0, 0))],
        out_specs=pl.BlockSpec((1, TOK, d), lambda i, j: (i, j, 0)),
        out_shape=jax.ShapeDtypeStruct((b, ntile * TOK, d), F32),
        compiler_params=_cparams(("arbitrary", "arbitrary")),
        name="outproj",
    )(x, mix3, w, modsel)


def _router_kernel(x_ref, g_ref, m_ref, w_ref, b_ref, h_ref, meta_ref, tw_ref, cnt_ref, run_ref, *, n_exp):
    first = jnp.logical_and(pl.program_id(0) == 0, pl.program_id(1) == 0)

    @pl.when(first)
    def _():
        run_ref[...] = jnp.zeros_like(run_ref)

    m = m_ref[0, 0]
    h = _rms(x_ref[0]) * g_ref[...] * (1.0 + m[4:5]) + m[3:4]
    h_ref[0] = _pack_bf16_pairs(h)
    logits = jnp.dot(h, w_ref[...], preferred_element_type=F32, precision=lax.Precision.HIGHEST) + b_ref[...]
    lane = lax.broadcasted_iota(jnp.int32, (TOK, LANES), 1).astype(F32)
    lg = jnp.where(lane < n_exp, logits, -jnp.inf)
    vals, idxs = [], []
    for _ in range(TOP_K):
        mx = jnp.max(lg, axis=-1, keepdims=True)
        idx = jnp.min(jnp.where(lg == mx, lane, float(LANES)), axis=-1, keepdims=True)
        vals.append(mx)
        idxs.append(idx)
        lg = jnp.where(lane == idx, -jnp.inf, lg)
    es = [jnp.exp(v - vals[0]) for v in vals]
    den = es[0] + es[1] + es[2] + es[3]
    onehot = jnp.zeros((TOK, LANES), F32)
    for idx in idxs:
        onehot = onehot + jnp.where(lane == idx, 1.0, 0.0)
    row = lax.broadcasted_iota(jnp.int32, (TOK, TOK), 0)
    col = lax.broadcasted_iota(jnp.int32, (TOK, TOK), 1)
    tri = jnp.where(col < row, 1.0, 0.0).astype(BF16)
    before = jnp.dot(tri, onehot.astype(BF16), preferred_element_type=F32) + run_ref[...]
    meta = jnp.zeros((TOK, LANES), F32)
    tw = jnp.zeros((TOK, LANES), F32)
    for k in range(TOP_K):
        rank = jnp.sum(jnp.where(lane == idxs[k], before, 0.0), axis=-1, keepdims=True)
        meta = meta + jnp.where(lane == float(k), idxs[k], 0.0) + jnp.where(lane == float(TOP_K + k), rank, 0.0)
        tw = tw + jnp.where(lane == float(k), es[k] / den, 0.0)
    meta_ref[0] = meta.astype(jnp.int32)
    tw_ref[0] = tw
    run = run_ref[...] + jnp.sum(onehot, axis=0, keepdims=True)
    run_ref[...] = run
    cnt_ref[...] = run


def _router(x, g, modsel, w, bvec, nlt, n_exp):
    b, t, d = x.shape
    tokspec = lambda last: pl.BlockSpec((1, TOK, last), lambda i, j: (i, j, 0))
    return pl.pallas_call(
        functools.partial(_router_kernel, n_exp=n_exp),
        grid=(b, t // TOK),
        in_specs=[tokspec(d),
                  pl.BlockSpec((1, d), lambda i, j: (0, 0)),
                  pl.BlockSpec((1, 1, 6, d), lambda i, j: (i, _type_idx(j, nlt), 0, 0)),
                  pl.BlockSpec((d, LANES), lambda i, j: (0, 0)),
                  pl.BlockSpec((1, LANES), lambda i, j: (0, 0))],
        out_specs=[tokspec(d // 2), tokspec(LANES), tokspec(LANES), pl.BlockSpec((1, LANES), lambda i, j: (0, 0))],
        out_shape=[jax.ShapeDtypeStruct((b, t, d // 2), jnp.uint32),
                   jax.ShapeDtypeStruct((b, t, LANES), jnp.int32),
                   jax.ShapeDtypeStruct((b, t, LANES), F32),
                   jax.ShapeDtypeStruct((1, LANES), F32)],
        scratch_shapes=[pltpu.VMEM((1, LANES), F32)],
        compiler_params=_cparams(("arbitrary", "arbitrary")),
        name="router",
    )(x, g, modsel, w, bvec)


def _dispatch_kernel(pos_ref, h_ref, xs_in_ref, xs_ref, sem):
    del xs_in_ref

    def row_copy(r, k):
        return pltpu.make_async_copy(h_ref.at[pl.ds(r, 1)], xs_ref.at[pl.ds(pos_ref[r * TOP_K + k], 1)], sem)

    def start(r, c):
        for k in range(TOP_K):
            row_copy(r, k).start()
        return c

    def wait(r, c):
        for k in range(TOP_K):
            row_copy(r, k).wait()
        return c

    lax.fori_loop(0, TOK, start, 0)
    lax.fori_loop(0, TOK, wait, 0, unroll=DMA_UNROLL)


def _dispatch(pos, h2, xs0):
    m, d = h2.shape
    return pl.pallas_call(
        _dispatch_kernel,
        grid=(m // TOK,),
        in_specs=[pl.BlockSpec((TOK * TOP_K,), lambda i: (i,), memory_space=pltpu.SMEM),
                  pl.BlockSpec((TOK, d), lambda i: (i, 0)),
                  pl.BlockSpec(memory_space=pl.ANY)],
        out_specs=pl.BlockSpec(memory_space=pl.ANY),
        out_shape=jax.ShapeDtypeStruct(xs0.shape, xs0.dtype),
        scratch_shapes=[pltpu.SemaphoreType.DMA(())],
        input_output_aliases={2: 0},
        compiler_params=_cparams(("arbitrary",)),
        name="dispatch",
    )(pos, h2, xs0)


def _experts_kernel(te_ref, tv_ref, x_ref, wgu_ref, bgu_ref, wd_ref, bd_ref, y_ref, *, d_ff, fchunk):
    i = pl.program_id(0)

    @pl.when(tv_ref[i] == 0)
    def _():
        y_ref[...] = jnp.zeros_like(y_ref)

    @pl.when(tv_ref[i] != 0)
    def _():
        x = jnp.concatenate(_unpack_bf16_pairs(x_ref[...]), axis=1).astype(BF16)
        acc = jnp.zeros(x.shape, F32)
        for c in range(d_ff // fchunk):
            lo = c * fchunk
            g = jnp.dot(x, wgu_ref[0, 0, :, lo:lo + fchunk], preferred_element_type=F32) + bgu_ref[0, :, lo:lo + fchunk]
            lin = (jnp.dot(x, wgu_ref[0, 0, :, d_ff + lo:d_ff + lo + fchunk], preferred_element_type=F32)
                   + bgu_ref[0, :, d_ff + lo:d_ff + lo + fchunk])
            g = jnp.minimum(g, SWIGLU_LIMIT)
            lin = jnp.clip(lin, -SWIGLU_LIMIT, SWIGLU_LIMIT)
            act = g * jax.nn.sigmoid(SWIGLU_ALPHA * g) * (lin + 1.0)
            acc = acc + jnp.dot(act.astype(BF16), wd_ref[0, 0, lo:lo + fchunk, :], preferred_element_type=F32)
        y_ref[...] = _pack_bf16_pairs(acc + bd_ref[0])


def _experts(te, tv, xs, wgu, bgu, wd, bd, layer):
    rows, dh = xs.shape
    _, n_exp, d, ff2 = wgu.shape
    d_ff = ff2 // 2
    grid_spec = pltpu.PrefetchScalarGridSpec(
        num_scalar_prefetch=2,
        grid=(rows // TOK,),
        in_specs=[pl.BlockSpec((TOK, dh), lambda i, te, tv: (i, 0)),
                  pl.BlockSpec((1, 1, d, ff2), lambda i, te, tv: (layer, te[i], 0, 0)),
                  pl.BlockSpec((1, 1, ff2), lambda i, te, tv: (te[i], 0, 0)),
                  pl.BlockSpec((1, 1, d_ff, d), lambda i, te, tv: (layer, te[i], 0, 0)),
                  pl.BlockSpec((1, 1, d), lambda i, te, tv: (te[i], 0, 0))],
        out_specs=pl.BlockSpec((TOK, dh), lambda i, te, tv: (i, 0)),
    )
    return pl.pallas_call(
        functools.partial(_experts_kernel, d_ff=d_ff, fchunk=512),
        grid_spec=grid_spec,
        out_shape=jax.ShapeDtypeStruct((rows, dh), jnp.uint32),
        compiler_params=_cparams(("arbitrary",)),
        name="experts",
    )(te, tv, xs, wgu, bgu.reshape(n_exp, 1, ff2), wd, bd.reshape(n_exp, 1, d))


def _combine_kernel(pos_ref, x_ref, tw_ref, m_ref, ys_ref, o_ref, buf_ref, sem):
    def row_copy(r, k):
        return pltpu.make_async_copy(ys_ref.at[pl.ds(pos_ref[r * TOP_K + k], 1)], buf_ref.at[k, pl.ds(r, 1)], sem)

    def start(r, c):
        for k in range(TOP_K):
            row_copy(r, k).start()
        return c

    def wait(r, c):
        for k in range(TOP_K):
            row_copy(r, k).wait()
        return c

    lax.fori_loop(0, TOK, start, 0, unroll=DMA_UNROLL)
    lax.fori_loop(0, TOK, wait, 0, unroll=DMA_UNROLL)
    tw = tw_ref[0]
    acc_lo = acc_hi = None
    for k in range(TOP_K):
        lo, hi = _unpack_bf16_pairs(buf_ref[k])
        wk = tw[:, k:k + 1]
        acc_lo = wk * lo if acc_lo is None else acc_lo + wk * lo
        acc_hi = wk * hi if acc_hi is None else acc_hi + wk * hi
    o_ref[0] = x_ref[0] + m_ref[0, 0][5:6] * jnp.concatenate([acc_lo, acc_hi], axis=1)


def _combine(pos, x, tw, modsel, ys, nlt):
    b, t, d = x.shape
    nt = t // TOK
    return pl.pallas_call(
        _combine_kernel,
        grid=(b, nt),
        in_specs=[pl.BlockSpec((TOK * TOP_K,), lambda i, j: (i * nt + j,), memory_space=pltpu.SMEM),
                  pl.BlockSpec((1, TOK, d), lambda i, j: (i, j, 0)),
                  pl.BlockSpec((1, TOK, LANES), lambda i, j: (i, j, 0)),
                  pl.BlockSpec((1, 1, 6, d), lambda i, j: (i, _type_idx(j, nlt), 0, 0)),
                  pl.BlockSpec(memory_space=pl.ANY)],
        out_specs=pl.BlockSpec((1, TOK, d), lambda i, j: (i, j, 0)),
        out_shape=jax.ShapeDtypeStruct((b, t, d), F32),
        scratch_shapes=[pltpu.VMEM((TOP_K, TOK, d // 2), jnp.uint32), pltpu.SemaphoreType.DMA(())],
        compiler_params=_cparams(("arbitrary", "arbitrary")),
        name="combine",
    )(pos, x, tw, modsel, ys)


def _moe(xa, norm_g, modsel, w_router, b_router, wgu, bgu, wd, bd, nlt, layer):
    b, t, d = xa.shape
    m = b * t
    n_exp = w_router.shape[1]
    wr = jnp.zeros((d, LANES), F32).at[:, :n_exp].set(w_router.astype(F32))
    br = jnp.zeros((1, LANES), F32).at[0, :n_exp].set(b_router.astype(F32))
    h2, meta, tw, cnt = _router(xa, norm_g.reshape(1, d), modsel, wr, br, nlt, n_exp)

    counts = cnt[0, :n_exp].astype(jnp.int32)
    padded = ((counts + TOK - 1) // TOK) * TOK
    gend = jnp.cumsum(padded)
    gstart = gend - padded
    meta2 = meta.reshape(m, LANES)
    pos = (gstart[meta2[:, 0:TOP_K]] + meta2[:, TOP_K:2 * TOP_K]).reshape(m * TOP_K)
    ntiles = (m * TOP_K) // TOK + n_exp
    tile_start = jnp.arange(ntiles, dtype=jnp.int32) * TOK
    te = jnp.sum((tile_start[:, None] >= gend[None, :]).astype(jnp.int32), axis=1)
    te = jnp.minimum(te, n_exp - 1)
    tv = (tile_start < gend[-1]).astype(jnp.int32)

    xs = _dispatch(pos, h2.reshape(m, d // 2), jnp.zeros((ntiles * TOK, d // 2), jnp.uint32))
    ys = _experts(te, tv, xs, wgu, bgu, wd, bd, layer)
    return _combine(pos, xa, tw, modsel, ys, nlt)


def _rope_tables(seq, ctx):
    t = jnp.arange(seq, dtype=jnp.int32)
    row = (t // GRID_W).astype(F32)
    col = (t % GRID_W).astype(F32)
    n_freq = MLA_ROPE // 4
    inv_freq = ROPE_BASE ** (-jnp.arange(n_freq, dtype=F32) / n_freq)
    ang = jnp.concatenate([row[:, None] * inv_freq, col[:, None] * inv_freq], axis=-1)
    cos, sin = jnp.cos(ang), jnp.sin(ang)
    cos2 = jnp.tile(jnp.concatenate([cos, cos], axis=-1), (1, 2))
    sin2 = jnp.tile(jnp.concatenate([-sin, sin], axis=-1), (1, 2))
    cos2 = jnp.concatenate([cos2, jnp.ones((ctx, LANES), F32)], axis=0)
    sin2 = jnp.concatenate([sin2, jnp.zeros((ctx, LANES), F32)], axis=0)
    return cos2, sin2


def _pad_cols(w, width):
    return jnp.pad(w, ((0, 0), (0, width - w.shape[1])))


def _layout_w_in(w):
    o_ckv = MLA_Q_RANK
    o_kpe = o_ckv + MLA_KV_RANK
    o_naq = o_kpe + MLA_ROPE
    o_gate = o_naq + 3 * NA_WIDTH + SSM_WIDTH
    parts = [w[:, o_gate:],
             w[:, :o_ckv],
             _pad_cols(w[:, o_kpe:o_naq], ZC_CKV - ZC_KPE),
             w[:, o_ckv:o_kpe],
             w[:, o_naq:o_gate]]
    out = jnp.concatenate(parts, axis=1)
    assert out.shape[1] == ZW
    return out.astype(BF16)


def kernel(x, c, ctx, c_ctx, w_ada, b_ada, norm1_g, w_in, mla_q_norm, mla_kv_norm, mla_w_uq, mla_w_ukv, mla_q_gain, mla_k_gain, na_q_gain, na_k_gain, na_rpb, ssm_lam_re, ssm_lam_im, ssm_log_dt, ssm_b_re, ssm_b_im, ssm_c_re, ssm_c_im, ssm_d, ssm_w_glu, ssm_b_glu, w_pa, w_pb, w_pc, w_out, norm2_g, w_router, b_router, w_gate_up, b_gate_up, w_down, b_down):
    bsz, seq, d = x.shape
    nctx = ctx.shape[1]
    depth = w_ada.shape[0]
    assert nctx == TOK and seq % (NA_QROWS * GRID_W) == 0 and seq // GRID_W >= NA_KROWS
    nlt = seq // TOK
    t = seq + nctx
    m = bsz * t
    rows = seq // GRID_W
    nblk = rows // NA_QROWS
    nchunk = t // SSM_CHUNK

    cos2, sin2 = _rope_tables(seq, nctx)
    cvec = jnp.zeros((16, d), F32).at[:bsz].set(c.astype(F32)).at[bsz].set(c_ctx.astype(F32))
    xa = jnp.concatenate([x, ctx], axis=1).astype(F32)
    row128 = lambda v: v.astype(F32).reshape(1, LANES)
    perm = jnp.asarray(_ssm_perm(), BF16)
    wgu_all = w_gate_up.astype(BF16)
    wd_all = w_down.astype(BF16)

    for l in range(depth):
        need_ctx = l < depth - 1
        ntile = t // TOK if need_ctx else nlt
        mod = _ada(cvec, w_ada[l], b_ada[l])
        mod_lat = mod[:bsz].reshape(bsz, 1, 6, d)
        mod_ctx = jnp.broadcast_to(mod[bsz].reshape(1, 1, 6, d), (bsz, 1, 6, d))
        modsel = jnp.concatenate([mod_lat, mod_ctx], axis=1)

        h1 = _norm_mod(xa, norm1_g[l], modsel, nlt)
        z = _matmul(h1.reshape(m, d), _layout_w_in(w_in[l]), BF16, "in_proj", ZW // 4)
        z3 = z.reshape(bsz, t, ZW)

        wq = mla_w_uq[l].reshape(MLA_Q_RANK, MLA_HEADS, MLA_QK)
        wq = jnp.concatenate([wq[:, :, :MLA_NOPE].reshape(MLA_Q_RANK, -1),
                              wq[:, :, MLA_NOPE:].reshape(MLA_Q_RANK, -1)], axis=1).astype(BF16)
        qg_r = jnp.tile(mla_q_gain[l][MLA_NOPE:], 2)
        kg_r = jnp.concatenate([mla_k_gain[l][MLA_NOPE:], jnp.zeros((MLA_ROPE,), mla_k_gain.dtype)])
        qf = _mla_q(z3, mla_q_norm[l].astype(F32).reshape(1, -1), wq, row128(mla_q_gain[l][:MLA_NOPE]),
                    row128(qg_r), cos2, sin2, ntile)
        kf, vf = _mla_kv(z3, mla_kv_norm[l].astype(F32).reshape(1, -1), mla_w_ukv[l].astype(BF16),
                         row128(mla_k_gain[l][:MLA_NOPE]), row128(kg_r), cos2, sin2)
        o_a = _mla_attn(qf, kf, vf, seq, need_ctx)

        bias = _na_bias_tables(na_rpb[l], rows)
        qn, kn = _na_norm(z3, row128(jnp.tile(na_q_gain[l], 2)), row128(jnp.tile(na_k_gain[l], 2)))
        o_b = _na_attn(qn, kn, z3, bias, nblk, need_ctx)

        wb2, mt2, wc2, d2, ar2, ai2 = _ssm_params(ssm_lam_re[l], ssm_lam_im[l], ssm_log_dt[l], ssm_b_re[l],
                                                   ssm_b_im[l], ssm_c_re[l], ssm_c_im[l], ssm_d[l])
        r = nchunk * bsz
        u = z3[:, :, ZC_SSM:ZC_SSM + SSM_WIDTH].reshape(bsz, nchunk, SSM_CHUNK, SSM_NSUPER, LANES)
        x8 = jnp.transpose(u, (3, 1, 0, 2, 4)).reshape(SSM_NSUPER * r, SSM_CW)
        xp = _matmul(x8, perm, BF16, "ssm_perm_in", SSM_CW)
        yp = _ssm(xp.reshape(SSM_NSUPER, r, SSM_CW), wb2, mt2, wc2, d2, ar2, ai2, bsz, nchunk, nctx // SSM_CHUNK)
        y8 = _matmul(yp.reshape(SSM_NSUPER * r, SSM_CW), perm.T, BF16, "ssm_perm_out", SSM_CW)
        y = jnp.transpose(y8.reshape(SSM_NSUPER, nchunk, bsz, SSM_CHUNK, LANES), (2, 1, 3, 0, 4))
        o_s = _glu(y.reshape(bsz, t, SSM_WIDTH), ssm_w_glu[l].astype(BF16), ssm_b_glu[l].astype(F32), ntile)

        mix = _merge(o_a, o_b, o_s, z3, w_pa[l].astype(BF16), w_pb[l].astype(BF16), w_pc[l].astype(BF16), ntile)
        xa = _outproj(xa, mix, w_out[l].astype(BF16), modsel, nlt, ntile)
        xa = _moe(xa, norm2_g[l].astype(F32), modsel, w_router[l], b_router[l], wgu_all, b_gate_up[l].astype(F32),
                  wd_all, b_down[l].astype(F32), nlt, l)
    return xa[:, :seq].astype(x.dtype)
```

```python
import functools
import math

import numpy as np
import jax
import jax.numpy as jnp
from jax import lax
from jax.experimental import pallas as pl
from jax.experimental.pallas import tpu as pltpu

F32 = jnp.float32
BF16 = jnp.bfloat16

GRID_W = 64
MLA_HEADS = 8
MLA_NOPE = 128
MLA_ROPE = 64
MLA_V = 128
MLA_QK = MLA_NOPE + MLA_ROPE
MLA_Q_RANK = 768
MLA_KV_RANK = 512
NA_HEADS = 16
NA_HEAD_DIM = 64
NA_WIDTH = NA_HEADS * NA_HEAD_DIM
NA_WIN_R = 8
NA_WIN_C = 16
SSM_WIDTH = 1024
SSM_GROUP = 16
SSM_GROUPS = SSM_WIDTH // SSM_GROUP
SSM_STATE = 64
TOP_K = 4
SWIGLU_LIMIT = 7.0
SWIGLU_ALPHA = 1.702
ROPE_BASE = 10000.0
EPS = 1e-6
MLA_SCALE = MLA_QK ** -0.5
NA_SCALE = NA_HEAD_DIM ** -0.5
NEG_BIG = -1e30

LANES = 128
TOK = 256
MLA_KPAD = 256
MLA_VPAD = 256
MLA_TQ = 1024
MLA_TK = 512
VMEM_LIMIT = 56 * 1024 * 1024
DMA_UNROLL = 4

ZC_GATE = 0
ZC_CQ = 6144
ZC_KPE = 6912
ZC_CKV = 7168
ZC_NAQ = 7680
ZC_NAK = 8704
ZC_NAV = 9728
ZC_SSM = 10752
ZW = 11776

SSM_CHUNK = 16
SSM_NSUPER = SSM_WIDTH // LANES
SSM_CW = SSM_CHUNK * LANES
SSM_PW = 2 * SSM_CHUNK * SSM_GROUP
NA_QROWS = 4
NA_KROWS = 12
NA_BB = 2


def _cparams(sem, vmem=VMEM_LIMIT):
    return pltpu.CompilerParams(dimension_semantics=sem, vmem_limit_bytes=vmem)


def _rms(x, eps=EPS):
    return x * lax.rsqrt(jnp.mean(x * x, axis=-1, keepdims=True) + eps)


def _split_bf16(x):
    hi = x.astype(BF16)
    return hi, (x - hi.astype(F32)).astype(BF16)


def _seg64_rms(t):
    r = lax.broadcasted_iota(jnp.int32, (LANES, LANES), 0)
    c = lax.broadcasted_iota(jnp.int32, (LANES, LANES), 1)
    same_half = jnp.where((r < 64) == (c < 64), 1.0, 0.0).astype(BF16)
    hi, lo = _split_bf16(t * t)
    ssum = (jnp.dot(hi, same_half, preferred_element_type=F32) + jnp.dot(lo, same_half, preferred_element_type=F32))
    return t * lax.rsqrt(ssum * (1.0 / 64.0) + EPS)


def _rope64(t, cos2, sin2):
    lane = lax.broadcasted_iota(jnp.int32, t.shape, 1)
    first = (lane % 64) < 32
    swapped = jnp.where(first, pltpu.roll(t, 96, axis=1), pltpu.roll(t, 32, axis=1))
    return t * cos2 + swapped * sin2


def _ones_col_tile(rows, dtype):
    lane = lax.broadcasted_iota(jnp.int32, (rows, LANES), 1)
    return jnp.where(lane == 0, 1.0, 0.0).astype(dtype)


def _qkt(q, k):
    return lax.dot_general(q, k, (((1,), (1,)), ((), ())), preferred_element_type=F32)


def _softmax_acc(s, v_ext):
    m = jnp.max(s, axis=-1, keepdims=True)
    return jnp.dot(jnp.exp((s - m).astype(BF16)), v_ext, preferred_element_type=F32)


def _softmax_pv(s, v_ext):
    oe = _softmax_acc(s, v_ext)
    return oe[:, 0:LANES] / oe[:, LANES:LANES + 1]


def _pack_bf16_pairs(x):
    n = x.shape[1] // 2
    r = x.astype(BF16).astype(F32)
    lo = lax.bitcast_convert_type(r[:, :n], jnp.uint32) >> 16
    hi = lax.bitcast_convert_type(r[:, n:], jnp.uint32) & jnp.uint32(0xFFFF0000)
    return hi | lo


def _unpack_bf16_pairs(w):
    lo = lax.bitcast_convert_type(w << 16, F32)
    hi = lax.bitcast_convert_type(w & jnp.uint32(0xFFFF0000), F32)
    return lo, hi


def _type_idx(j, nlt):
    return jnp.where(j >= nlt, 1, 0)


def _ada_kernel(c_ref, w_ref, b_ref, o_ref):
    c = c_ref[...]
    a = (c * jax.nn.sigmoid(c)).astype(BF16)
    o_ref[...] = jnp.dot(a, w_ref[...].astype(BF16), preferred_element_type=F32) + b_ref[...]


def _ada(cvec, w, b):
    rows, d = cvec.shape
    n = w.shape[1]
    tn = 1024
    return pl.pallas_call(
        _ada_kernel,
        grid=(n // tn,),
        in_specs=[pl.BlockSpec((rows, d), lambda j: (0, 0)),
                  pl.BlockSpec((d, tn), lambda j: (0, j)),
                  pl.BlockSpec((1, tn), lambda j: (0, j))],
        out_specs=pl.BlockSpec((rows, tn), lambda j: (0, j)),
        out_shape=jax.ShapeDtypeStruct((rows, n), F32),
        compiler_params=_cparams(("arbitrary",)),
        name="ada",
    )(cvec, w, b.reshape(1, n))


def _norm_mod_kernel(x_ref, g_ref, m_ref, o_ref):
    m = m_ref[0, 0]
    y = _rms(x_ref[0]) * g_ref[...]
    o_ref[0] = (y * (1.0 + m[1:2]) + m[0:1]).astype(o_ref.dtype)


def _norm_mod(x, g, modsel, nlt):
    b, t, d = x.shape
    return pl.pallas_call(
        _norm_mod_kernel,
        grid=(b, t // TOK),
        in_specs=[pl.BlockSpec((1, TOK, d), lambda i, j: (i, j, 0)),
                  pl.BlockSpec((1, d), lambda i, j: (0, 0)),
                  pl.BlockSpec((1, 1, 6, d), lambda i, j: (i, _type_idx(j, nlt), 0, 0))],
        out_specs=pl.BlockSpec((1, TOK, d), lambda i, j: (i, j, 0)),
        out_shape=jax.ShapeDtypeStruct((b, t, d), BF16),
        compiler_params=_cparams(("arbitrary", "arbitrary")),
        name="norm_mod",
    )(x, g.reshape(1, d), modsel)


def _mm_kernel(a_ref, w_ref, o_ref):
    o_ref[...] = jnp.dot(a_ref[...], w_ref[...], preferred_element_type=F32).astype(o_ref.dtype)


def _pick(n, cands):
    for c in cands:
        if n % c == 0:
            return c
    return n


def _matmul(a, w, out_dtype, name, tn):
    m, k = a.shape
    n = w.shape[1]
    tm = _pick(m, (512, 256, 128))
    return pl.pallas_call(
        _mm_kernel,
        grid=(n // tn, m // tm),
        in_specs=[pl.BlockSpec((tm, k), lambda j, i: (i, 0)),
                  pl.BlockSpec((k, tn), lambda j, i: (0, j))],
        out_specs=pl.BlockSpec((tm, tn), lambda j, i: (i, j)),
        out_shape=jax.ShapeDtypeStruct((m, n), out_dtype),
        compiler_params=_cparams(("arbitrary", "arbitrary")),
        name=name,
    )(a, w)


def _mla_q_kernel(z_ref, g_ref, w_ref, gn_ref, gr_ref, cos_ref, sin_ref, o_ref):
    cq = z_ref[0].astype(F32)
    xn = (_rms(cq) * g_ref[...]).astype(BF16)
    q = jnp.dot(xn, w_ref[...], preferred_element_type=F32)
    lane = lax.broadcasted_iota(jnp.int32, (TOK, LANES), 1)
    left = lane < 64
    for h in range(MLA_HEADS):
        qn = q[:, h * MLA_NOPE:(h + 1) * MLA_NOPE]
        o_ref[0, h, :, 0:LANES] = (_rms(qn) * (gn_ref[...] * MLA_SCALE)).astype(o_ref.dtype)
    base = MLA_HEADS * MLA_NOPE
    for hp in range(MLA_HEADS // 2):
        t = q[:, base + hp * LANES: base + (hp + 1) * LANES]
        t = _seg64_rms(t) * (gr_ref[...] * MLA_SCALE)
        r = _rope64(t, cos_ref[...], sin_ref[...])
        o_ref[0, 2 * hp, :, LANES:2 * LANES] = jnp.where(left, r, 0.0).astype(o_ref.dtype)
        o_ref[0, 2 * hp + 1, :, LANES:2 * LANES] = jnp.where(left, pltpu.roll(r, 64, axis=1), 0.0).astype(o_ref.dtype)


def _mla_q(z3, g, w, gn, gr2, cos2, sin2, ntile):
    b, t, _ = z3.shape
    return pl.pallas_call(
        _mla_q_kernel,
        grid=(b, ntile),
        in_specs=[pl.BlockSpec((1, TOK, MLA_Q_RANK), lambda i, j: (i, j, ZC_CQ // MLA_Q_RANK)),
                  pl.BlockSpec((1, MLA_Q_RANK), lambda i, j: (0, 0)),
                  pl.BlockSpec(w.shape, lambda i, j: (0, 0)),
                  pl.BlockSpec((1, LANES), lambda i, j: (0, 0)),
                  pl.BlockSpec((1, LANES), lambda i, j: (0, 0)),
                  pl.BlockSpec((TOK, LANES), lambda i, j: (j, 0)),
                  pl.BlockSpec((TOK, LANES), lambda i, j: (j, 0))],
        out_specs=pl.BlockSpec((1, MLA_HEADS, TOK, MLA_KPAD), lambda i, j: (i, 0, j, 0)),
        out_shape=jax.ShapeDtypeStruct((b, MLA_HEADS, ntile * TOK, MLA_KPAD), BF16),
        compiler_params=_cparams(("arbitrary", "arbitrary")),
        name="mla_q",
    )(z3, g, w, gn, gr2, cos2, sin2)


def _mla_kv_kernel(z_ref, zk_ref, g_ref, w_ref, gn_ref, gr_ref, cos_ref, sin_ref, k_ref, v_ref):
    ckv = z_ref[0].astype(F32)
    xn = (_rms(ckv) * g_ref[...]).astype(BF16)
    kv = jnp.dot(xn, w_ref[...], preferred_element_type=F32)
    lane = lax.broadcasted_iota(jnp.int32, (TOK, LANES), 1)
    left = lane < 64
    t = jnp.where(left, zk_ref[0].astype(F32), 0.0)
    ms = jnp.sum(t * t, axis=-1, keepdims=True) * (1.0 / MLA_ROPE)
    kr = t * lax.rsqrt(ms + EPS) * gr_ref[...]
    kr = jnp.where(left, _rope64(kr, cos_ref[...], sin_ref[...]), 0.0).astype(k_ref.dtype)
    ones = _ones_col_tile(TOK, v_ref.dtype)
    hw = MLA_NOPE + MLA_V
    for h in range(MLA_HEADS):
        kn = kv[:, h * hw: h * hw + MLA_NOPE]
        k_ref[0, h, :, 0:LANES] = (_rms(kn) * gn_ref[...]).astype(k_ref.dtype)
        k_ref[0, h, :, LANES:2 * LANES] = kr
        v_ref[0, h, :, 0:LANES] = kv[:, h * hw + MLA_NOPE:(h + 1) * hw].astype(v_ref.dtype)
        v_ref[0, h, :, LANES:2 * LANES] = ones


def _mla_kv(z3, g, w, gn, gr2, cos2, sin2):
    b, t, _ = z3.shape
    return pl.pallas_call(
        _mla_kv_kernel,
        grid=(b, t // TOK),
        in_specs=[pl.BlockSpec((1, TOK, MLA_KV_RANK), lambda i, j: (i, j, ZC_CKV // MLA_KV_RANK)),
                  pl.BlockSpec((1, TOK, LANES), lambda i, j: (i, j, ZC_KPE // LANES)),
                  pl.BlockSpec((1, MLA_KV_RANK), lambda i, j: (0, 0)),
                  pl.BlockSpec(w.shape, lambda i, j: (0, 0)),
                  pl.BlockSpec((1, LANES), lambda i, j: (0, 0)),
                  pl.BlockSpec((1, LANES), lambda i, j: (0, 0)),
                  pl.BlockSpec((TOK, LANES), lambda i, j: (j, 0)),
                  pl.BlockSpec((TOK, LANES), lambda i, j: (j, 0))],
        out_specs=[pl.BlockSpec((1, MLA_HEADS, TOK, MLA_KPAD), lambda i, j: (i, 0, j, 0)),
                   pl.BlockSpec((1, MLA_HEADS, TOK, MLA_VPAD), lambda i, j: (i, 0, j, 0))],
        out_shape=[jax.ShapeDtypeStruct((b, MLA_HEADS, t, MLA_KPAD), BF16),
                   jax.ShapeDtypeStruct((b, MLA_HEADS, t, MLA_VPAD), BF16)],
        compiler_params=_cparams(("arbitrary", "arbitrary")),
        name="mla_kv",
    )(z3, z3, g, w, gn, gr2, cos2, sin2)


def _mla_attn_kernel(q_ref, k_ref, v_ref, o_ref, *, nk, tk):
    q = q_ref[0, 0]
    m = acc = None
    for c0 in range(0, nk, tk):
        sz = min(tk, nk - c0)
        s = _qkt(q, k_ref[0, 0, c0:c0 + sz, :])
        cm = jnp.max(s, axis=-1, keepdims=True)
        m_new = cm if m is None else jnp.maximum(m, cm)
        pv = jnp.dot(jnp.exp((s - m_new).astype(BF16)), v_ref[0, 0, c0:c0 + sz, :], preferred_element_type=F32)
        acc = pv if m is None else acc * jnp.exp(m - m_new) + pv
        m = m_new
    o_ref[0] = (acc[:, 0:LANES] / acc[:, LANES:LANES + 1]).astype(o_ref.dtype)


def _mla_attn_ctx_kernel(q_ref, k_ref, v_ref, o_ref):
    o_ref[0] = _softmax_pv(_qkt(q_ref[0, 0], k_ref[0, 0]), v_ref[0, 0]).astype(o_ref.dtype)


def _mla_attn(q, k, v, seq, need_ctx):
    b, h, t, _ = k.shape
    tq = _pick(seq, (MLA_TQ, TOK))
    sem = ("arbitrary", "arbitrary", "arbitrary")
    o = pl.pallas_call(
        functools.partial(_mla_attn_kernel, nk=t, tk=MLA_TK),
        grid=(b, h, seq // tq),
        in_specs=[pl.BlockSpec((1, 1, tq, MLA_KPAD), lambda i, hh, j: (i, hh, j, 0)),
                  pl.BlockSpec((1, 1, t, MLA_KPAD), lambda i, hh, j: (i, hh, 0, 0)),
                  pl.BlockSpec((1, 1, t, MLA_VPAD), lambda i, hh, j: (i, hh, 0, 0))],
        out_specs=pl.BlockSpec((1, tq, MLA_V), lambda i, hh, j: (i, j, hh)),
        out_shape=jax.ShapeDtypeStruct((b, seq, h * MLA_V), BF16),
        compiler_params=_cparams(sem),
        name="mla_attn",
    )(q, k, v)
    if not need_ctx:
        return o
    nctx = t - seq
    ctx_spec = lambda w: pl.BlockSpec((1, 1, nctx, w), lambda i, hh, j: (i, hh, seq // nctx, 0))
    o_c = pl.pallas_call(
        _mla_attn_ctx_kernel,
        grid=(b, h, 1),
        in_specs=[ctx_spec(MLA_KPAD), ctx_spec(MLA_KPAD), ctx_spec(MLA_VPAD)],
        out_specs=pl.BlockSpec((1, nctx, MLA_V), lambda i, hh, j: (i, 0, hh)),
        out_shape=jax.ShapeDtypeStruct((b, nctx, h * MLA_V), BF16),
        compiler_params=_cparams(sem),
        name="mla_attn_ctx",
    )(q, k, v)
    return jnp.concatenate([o, o_c], axis=1)


def _na_norm_kernel(q_ref, k_ref, gq_ref, gk_ref, qn_ref, kn_ref):
    for c in range(q_ref.shape[2] // LANES):
        sl = slice(c * LANES, (c + 1) * LANES)
        qn_ref[0, :, sl] = (_seg64_rms(q_ref[0, :, sl].astype(F32)) * (gq_ref[...] * NA_SCALE)).astype(qn_ref.dtype)
        kn_ref[0, :, sl] = (_seg64_rms(k_ref[0, :, sl].astype(F32)) * gk_ref[...]).astype(kn_ref.dtype)


def _na_norm(z3, gq2, gk2):
    b, t, _ = z3.shape
    w = 512
    zspec = lambda col0: pl.BlockSpec((1, TOK, w), lambda i, j, c: (i, j, col0 // w + c))
    ospec = pl.BlockSpec((1, TOK, w), lambda i, j, c: (i, j, c))
    gspec = pl.BlockSpec((1, LANES), lambda i, j, c: (0, 0))
    oshape = jax.ShapeDtypeStruct((b, t, NA_WIDTH), BF16)
    return pl.pallas_call(
        _na_norm_kernel,
        grid=(b, t // TOK, NA_WIDTH // w),
        in_specs=[zspec(ZC_NAQ), zspec(ZC_NAK), gspec, gspec],
        out_specs=[ospec, ospec],
        out_shape=[oshape, oshape],
        compiler_params=_cparams(("arbitrary", "arbitrary", "arbitrary")),
        name="na_norm",
    )(z3, z3, gq2, gk2)


def _na_kernel(q_ref, k0_ref, k1_ref, k2_ref, kc_ref, v0_ref, v1_ref, v2_ref, vc_ref, bias_ref, o_ref, *, nblk):
    j = pl.program_id(1)
    lane = lax.broadcasted_iota(jnp.int32, (TOK, LANES), 1)
    left = lane < 64
    ones = _ones_col_tile(TOK, BF16)
    nb = q_ref.shape[0]

    def stacked_q(bi):
        qn = q_ref[bi]
        zero = jnp.zeros_like(qn)
        return jnp.concatenate([jnp.where(left, qn, zero), jnp.where(left, zero, qn)], axis=0)

    def finish(bi, acc):
        out = acc[:, 0:LANES] / acc[:, LANES:LANES + 1]
        o_ref[bi] = jnp.where(left, out[0:TOK], out[TOK:2 * TOK]).astype(o_ref.dtype)

    @pl.when(j >= nblk)
    def _():
        for bi in range(nb):
            v_ext = jnp.concatenate([vc_ref[bi], ones], axis=1)
            finish(bi, _softmax_acc(_qkt(stacked_q(bi), kc_ref[bi]), v_ext))

    @pl.when(j < nblk)
    def _():
        q2 = [stacked_q(bi) for bi in range(nb)]
        m = [None] * nb
        acc = [None] * nb
        for c, (k_ref, v_ref) in enumerate(((k0_ref, v0_ref), (k1_ref, v1_ref), (k2_ref, v2_ref), (kc_ref, vc_ref))):
            bias = bias_ref[0, :, :, c * TOK:(c + 1) * TOK].reshape(2 * TOK, TOK)
            for bi in range(nb):
                v_ext = jnp.concatenate([v_ref[bi], ones], axis=1)
                s = _qkt(q2[bi], k_ref[bi]) + bias
                cm = jnp.max(s, axis=-1, keepdims=True)
                m_new = cm if m[bi] is None else jnp.maximum(m[bi], cm)
                pv = jnp.dot(jnp.exp((s - m_new).astype(BF16)), v_ext, preferred_element_type=F32)
                acc[bi] = pv if m[bi] is None else acc[bi] * jnp.exp(m[bi] - m_new) + pv
                m[bi] = m_new
        for bi in range(nb):
            finish(bi, acc[bi])


def _na_attn(qn, kn, z3, bias, nblk, need_ctx):
    b = z3.shape[0]
    bb = _pick(b, (NA_BB, 1))
    kb = lambda j: jnp.clip(j - 1, 0, nblk - 3)
    nj = nblk + (1 if need_ctx else 0)

    def tspec(col_blk0, tok_idx):
        return pl.BlockSpec((bb, TOK, LANES), lambda hp, j, i: (i, tok_idx(j), col_blk0 + hp))

    var = lambda j: jnp.where(j == 0, 0, jnp.where(j == nblk - 1, 2, 1))
    window = lambda c0: [tspec(c0, (lambda j, d=d: kb(j) + d)) for d in range(3)] + [tspec(c0, lambda j: nblk)]
    in_specs = [tspec(0, lambda j: j)] + window(0) + window(ZC_NAV // LANES)
    in_specs += [pl.BlockSpec((1, 2, TOK, 4 * TOK), lambda hp, j, i: (var(j), hp, 0, 0))]
    return pl.pallas_call(
        functools.partial(_na_kernel, nblk=nblk),
        grid=(NA_HEADS // 2, nj, b // bb),
        in_specs=in_specs,
        out_specs=pl.BlockSpec((bb, TOK, LANES), lambda hp, j, i: (i, j, hp)),
        out_shape=jax.ShapeDtypeStruct((b, nj * TOK, NA_WIDTH), BF16),
        compiler_params=_cparams(("arbitrary", "arbitrary", "arbitrary")),
        name="na_attn",
    )(qn, kn, kn, kn, kn, z3, z3, z3, z3, bias)


def _na_bias_tables(rpb, rows):
    nblk = rows // NA_QROWS
    nh = rpb.shape[0]
    ncol = 2 * NA_WIN_C - 1
    dd = np.arange(ncol)[:, None, None]
    qc1 = np.arange(GRID_W)[None, :, None]
    kc1 = np.arange(GRID_W)[None, None, :]
    onehot = (kc1 - qc1 + NA_WIN_C - 1 == dd).astype(np.float32)
    toep = jnp.einsum('hrd,dqk->hrqk', rpb.astype(F32), onehot, precision=lax.Precision.HIGHEST)
    c0 = np.clip(np.arange(GRID_W) - NA_WIN_C // 2, 0, GRID_W - NA_WIN_C)
    kcs = np.arange(GRID_W)
    valid_c = (kcs[None, :] >= c0[:, None]) & (kcs[None, :] < c0[:, None] + NA_WIN_C)
    tabs = []
    for j in (0, 1, nblk - 1):
        start = int(np.clip(NA_QROWS * j - NA_QROWS, 0, rows - NA_KROWS))
        qr = NA_QROWS * j + np.arange(NA_QROWS)
        kr = start + np.arange(NA_KROWS)
        r0 = np.clip(qr - NA_WIN_R // 2, 0, rows - NA_WIN_R)
        valid_r = (kr[None, :] >= r0[:, None]) & (kr[None, :] < r0[:, None] + NA_WIN_R)
        drow = np.clip(kr[None, :] - qr[:, None] + NA_WIN_R - 1, 0, 2 * NA_WIN_R - 2).reshape(-1)
        sel = jnp.take(toep, drow, axis=1).reshape(nh, NA_QROWS, NA_KROWS, GRID_W, GRID_W)
        sel = jnp.transpose(sel, (0, 1, 3, 2, 4)).reshape(nh, TOK, NA_KROWS * GRID_W)
        valid = (valid_r[:, None, :, None] & valid_c[None, :, None, :]).reshape(TOK, NA_KROWS * GRID_W)
        loc = jnp.where(valid[None], sel, NEG_BIG)
        tabs.append(jnp.concatenate([loc, jnp.zeros((nh, TOK, TOK), F32)], axis=-1))
    return jnp.stack(tabs)


def _ssm_kernel(u_ref, wb_ref, mt_ref, wc_ref, d_ref, ar_ref, ai_ref, y_ref, sh_ref, *, bsz, nchunk, ncc):
    u = u_ref[0]
    sh_ref[...] = jnp.dot(u, wb_ref[0], preferred_element_type=F32)
    a_r = ar_ref[0]
    a_i = ai_ref[0]
    arf, aif, arr, air = a_r[:, 0:LANES], a_i[:, 0:LANES], a_r[:, LANES:], a_i[:, LANES:]
    nlc = nchunk - ncc

    def body(i, carry):
        hre_f, him_f, hre_r, him_r = carry
        cf = jnp.where(i < ncc, nlc + i, i - ncc)
        cr = nchunk - 1 - i
        rf = pl.multiple_of(cf * bsz, bsz)
        rr = pl.multiple_of(cr * bsz, bsz)
        sre_f = sh_ref[pl.ds(rf, bsz), 0:LANES]
        sim_f = sh_ref[pl.ds(rf, bsz), LANES:2 * LANES]
        sre_r = sh_ref[pl.ds(rr, bsz), 2 * LANES:3 * LANES]
        sim_r = sh_ref[pl.ds(rr, bsz), 3 * LANES:4 * LANES]
        sh_ref[pl.ds(rf, bsz), 0:LANES] = hre_f
        sh_ref[pl.ds(rf, bsz), LANES:2 * LANES] = him_f
        sh_ref[pl.ds(rr, bsz), 2 * LANES:3 * LANES] = hre_r
        sh_ref[pl.ds(rr, bsz), 3 * LANES:4 * LANES] = him_r
        return (arf * hre_f - aif * him_f + sre_f, arf * him_f + aif * hre_f + sim_f,
                arr * hre_r - air * him_r + sre_r, arr * him_r + air * hre_r + sim_r)

    zero = jnp.zeros((bsz, LANES), F32)
    lax.fori_loop(0, nchunk, body, (zero, zero, zero, zero))
    y = jnp.dot(u, mt_ref[0], preferred_element_type=F32)
    y = y + jnp.dot(sh_ref[...].astype(BF16), wc_ref[0], preferred_element_type=F32)
    y_ref[0] = (y + d_ref[0] * u.astype(F32)).astype(y_ref.dtype)


def _ssm(xp, wb2, mt2, wc2, d2, ar2, ai2, bsz, nchunk, ncc):
    ns, r, cw = xp.shape
    pw = SSM_PW
    nq = cw // pw
    wspec = lambda shp: pl.BlockSpec((1,) + shp, lambda g, q: (g * nq + q, 0, 0))
    return pl.pallas_call(
        functools.partial(_ssm_kernel, bsz=bsz, nchunk=nchunk, ncc=ncc),
        grid=(ns, nq),
        in_specs=[pl.BlockSpec((1, r, pw), lambda g, q: (g, 0, q)),
                  wspec((pw, pw)), wspec((pw, pw)), wspec((pw, pw)),
                  wspec((1, pw)), wspec((1, 2 * LANES)), wspec((1, 2 * LANES))],
        out_specs=pl.BlockSpec((1, r, pw), lambda g, q: (g, 0, q)),
        out_shape=jax.ShapeDtypeStruct((ns, r, cw), BF16),
        scratch_shapes=[pltpu.VMEM((r, pw), F32)],
        compiler_params=_cparams(("arbitrary", "arbitrary")),
        name="ssm",
    )(xp, wb2, mt2, wc2, d2, ar2, ai2)


def _ssm_perm():
    r = np.arange(SSM_CW)
    s, g, mm = r // LANES, (r % LANES) // SSM_GROUP, r % SSM_GROUP
    dst = g * (SSM_CHUNK * SSM_GROUP) + s * SSM_GROUP + mm
    return (dst[:, None] == np.arange(SSM_CW)[None, :]).astype(np.float32)


def _ssm_params(lam_re, lam_im, log_dt, b_re, b_im, c_re, c_im, d_skip):
    L = SSM_CHUNK
    lam_re, lam_im = lam_re.astype(F32), lam_im.astype(F32)
    dt = jnp.exp(log_dt.astype(F32))[..., None]
    tau = jnp.arange(L + 1, dtype=F32)[:, None, None, None]
    mag = jnp.exp(lam_re * dt * tau)
    pw_re = mag * jnp.cos(lam_im * dt * tau)
    pw_im = mag * jnp.sin(lam_im * dt * tau)
    a_re, a_im = pw_re[1], pw_im[1]
    den = lam_re * lam_re + lam_im * lam_im
    z_re = ((a_re - 1.0) * lam_re + a_im * lam_im) / den
    z_im = (a_im * lam_re - (a_re - 1.0) * lam_im) / den
    bb_re = z_re[..., None] * b_re.astype(F32) - z_im[..., None] * b_im.astype(F32)
    bb_im = z_re[..., None] * b_im.astype(F32) + z_im[..., None] * b_re.astype(F32)
    cr, ci = c_re.astype(F32), c_im.astype(F32)

    ab_re = pw_re[..., None] * bb_re - pw_im[..., None] * bb_im
    ab_im = pw_re[..., None] * bb_im + pw_im[..., None] * bb_re
    ktap = (jnp.einsum('dgnp,tdgpm->tdgnm', cr, ab_re) - jnp.einsum('dgnp,tdgpm->tdgnm', ci, ab_im))[:L]
    i_idx = np.arange(L)[:, None]
    s_idx = np.arange(L)[None, :]
    lag_f = i_idx - s_idx
    lag_r = s_idx - i_idx
    kf = jnp.where((lag_f >= 0)[:, :, None, None, None], ktap[np.clip(lag_f, 0, L - 1), 0], 0.0)
    kr = jnp.where((lag_r >= 0)[:, :, None, None, None], ktap[np.clip(lag_r, 0, L - 1), 1], 0.0)
    w1 = L * SSM_GROUP
    mt = jnp.transpose(kf + kr, (2, 1, 4, 0, 3)).reshape(SSM_GROUPS, w1, w1)

    exp_f = L - 1 - np.arange(L)
    exp_r = np.arange(L)
    inj = lambda ab, d, exps: jnp.transpose(ab[exps, d], (1, 0, 3, 2)).reshape(SSM_GROUPS, w1, SSM_STATE)
    wb_parts = [inj(ab_re, 0, exp_f), inj(ab_im, 0, exp_f), inj(ab_re, 1, exp_r), inj(ab_im, 1, exp_r)]

    def ro(d, exps):
        pr, pi = pw_re[exps, d], pw_im[exps, d]
        e_re = cr[d][None] * pr[:, :, None, :] - ci[d][None] * pi[:, :, None, :]
        e_im = cr[d][None] * pi[:, :, None, :] + ci[d][None] * pr[:, :, None, :]
        f = lambda e: jnp.transpose(e, (1, 3, 0, 2)).reshape(SSM_GROUPS, SSM_STATE, w1)
        return [f(e_re), f(-e_im)]

    wc_parts = ro(0, np.arange(L) + 1) + ro(1, L - np.arange(L))
    npair = SSM_GROUPS // 2

    def pair_diag(x):
        x = x.reshape(npair, 2, x.shape[1], x.shape[2])
        zed = jnp.zeros_like(x[:, 0])
        return jnp.concatenate([jnp.concatenate([x[:, 0], zed], axis=2),
                                jnp.concatenate([zed, x[:, 1]], axis=2)], axis=1)

    wb2 = jnp.concatenate([pair_diag(p) for p in wb_parts], axis=2)
    wc2 = jnp.concatenate([pair_diag(p) for p in wc_parts], axis=1)
    mt2 = pair_diag(mt)
    d2 = jnp.tile(d_skip.astype(F32).reshape(SSM_GROUPS, 1, SSM_GROUP), (1, L, 1)).reshape(npair, 1, 2 * w1)
    packp = lambda x: x.reshape(2, npair, 2 * SSM_STATE)
    ar2 = jnp.concatenate([packp(pw_re[L])[0], packp(pw_re[L])[1]], axis=-1)[:, None, :]
    ai2 = jnp.concatenate([packp(pw_im[L])[0], packp(pw_im[L])[1]], axis=-1)[:, None, :]
    return wb2.astype(BF16), mt2.astype(BF16), wc2.astype(BF16), d2, ar2, ai2


def _glu_kernel(y_ref, w_ref, b_ref, o_ref):
    y = y_ref[0].astype(F32)
    g = 0.5 * y * (1.0 + jnp.tanh(math.sqrt(2.0 / math.pi) * (y + 0.044715 * (y * y * y))))
    t = jnp.dot(g.astype(BF16), w_ref[...], preferred_element_type=F32) + b_ref[...]
    o_ref[0] = (g * jax.nn.sigmoid(t)).astype(o_ref.dtype)


def _glu(y3, w, b, ntile):
    bsz, t, n = y3.shape
    return pl.pallas_call(
        _glu_kernel,
        grid=(bsz, ntile),
        in_specs=[pl.BlockSpec((1, TOK, n), lambda i, j: (i, j, 0)),
                  pl.BlockSpec((n, n), lambda i, j: (0, 0)),
                  pl.BlockSpec((1, n), lambda i, j: (0, 0))],
        out_specs=pl.BlockSpec((1, TOK, n), lambda i, j: (i, j, 0)),
        out_shape=jax.ShapeDtypeStruct((bsz, ntile * TOK, n), BF16),
        compiler_params=_cparams(("arbitrary", "arbitrary")),
        name="glu",
    )(y3, w, b.reshape(1, n))


def _merge_kernel(oa_ref, ob_ref, os_ref, ga_ref, gb_ref, gs_ref, wa_ref, wb_ref, wc_ref, o_ref):
    acc = jax.nn.sigmoid(ga_ref[0].astype(F32)) * jnp.dot(oa_ref[0], wa_ref[...], preferred_element_type=F32)
    acc += jax.nn.sigmoid(gb_ref[0].astype(F32)) * jnp.dot(ob_ref[0], wb_ref[...], preferred_element_type=F32)
    acc += jax.nn.sigmoid(gs_ref[0].astype(F32)) * jnp.dot(os_ref[0], wc_ref[...], preferred_element_type=F32)
    o_ref[0] = acc.astype(o_ref.dtype)


def _merge(oa, ob, os_, z3, wa, wb, wc, ntile):
    bsz, _, k = oa.shape
    d = wa.shape[1]
    ospec = pl.BlockSpec((1, TOK, k), lambda i, j: (i, j, 0))
    gspec = lambda br: pl.BlockSpec((1, TOK, d), lambda i, j: (i, j, ZC_GATE // d + br))
    wspec = pl.BlockSpec((k, d), lambda i, j: (0, 0))
    return pl.pallas_call(
        _merge_kernel,
        grid=(bsz, ntile),
        in_specs=[ospec, ospec, ospec, gspec(0), gspec(1), gspec(2), wspec, wspec, wspec],
        out_specs=pl.BlockSpec((1, TOK, d), lambda i, j: (i, j, 0)),
        out_shape=jax.ShapeDtypeStruct((bsz, ntile * TOK, d), BF16),
        compiler_params=_cparams(("arbitrary", "arbitrary")),
        name="merge",
    )(oa, ob, os_, z3, z3, z3, wa, wb, wc)


def _outproj_kernel(x_ref, mix_ref, w_ref, m_ref, o_ref):
    gate = m_ref[0, 0][2:3]
    o_ref[0] = x_ref[0] + gate * jnp.dot(mix_ref[0], w_ref[...], preferred_element_type=F32)


def _outproj(x, mix3, w, modsel, nlt, ntile):
    b, t, d = x.shape
    return pl.pallas_call(
        _outproj_kernel,
        grid=(b, ntile),
        in_specs=[pl.BlockSpec((1, TOK, d), lambda i, j: (i, j, 0)),
                  pl.BlockSpec((1, TOK, d), lambda i, j: (i, j, 0)),
                  pl.BlockSpec((d, d), lambda i, j: (0, 0)),
                  pl.BlockSpec((1, 1, 6, d), lambda i, j: (i, _type_idx(j, nlt), 0, 0))],
        out_specs=pl.BlockSpec((1, TOK, d), lambda i, j: (i, j, 0)),
        out_shape=jax.ShapeDtypeStruct((b, ntile * TOK, d), F32),
        compiler_params=_cparams(("arbitrary", "arbitrary")),
        name="outproj",
    )(x, mix3, w, modsel)


def _router_kernel(x_ref, g_ref, m_ref, whi_ref, wlo_ref, b_ref, h_ref, meta_ref, tw_ref, cnt_ref, run_ref, *, n_exp):
    first = jnp.logical_and(pl.program_id(0) == 0, pl.program_id(1) == 0)

    @pl.when(first)
    def _():
        run_ref[...] = jnp.zeros_like(run_ref)

    m = m_ref[0, 0]
    h = _rms(x_ref[0]) * g_ref[...] * (1.0 + m[4:5]) + m[3:4]
    h_ref[0] = _pack_bf16_pairs(h)
    h_hi, h_lo = _split_bf16(h)
    logits = (jnp.dot(h_hi, whi_ref[...], preferred_element_type=F32)
              + jnp.dot(h_lo, whi_ref[...], preferred_element_type=F32)
              + jnp.dot(h_hi, wlo_ref[...], preferred_element_type=F32)) + b_ref[...]
    lane = lax.broadcasted_iota(jnp.int32, (TOK, LANES), 1).astype(F32)
    lg = jnp.where(lane < n_exp, logits, -jnp.inf)
    vals, idxs = [], []
    for _ in range(TOP_K):
        mx = jnp.max(lg, axis=-1, keepdims=True)
        idx = jnp.min(jnp.where(lg == mx, lane, float(LANES)), axis=-1, keepdims=True)
        vals.append(mx)
        idxs.append(idx)
        lg = jnp.where(lane == idx, -jnp.inf, lg)
    es = [jnp.exp(v - vals[0]) for v in vals]
    den = es[0] + es[1] + es[2] + es[3]
    onehot = jnp.zeros((TOK, LANES), F32)
    for idx in idxs:
        onehot = onehot + jnp.where(lane == idx, 1.0, 0.0)
    row = lax.broadcasted_iota(jnp.int32, (TOK, TOK), 0)
    col = lax.broadcasted_iota(jnp.int32, (TOK, TOK), 1)
    tri = jnp.where(col < row, 1.0, 0.0).astype(BF16)
    before = jnp.dot(tri, onehot.astype(BF16), preferred_element_type=F32) + run_ref[...]
    meta = jnp.zeros((TOK, LANES), F32)
    tw = jnp.zeros((TOK, LANES), F32)
    for k in range(TOP_K):
        rank = jnp.sum(jnp.where(lane == idxs[k], before, 0.0), axis=-1, keepdims=True)
        meta = meta + jnp.where(lane == float(k), idxs[k], 0.0) + jnp.where(lane == float(TOP_K + k), rank, 0.0)
        tw = tw + jnp.where(lane == float(k), es[k] / den, 0.0)
    meta_ref[0] = meta.astype(jnp.int32)
    tw_ref[0] = tw
    run = run_ref[...] + jnp.sum(onehot, axis=0, keepdims=True)
    run_ref[...] = run
    cnt_ref[...] = run


def _router(x, g, modsel, w, bvec, nlt, n_exp):
    b, t, d = x.shape
    tokspec = lambda last: pl.BlockSpec((1, TOK, last), lambda i, j: (i, j, 0))
    return pl.pallas_call(
        functools.partial(_router_kernel, n_exp=n_exp),
        grid=(b, t // TOK),
        in_specs=[tokspec(d),
                  pl.BlockSpec((1, d), lambda i, j: (0, 0)),
                  pl.BlockSpec((1, 1, 6, d), lambda i, j: (i, _type_idx(j, nlt), 0, 0)),
                  pl.BlockSpec((d, LANES), lambda i, j: (0, 0)),
                  pl.BlockSpec((d, LANES), lambda i, j: (0, 0)),
                  pl.BlockSpec((1, LANES), lambda i, j: (0, 0))],
        out_specs=[tokspec(d // 2), tokspec(LANES), tokspec(LANES), pl.BlockSpec((1, LANES), lambda i, j: (0, 0))],
        out_shape=[jax.ShapeDtypeStruct((b, t, d // 2), jnp.uint32),
                   jax.ShapeDtypeStruct((b, t, LANES), jnp.int32),
                   jax.ShapeDtypeStruct((b, t, LANES), F32),
                   jax.ShapeDtypeStruct((1, LANES), F32)],
        scratch_shapes=[pltpu.VMEM((1, LANES), F32)],
        compiler_params=_cparams(("arbitrary", "arbitrary")),
        name="router",
    )(x, g, modsel, *_split_bf16(w), bvec)


def _dispatch_kernel(pos_ref, h_ref, xs_in_ref, xs_ref, sem):
    del xs_in_ref

    def row_copy(r, k):
        return pltpu.make_async_copy(h_ref.at[pl.ds(r, 1)], xs_ref.at[pl.ds(pos_ref[r * TOP_K + k], 1)], sem)

    def start(r, c):
        for k in range(TOP_K):
            row_copy(r, k).start(priority=k % 2)
        return c

    def wait(r, c):
        for k in range(TOP_K):
            row_copy(r, k).wait()
        return c

    lax.fori_loop(0, TOK, start, 0)
    lax.fori_loop(0, TOK, wait, 0, unroll=DMA_UNROLL)


def _dispatch(pos, h2, xs0):
    m, d = h2.shape
    return pl.pallas_call(
        _dispatch_kernel,
        grid=(m // TOK,),
        in_specs=[pl.BlockSpec((TOK * TOP_K,), lambda i: (i,), memory_space=pltpu.SMEM),
                  pl.BlockSpec((TOK, d), lambda i: (i, 0)),
                  pl.BlockSpec(memory_space=pl.ANY)],
        out_specs=pl.BlockSpec(memory_space=pl.ANY),
        out_shape=jax.ShapeDtypeStruct(xs0.shape, xs0.dtype),
        scratch_shapes=[pltpu.SemaphoreType.DMA(())],
        input_output_aliases={2: 0},
        compiler_params=_cparams(("arbitrary",)),
        name="dispatch",
    )(pos, h2, xs0)


def _experts_kernel(te_ref, tv_ref, x_ref, wgu_ref, bgu_ref, wd_ref, bd_ref, y_ref, *, d_ff, fchunk):
    i = pl.program_id(0)

    @pl.when(tv_ref[i] == 0)
    def _():
        y_ref[...] = jnp.zeros_like(y_ref)

    @pl.when(tv_ref[i] != 0)
    def _():
        x = jnp.concatenate(_unpack_bf16_pairs(x_ref[...]), axis=1).astype(BF16)
        acc = jnp.zeros(x.shape, F32)
        for c in range(d_ff // fchunk):
            lo = c * fchunk
            g = jnp.dot(x, wgu_ref[0, 0, :, lo:lo + fchunk], preferred_element_type=F32) + bgu_ref[0, :, lo:lo + fchunk]
            lin = (jnp.dot(x, wgu_ref[0, 0, :, d_ff + lo:d_ff + lo + fchunk], preferred_element_type=F32)
                   + bgu_ref[0, :, d_ff + lo:d_ff + lo + fchunk])
            g = jnp.minimum(g, SWIGLU_LIMIT)
            lin = jnp.clip(lin, -SWIGLU_LIMIT, SWIGLU_LIMIT)
            act = g * jax.nn.sigmoid(SWIGLU_ALPHA * g) * (lin + 1.0)
            acc = acc + jnp.dot(act.astype(BF16), wd_ref[0, 0, lo:lo + fchunk, :], preferred_element_type=F32)
        y_ref[...] = _pack_bf16_pairs(acc + bd_ref[0])


def _experts(te, tv, xs, wgu, bgu, wd, bd, layer):
    rows, dh = xs.shape
    _, n_exp, d, ff2 = wgu.shape
    d_ff = ff2 // 2
    grid_spec = pltpu.PrefetchScalarGridSpec(
        num_scalar_prefetch=2,
        grid=(rows // TOK,),
        in_specs=[pl.BlockSpec((TOK, dh), lambda i, te, tv: (i, 0)),
                  pl.BlockSpec((1, 1, d, ff2), lambda i, te, tv: (layer, te[i], 0, 0)),
                  pl.BlockSpec((1, 1, ff2), lambda i, te, tv: (te[i], 0, 0)),
                  pl.BlockSpec((1, 1, d_ff, d), lambda i, te, tv: (layer, te[i], 0, 0)),
                  pl.BlockSpec((1, 1, d), lambda i, te, tv: (te[i], 0, 0))],
        out_specs=pl.BlockSpec((TOK, dh), lambda i, te, tv: (i, 0)),
    )
    return pl.pallas_call(
        functools.partial(_experts_kernel, d_ff=d_ff, fchunk=512),
        grid_spec=grid_spec,
        out_shape=jax.ShapeDtypeStruct((rows, dh), jnp.uint32),
        compiler_params=_cparams(("arbitrary",)),
        name="experts",
    )(te, tv, xs, wgu, bgu.reshape(n_exp, 1, ff2), wd, bd.reshape(n_exp, 1, d))


def _combine_kernel(pos_ref, x_ref, tw_ref, m_ref, ys_ref, o_ref, buf_ref, sem):
    def row_copy(r, k):
        return pltpu.make_async_copy(ys_ref.at[pl.ds(pos_ref[r * TOP_K + k], 1)], buf_ref.at[k, pl.ds(r, 1)], sem)

    def start(r, c):
        for k in range(TOP_K):
            row_copy(r, k).start(priority=k % 2)
        return c

    def wait(r, c):
        for k in range(TOP_K):
            row_copy(r, k).wait()
        return c

    lax.fori_loop(0, TOK, start, 0, unroll=DMA_UNROLL)
    lax.fori_loop(0, TOK, wait, 0, unroll=DMA_UNROLL)
    tw = tw_ref[0]
    acc_lo = acc_hi = None
    for k in range(TOP_K):
        lo, hi = _unpack_bf16_pairs(buf_ref[k])
        wk = tw[:, k:k + 1]
        acc_lo = wk * lo if acc_lo is None else acc_lo + wk * lo
        acc_hi = wk * hi if acc_hi is None else acc_hi + wk * hi
    o_ref[0] = x_ref[0] + m_ref[0, 0][5:6] * jnp.concatenate([acc_lo, acc_hi], axis=1)


def _combine(pos, x, tw, modsel, ys, nlt):
    b, t, d = x.shape
    nt = t // TOK
    return pl.pallas_call(
        _combine_kernel,
        grid=(b, nt),
        in_specs=[pl.BlockSpec((TOK * TOP_K,), lambda i, j: (i * nt + j,), memory_space=pltpu.SMEM),
                  pl.BlockSpec((1, TOK, d), lambda i, j: (i, j, 0)),
                  pl.BlockSpec((1, TOK, LANES), lambda i, j: (i, j, 0)),
                  pl.BlockSpec((1, 1, 6, d), lambda i, j: (i, _type_idx(j, nlt), 0, 0)),
                  pl.BlockSpec(memory_space=pl.ANY)],
        out_specs=pl.BlockSpec((1, TOK, d), lambda i, j: (i, j, 0)),
        out_shape=jax.ShapeDtypeStruct((b, t, d), F32),
        scratch_shapes=[pltpu.VMEM((TOP_K, TOK, d // 2), jnp.uint32), pltpu.SemaphoreType.DMA(())],
        compiler_params=_cparams(("arbitrary", "arbitrary")),
        name="combine",
    )(pos, x, tw, modsel, ys)


def _moe(xa, norm_g, modsel, w_router, b_router, wgu, bgu, wd, bd, nlt, layer):
    b, t, d = xa.shape
    m = b * t
    n_exp = w_router.shape[1]
    wr = jnp.zeros((d, LANES), F32).at[:, :n_exp].set(w_router.astype(F32))
    br = jnp.zeros((1, LANES), F32).at[0, :n_exp].set(b_router.astype(F32))
    h2, meta, tw, cnt = _router(xa, norm_g.reshape(1, d), modsel, wr, br, nlt, n_exp)

    counts = cnt[0, :n_exp].astype(jnp.int32)
    padded = ((counts + TOK - 1) // TOK) * TOK
    gend = jnp.cumsum(padded)
    gstart = gend - padded
    meta2 = meta.reshape(m, LANES)
    pos = (gstart[meta2[:, 0:TOP_K]] + meta2[:, TOP_K:2 * TOP_K]).reshape(m * TOP_K)
    ntiles = (m * TOP_K) // TOK + n_exp
    tile_start = jnp.arange(ntiles, dtype=jnp.int32) * TOK
    te = jnp.sum((tile_start[:, None] >= gend[None, :]).astype(jnp.int32), axis=1)
    te = jnp.minimum(te, n_exp - 1)
    tv = (tile_start < gend[-1]).astype(jnp.int32)

    xs = _dispatch(pos, h2.reshape(m, d // 2), jnp.zeros((ntiles * TOK, d // 2), jnp.uint32))
    ys = _experts(te, tv, xs, wgu, bgu, wd, bd, layer)
    return _combine(pos, xa, tw, modsel, ys, nlt)


def _rope_tables(seq, ctx):
    t = jnp.arange(seq, dtype=jnp.int32)
    row = (t // GRID_W).astype(F32)
    col = (t % GRID_W).astype(F32)
    n_freq = MLA_ROPE // 4
    inv_freq = ROPE_BASE ** (-jnp.arange(n_freq, dtype=F32) / n_freq)
    ang = jnp.concatenate([row[:, None] * inv_freq, col[:, None] * inv_freq], axis=-1)
    cos, sin = jnp.cos(ang), jnp.sin(ang)
    cos2 = jnp.tile(jnp.concatenate([cos, cos], axis=-1), (1, 2))
    sin2 = jnp.tile(jnp.concatenate([-sin, sin], axis=-1), (1, 2))
    cos2 = jnp.concatenate([cos2, jnp.ones((ctx, LANES), F32)], axis=0)
    sin2 = jnp.concatenate([sin2, jnp.zeros((ctx, LANES), F32)], axis=0)
    return cos2, sin2


def _pad_cols(w, width):
    return jnp.pad(w, ((0, 0), (0, width - w.shape[1])))


def _layout_w_in(w):
    o_ckv = MLA_Q_RANK
    o_kpe = o_ckv + MLA_KV_RANK
    o_naq = o_kpe + MLA_ROPE
    o_gate = o_naq + 3 * NA_WIDTH + SSM_WIDTH
    parts = [w[:, o_gate:],
             w[:, :o_ckv],
             _pad_cols(w[:, o_kpe:o_naq], ZC_CKV - ZC_KPE),
             w[:, o_ckv:o_kpe],
             w[:, o_naq:o_gate]]
    out = jnp.concatenate(parts, axis=1)
    assert out.shape[1] == ZW
    return out.astype(BF16)


def kernel(x, c, ctx, c_ctx, w_ada, b_ada, norm1_g, w_in, mla_q_norm, mla_kv_norm, mla_w_uq, mla_w_ukv, mla_q_gain, mla_k_gain, na_q_gain, na_k_gain, na_rpb, ssm_lam_re, ssm_lam_im, ssm_log_dt, ssm_b_re, ssm_b_im, ssm_c_re, ssm_c_im, ssm_d, ssm_w_glu, ssm_b_glu, w_pa, w_pb, w_pc, w_out, norm2_g, w_router, b_router, w_gate_up, b_gate_up, w_down, b_down):
    bsz, seq, d = x.shape
    nctx = ctx.shape[1]
    depth = w_ada.shape[0]
    assert nctx == TOK and seq % (NA_QROWS * GRID_W) == 0 and seq // GRID_W >= NA_KROWS
    nlt = seq // TOK
    t = seq + nctx
    m = bsz * t
    rows = seq // GRID_W
    nblk = rows // NA_QROWS
    nchunk = t // SSM_CHUNK

    cos2, sin2 = _rope_tables(seq, nctx)
    cvec = jnp.zeros((16, d), F32).at[:bsz].set(c.astype(F32)).at[bsz].set(c_ctx.astype(F32))
    xa = jnp.concatenate([x, ctx], axis=1).astype(F32)
    row128 = lambda v: v.astype(F32).reshape(1, LANES)
    perm = jnp.asarray(_ssm_perm(), BF16)
    wgu_all = w_gate_up.astype(BF16)
    wd_all = w_down.astype(BF16)

    for l in range(depth):
        need_ctx = l < depth - 1
        ntile = t // TOK if need_ctx else nlt
        mod = _ada(cvec, w_ada[l], b_ada[l])
        mod_lat = mod[:bsz].reshape(bsz, 1, 6, d)
        mod_ctx = jnp.broadcast_to(mod[bsz].reshape(1, 1, 6, d), (bsz, 1, 6, d))
        modsel = jnp.concatenate([mod_lat, mod_ctx], axis=1)

        h1 = _norm_mod(xa, norm1_g[l], modsel, nlt)
        z = _matmul(h1.reshape(m, d), _layout_w_in(w_in[l]), BF16, "in_proj", ZW // 4)
        z3 = z.reshape(bsz, t, ZW)

        wq = mla_w_uq[l].reshape(MLA_Q_RANK, MLA_HEADS, MLA_QK)
        wq = jnp.concatenate([wq[:, :, :MLA_NOPE].reshape(MLA_Q_RANK, -1),
                              wq[:, :, MLA_NOPE:].reshape(MLA_Q_RANK, -1)], axis=1).astype(BF16)
        qg_r = jnp.tile(mla_q_gain[l][MLA_NOPE:], 2)
        kg_r = jnp.concatenate([mla_k_gain[l][MLA_NOPE:], jnp.zeros((MLA_ROPE,), mla_k_gain.dtype)])
        qf = _mla_q(z3, mla_q_norm[l].astype(F32).reshape(1, -1), wq, row128(mla_q_gain[l][:MLA_NOPE]),
                    row128(qg_r), cos2, sin2, ntile)
        kf, vf = _mla_kv(z3, mla_kv_norm[l].astype(F32).reshape(1, -1), mla_w_ukv[l].astype(BF16),
                         row128(mla_k_gain[l][:MLA_NOPE]), row128(kg_r), cos2, sin2)
        o_a = _mla_attn(qf, kf, vf, seq, need_ctx)

        bias = _na_bias_tables(na_rpb[l], rows)
        qn, kn = _na_norm(z3, row128(jnp.tile(na_q_gain[l], 2)), row128(jnp.tile(na_k_gain[l], 2)))
        o_b = _na_attn(qn, kn, z3, bias, nblk, need_ctx)

        wb2, mt2, wc2, d2, ar2, ai2 = _ssm_params(ssm_lam_re[l], ssm_lam_im[l], ssm_log_dt[l], ssm_b_re[l],
                                                   ssm_b_im[l], ssm_c_re[l], ssm_c_im[l], ssm_d[l])
        r = nchunk * bsz
        u = z3[:, :, ZC_SSM:ZC_SSM + SSM_WIDTH].reshape(bsz, nchunk, SSM_CHUNK, SSM_NSUPER, LANES)
        x8 = jnp.transpose(u, (3, 1, 0, 2, 4)).reshape(SSM_NSUPER * r, SSM_CW)
        xp = _matmul(x8, perm, BF16, "ssm_perm_in", SSM_CW)
        yp = _ssm(xp.reshape(SSM_NSUPER, r, SSM_CW), wb2, mt2, wc2, d2, ar2, ai2, bsz, nchunk, nctx // SSM_CHUNK)
        y8 = _matmul(yp.reshape(SSM_NSUPER * r, SSM_CW), perm.T, BF16, "ssm_perm_out", SSM_CW)
        y = jnp.transpose(y8.reshape(SSM_NSUPER, nchunk, bsz, SSM_CHUNK, LANES), (2, 1, 3, 0, 4))
        o_s = _glu(y.reshape(bsz, t, SSM_WIDTH), ssm_w_glu[l].astype(BF16), ssm_b_glu[l].astype(F32), ntile)

        mix = _merge(o_a, o_b, o_s, z3, w_pa[l].astype(BF16), w_pb[l].astype(BF16), w_pc[l].astype(BF16), ntile)
        xa = _outproj(xa, mix, w_out[l].astype(BF16), modsel, nlt, ntile)
        xa = _moe(xa, norm2_g[l].astype(F32), modsel, w_router[l], b_router[l], wgu_all, b_gate_up[l].astype(F32),
                  wd_all, b_down[l].astype(F32), nlt, l)
    return xa[:, :seq].astype(x.dtype)
```

```python
import functools
import math

import numpy as np
import jax
import jax.numpy as jnp
from jax import lax
from jax.experimental import pallas as pl
from jax.experimental.pallas import tpu as pltpu

F32 = jnp.float32
BF16 = jnp.bfloat16

GRID_W = 64
MLA_HEADS = 8
MLA_NOPE = 128
MLA_ROPE = 64
MLA_V = 128
MLA_QK = MLA_NOPE + MLA_ROPE
MLA_Q_RANK = 768
MLA_KV_RANK = 512
NA_HEADS = 16
NA_HEAD_DIM = 64
NA_WIDTH = NA_HEADS * NA_HEAD_DIM
NA_WIN_R = 8
NA_WIN_C = 16
SSM_WIDTH = 1024
SSM_GROUP = 16
SSM_GROUPS = SSM_WIDTH // SSM_GROUP
SSM_STATE = 64
TOP_K = 4
SWIGLU_LIMIT = 7.0
SWIGLU_ALPHA = 1.702
ROPE_BASE = 10000.0
EPS = 1e-6
MLA_SCALE = MLA_QK ** -0.5
NA_SCALE = NA_HEAD_DIM ** -0.5
NEG_BIG = -1e30

LANES = 128
TOK = 256
MLA_KPAD = 256
MLA_VPAD = 256
MLA_TQ = 1024
MLA_TK = 512
VMEM_LIMIT = 56 * 1024 * 1024
DMA_UNROLL = 4

ZC_GATE = 0
ZC_NAQ = 6144
ZC_NAK = 7168
ZC_NAV = 8192
ZC_CQ = 9216
ZC_KPE = 9984
ZC_CKV = 10240
ZC_SSM = 10752
ZW = 11776

SSM_CHUNK = 16
SSM_NSUPER = SSM_WIDTH // LANES
SSM_CW = SSM_CHUNK * LANES
SSM_PW = 2 * SSM_CHUNK * SSM_GROUP
NA_QROWS = 4
NA_KROWS = 12
NA_BB = 2


def _cparams(sem, vmem=VMEM_LIMIT):
    return pltpu.CompilerParams(dimension_semantics=sem, vmem_limit_bytes=vmem)


def _rms(x, eps=EPS):
    return x * lax.rsqrt(jnp.mean(x * x, axis=-1, keepdims=True) + eps)


def _split_bf16(x):
    hi = x.astype(BF16)
    return hi, (x - hi.astype(F32)).astype(BF16)


def _seg64_rms(t):
    r = lax.broadcasted_iota(jnp.int32, (LANES, LANES), 0)
    c = lax.broadcasted_iota(jnp.int32, (LANES, LANES), 1)
    same_half = jnp.where((r < 64) == (c < 64), 1.0, 0.0).astype(BF16)
    hi, lo = _split_bf16(t * t)
    ssum = (jnp.dot(hi, same_half, preferred_element_type=F32) + jnp.dot(lo, same_half, preferred_element_type=F32))
    return t * lax.rsqrt(ssum * (1.0 / 64.0) + EPS)


def _rope64(t, cos2, sin2):
    lane = lax.broadcasted_iota(jnp.int32, t.shape, 1)
    first = (lane % 64) < 32
    swapped = jnp.where(first, pltpu.roll(t, 96, axis=1), pltpu.roll(t, 32, axis=1))
    return t * cos2 + swapped * sin2


def _ones_col_tile(rows, dtype):
    lane = lax.broadcasted_iota(jnp.int32, (rows, LANES), 1)
    return jnp.where(lane == 0, 1.0, 0.0).astype(dtype)


def _qkt(q, k):
    return lax.dot_general(q, k, (((1,), (1,)), ((), ())), preferred_element_type=F32)


def _softmax_acc(s, v_ext):
    m = jnp.max(s, axis=-1, keepdims=True)
    return jnp.dot(jnp.exp((s - m).astype(BF16)), v_ext, preferred_element_type=F32)


def _softmax_pv(s, v_ext):
    oe = _softmax_acc(s, v_ext)
    return oe[:, 0:LANES] / oe[:, LANES:LANES + 1]


def _pack_bf16_pairs(x):
    n = x.shape[1] // 2
    r = x.astype(BF16).astype(F32)
    lo = lax.bitcast_convert_type(r[:, :n], jnp.uint32) >> 16
    hi = lax.bitcast_convert_type(r[:, n:], jnp.uint32) & jnp.uint32(0xFFFF0000)
    return hi | lo


def _unpack_bf16_pairs(w):
    lo = lax.bitcast_convert_type(w << 16, F32)
    hi = lax.bitcast_convert_type(w & jnp.uint32(0xFFFF0000), F32)
    return lo, hi


def _type_idx(j, nlt):
    return jnp.where(j >= nlt, 1, 0)


def _ada_kernel(c_ref, w_ref, b_ref, o_ref):
    c = c_ref[...]
    a = (c * jax.nn.sigmoid(c)).astype(BF16)
    o_ref[...] = jnp.dot(a, w_ref[...].astype(BF16), preferred_element_type=F32) + b_ref[...]


def _ada(cvec, w, b):
    rows, d = cvec.shape
    n = w.shape[1]
    tn = 1024
    return pl.pallas_call(
        _ada_kernel,
        grid=(n // tn,),
        in_specs=[pl.BlockSpec((rows, d), lambda j: (0, 0)),
                  pl.BlockSpec((d, tn), lambda j: (0, j)),
                  pl.BlockSpec((1, tn), lambda j: (0, j))],
        out_specs=pl.BlockSpec((rows, tn), lambda j: (0, j)),
        out_shape=jax.ShapeDtypeStruct((rows, n), F32),
        compiler_params=_cparams(("arbitrary",)),
        name="ada",
    )(cvec, w, b.reshape(1, n))


def _norm_mod_kernel(x_ref, g_ref, m_ref, o_ref):
    m = m_ref[0, 0]
    y = _rms(x_ref[0]) * g_ref[...]
    o_ref[0] = (y * (1.0 + m[1:2]) + m[0:1]).astype(o_ref.dtype)


def _norm_mod(x, g, modsel, nlt):
    b, t, d = x.shape
    return pl.pallas_call(
        _norm_mod_kernel,
        grid=(b, t // TOK),
        in_specs=[pl.BlockSpec((1, TOK, d), lambda i, j: (i, j, 0)),
                  pl.BlockSpec((1, d), lambda i, j: (0, 0)),
                  pl.BlockSpec((1, 1, 6, d), lambda i, j: (i, _type_idx(j, nlt), 0, 0))],
        out_specs=pl.BlockSpec((1, TOK, d), lambda i, j: (i, j, 0)),
        out_shape=jax.ShapeDtypeStruct((b, t, d), BF16),
        compiler_params=_cparams(("arbitrary", "arbitrary")),
        name="norm_mod",
    )(x, g.reshape(1, d), modsel)


def _mm_kernel(a_ref, w_ref, o_ref):
    o_ref[...] = jnp.dot(a_ref[...], w_ref[...], preferred_element_type=F32).astype(o_ref.dtype)


def _pick(n, cands):
    for c in cands:
        if n % c == 0:
            return c
    return n


def _matmul(a, w, out_dtype, name, tn):
    m, k = a.shape
    n = w.shape[1]
    tm = _pick(m, (512, 256, 128))
    return pl.pallas_call(
        _mm_kernel,
        grid=(n // tn, m // tm),
        in_specs=[pl.BlockSpec((tm, k), lambda j, i: (i, 0)),
                  pl.BlockSpec((k, tn), lambda j, i: (0, j))],
        out_specs=pl.BlockSpec((tm, tn), lambda j, i: (i, j)),
        out_shape=jax.ShapeDtypeStruct((m, n), out_dtype),
        compiler_params=_cparams(("arbitrary", "arbitrary")),
        name=name,
    )(a, w)


def _mla_q_part(z_ref, g_ref, w_ref, gn_ref, gr_ref, cos, sin, o_ref):
    cq = z_ref[0].astype(F32)
    xn = (_rms(cq) * g_ref[...]).astype(BF16)
    q = jnp.dot(xn, w_ref[...], preferred_element_type=F32)
    lane = lax.broadcasted_iota(jnp.int32, (TOK, LANES), 1)
    left = lane < 64
    for h in range(MLA_HEADS):
        qn = q[:, h * MLA_NOPE:(h + 1) * MLA_NOPE]
        o_ref[0, h, :, 0:LANES] = (_rms(qn) * (gn_ref[...] * MLA_SCALE)).astype(o_ref.dtype)
    base = MLA_HEADS * MLA_NOPE
    for hp in range(MLA_HEADS // 2):
        t = q[:, base + hp * LANES: base + (hp + 1) * LANES]
        t = _seg64_rms(t) * (gr_ref[...] * MLA_SCALE)
        r = _rope64(t, cos, sin)
        o_ref[0, 2 * hp, :, LANES:2 * LANES] = jnp.where(left, r, 0.0).astype(o_ref.dtype)
        o_ref[0, 2 * hp + 1, :, LANES:2 * LANES] = jnp.where(left, pltpu.roll(r, 64, axis=1), 0.0).astype(o_ref.dtype)


def _mla_kv_part(z_ref, zk_ref, g_ref, w_ref, gn_ref, gr_ref, cos, sin, k_ref, v_ref):
    ckv = z_ref[0].astype(F32)
    xn = (_rms(ckv) * g_ref[...]).astype(BF16)
    kv = jnp.dot(xn, w_ref[...], preferred_element_type=F32)
    lane = lax.broadcasted_iota(jnp.int32, (TOK, LANES), 1)
    left = lane < 64
    t = jnp.where(left, zk_ref[0].astype(F32), 0.0)
    ms = jnp.sum(t * t, axis=-1, keepdims=True) * (1.0 / MLA_ROPE)
    kr = t * lax.rsqrt(ms + EPS) * gr_ref[...]
    kr = jnp.where(left, _rope64(kr, cos, sin), 0.0).astype(k_ref.dtype)
    ones = _ones_col_tile(TOK, v_ref.dtype)
    hw = MLA_NOPE + MLA_V
    for h in range(MLA_HEADS):
        kn = kv[:, h * hw: h * hw + MLA_NOPE]
        k_ref[0, h, :, 0:LANES] = (_rms(kn) * gn_ref[...]).astype(k_ref.dtype)
        k_ref[0, h, :, LANES:2 * LANES] = kr
        v_ref[0, h, :, 0:LANES] = kv[:, h * hw + MLA_NOPE:(h + 1) * hw].astype(v_ref.dtype)
        v_ref[0, h, :, LANES:2 * LANES] = ones


def _na_norm_part(q_ref, k_ref, gq_ref, gk_ref, qn_ref, kn_ref):
    for c in range(q_ref.shape[2] // LANES):
        sl = slice(c * LANES, (c + 1) * LANES)
        qn_ref[0, :, sl] = (_seg64_rms(q_ref[0, :, sl].astype(F32)) * (gq_ref[...] * NA_SCALE)).astype(qn_ref.dtype)
        kn_ref[0, :, sl] = (_seg64_rms(k_ref[0, :, sl].astype(F32)) * gk_ref[...]).astype(kn_ref.dtype)


def _proj_kernel(zq_ref, zkv_ref, zk_ref, znq_ref, znk_ref, gq_ref, wq_ref, qgn_ref, qgr_ref, gkv_ref, wkv_ref,
                 kgn_ref, kgr_ref, cos_ref, sin_ref, ngq_ref, ngk_ref, q_ref, k_ref, v_ref, qn_ref, kn_ref):
    cos, sin = cos_ref[...], sin_ref[...]
    _mla_q_part(zq_ref, gq_ref, wq_ref, qgn_ref, qgr_ref, cos, sin, q_ref)
    _mla_kv_part(zkv_ref, zk_ref, gkv_ref, wkv_ref, kgn_ref, kgr_ref, cos, sin, k_ref, v_ref)
    _na_norm_part(znq_ref, znk_ref, ngq_ref, ngk_ref, qn_ref, kn_ref)


def _projections(z3, gq, wq, qgn, qgr2, gkv, wkv, kgn, kgr2, cos2, sin2, ngq2, ngk2):
    b, t, _ = z3.shape
    zspec = lambda w, col0: pl.BlockSpec((1, TOK, w), lambda i, j: (i, j, col0 // w))
    const = lambda a: pl.BlockSpec(a.shape, lambda i, j: (0,) * a.ndim)
    tab = pl.BlockSpec((TOK, LANES), lambda i, j: (j, 0))
    head_spec = lambda w: pl.BlockSpec((1, MLA_HEADS, TOK, w), lambda i, j: (i, 0, j, 0))
    na_spec = pl.BlockSpec((1, TOK, NA_WIDTH), lambda i, j: (i, j, 0))
    head_shape = lambda w: jax.ShapeDtypeStruct((b, MLA_HEADS, t, w), BF16)
    na_shape = jax.ShapeDtypeStruct((b, t, NA_WIDTH), BF16)
    return pl.pallas_call(
        _proj_kernel,
        grid=(b, t // TOK),
        in_specs=[zspec(MLA_Q_RANK, ZC_CQ), zspec(MLA_KV_RANK, ZC_CKV), zspec(LANES, ZC_KPE),
                  zspec(NA_WIDTH, ZC_NAQ), zspec(NA_WIDTH, ZC_NAK),
                  const(gq), const(wq), const(qgn), const(qgr2), const(gkv), const(wkv), const(kgn), const(kgr2),
                  tab, tab, const(ngq2), const(ngk2)],
        out_specs=[head_spec(MLA_KPAD), head_spec(MLA_KPAD), head_spec(MLA_VPAD), na_spec, na_spec],
        out_shape=[head_shape(MLA_KPAD), head_shape(MLA_KPAD), head_shape(MLA_VPAD), na_shape, na_shape],
        compiler_params=_cparams(("arbitrary", "arbitrary")),
        name="projections",
    )(z3, z3, z3, z3, z3, gq, wq, qgn, qgr2, gkv, wkv, kgn, kgr2, cos2, sin2, ngq2, ngk2)


def _mla_attn_kernel(q_ref, k_ref, v_ref, o_ref, *, nk, tk):
    q = q_ref[0, 0]
    m = acc = None
    for c0 in range(0, nk, tk):
        sz = min(tk, nk - c0)
        s = _qkt(q, k_ref[0, 0, c0:c0 + sz, :])
        cm = jnp.max(s, axis=-1, keepdims=True)
        m_new = cm if m is None else jnp.maximum(m, cm)
        pv = jnp.dot(jnp.exp((s - m_new).astype(BF16)), v_ref[0, 0, c0:c0 + sz, :], preferred_element_type=F32)
        acc = pv if m is None else acc * jnp.exp(m - m_new) + pv
        m = m_new
    o_ref[0] = (acc[:, 0:LANES] / acc[:, LANES:LANES + 1]).astype(o_ref.dtype)


def _mla_attn_ctx_kernel(q_ref, k_ref, v_ref, o_ref):
    o_ref[0] = _softmax_pv(_qkt(q_ref[0, 0], k_ref[0, 0]), v_ref[0, 0]).astype(o_ref.dtype)


def _mla_attn(q, k, v, seq, need_ctx):
    b, h, t, _ = k.shape
    tq = _pick(seq, (MLA_TQ, TOK))
    sem = ("arbitrary", "arbitrary", "arbitrary")
    o = pl.pallas_call(
        functools.partial(_mla_attn_kernel, nk=t, tk=MLA_TK),
        grid=(b, h, seq // tq),
        in_specs=[pl.BlockSpec((1, 1, tq, MLA_KPAD), lambda i, hh, j: (i, hh, j, 0)),
                  pl.BlockSpec((1, 1, t, MLA_KPAD), lambda i, hh, j: (i, hh, 0, 0)),
                  pl.BlockSpec((1, 1, t, MLA_VPAD), lambda i, hh, j: (i, hh, 0, 0))],
        out_specs=pl.BlockSpec((1, tq, MLA_V), lambda i, hh, j: (i, j, hh)),
        out_shape=jax.ShapeDtypeStruct((b, seq, h * MLA_V), BF16),
        compiler_params=_cparams(sem),
        name="mla_attn",
    )(q, k, v)
    if not need_ctx:
        return o
    nctx = t - seq
    ctx_spec = lambda w: pl.BlockSpec((1, 1, nctx, w), lambda i, hh, j: (i, hh, seq // nctx, 0))
    o_c = pl.pallas_call(
        _mla_attn_ctx_kernel,
        grid=(b, h, 1),
        in_specs=[ctx_spec(MLA_KPAD), ctx_spec(MLA_KPAD), ctx_spec(MLA_VPAD)],
        out_specs=pl.BlockSpec((1, nctx, MLA_V), lambda i, hh, j: (i, 0, hh)),
        out_shape=jax.ShapeDtypeStruct((b, nctx, h * MLA_V), BF16),
        compiler_params=_cparams(sem),
        name="mla_attn_ctx",
    )(q, k, v)
    return jnp.concatenate([o, o_c], axis=1)


def _na_kernel(q_ref, k0_ref, k1_ref, k2_ref, kc_ref, v0_ref, v1_ref, v2_ref, vc_ref, bias_ref, o_ref, *, nblk):
    j = pl.program_id(1)
    lane = lax.broadcasted_iota(jnp.int32, (TOK, LANES), 1)
    left = lane < 64
    ones = _ones_col_tile(TOK, BF16)
    nb = q_ref.shape[0]

    def stacked_q(bi):
        qn = q_ref[bi]
        zero = jnp.zeros_like(qn)
        return jnp.concatenate([jnp.where(left, qn, zero), jnp.where(left, zero, qn)], axis=0)

    def finish(bi, acc):
        out = acc[:, 0:LANES] / acc[:, LANES:LANES + 1]
        o_ref[bi] = jnp.where(left, out[0:TOK], out[TOK:2 * TOK]).astype(o_ref.dtype)

    @pl.when(j >= nblk)
    def _():
        for bi in range(nb):
            v_ext = jnp.concatenate([vc_ref[bi], ones], axis=1)
            finish(bi, _softmax_acc(_qkt(stacked_q(bi), kc_ref[bi]), v_ext))

    @pl.when(j < nblk)
    def _():
        q2 = [stacked_q(bi) for bi in range(nb)]
        m = [None] * nb
        acc = [None] * nb
        for c, (k_ref, v_ref) in enumerate(((k0_ref, v0_ref), (k1_ref, v1_ref), (k2_ref, v2_ref), (kc_ref, vc_ref))):
            bias = bias_ref[0, :, :, c * TOK:(c + 1) * TOK].reshape(2 * TOK, TOK)
            for bi in range(nb):
                v_ext = jnp.concatenate([v_ref[bi], ones], axis=1)
                s = _qkt(q2[bi], k_ref[bi]) + bias
                cm = jnp.max(s, axis=-1, keepdims=True)
                m_new = cm if m[bi] is None else jnp.maximum(m[bi], cm)
                pv = jnp.dot(jnp.exp((s - m_new).astype(BF16)), v_ext, preferred_element_type=F32)
                acc[bi] = pv if m[bi] is None else acc[bi] * jnp.exp(m[bi] - m_new) + pv
                m[bi] = m_new
        for bi in range(nb):
            finish(bi, acc[bi])


def _na_attn(qn, kn, z3, bias, nblk, need_ctx):
    b = z3.shape[0]
    bb = _pick(b, (NA_BB, 1))
    kb = lambda j: jnp.clip(j - 1, 0, nblk - 3)
    nj = nblk + (1 if need_ctx else 0)

    def tspec(col_blk0, tok_idx):
        return pl.BlockSpec((bb, TOK, LANES), lambda hp, j, i: (i, tok_idx(j), col_blk0 + hp))

    var = lambda j: jnp.where(j == 0, 0, jnp.where(j == nblk - 1, 2, 1))
    window = lambda c0: [tspec(c0, (lambda j, d=d: kb(j) + d)) for d in range(3)] + [tspec(c0, lambda j: nblk)]
    in_specs = [tspec(0, lambda j: j)] + window(0) + window(ZC_NAV // LANES)
    in_specs += [pl.BlockSpec((1, 2, TOK, 4 * TOK), lambda hp, j, i: (var(j), hp, 0, 0))]
    return pl.pallas_call(
        functools.partial(_na_kernel, nblk=nblk),
        grid=(NA_HEADS // 2, nj, b // bb),
        in_specs=in_specs,
        out_specs=pl.BlockSpec((bb, TOK, LANES), lambda hp, j, i: (i, j, hp)),
        out_shape=jax.ShapeDtypeStruct((b, nj * TOK, NA_WIDTH), BF16),
        compiler_params=_cparams(("arbitrary", "arbitrary", "arbitrary")),
        name="na_attn",
    )(qn, kn, kn, kn, kn, z3, z3, z3, z3, bias)


def _na_bias_tables(rpb, rows):
    nblk = rows // NA_QROWS
    nh = rpb.shape[0]
    ncol = 2 * NA_WIN_C - 1
    dd = np.arange(ncol)[:, None, None]
    qc1 = np.arange(GRID_W)[None, :, None]
    kc1 = np.arange(GRID_W)[None, None, :]
    onehot = (kc1 - qc1 + NA_WIN_C - 1 == dd).astype(np.float32)
    toep = jnp.einsum('hrd,dqk->hrqk', rpb.astype(F32), onehot, precision=lax.Precision.HIGHEST)
    c0 = np.clip(np.arange(GRID_W) - NA_WIN_C // 2, 0, GRID_W - NA_WIN_C)
    kcs = np.arange(GRID_W)
    valid_c = (kcs[None, :] >= c0[:, None]) & (kcs[None, :] < c0[:, None] + NA_WIN_C)
    tabs = []
    for j in (0, 1, nblk - 1):
        start = int(np.clip(NA_QROWS * j - NA_QROWS, 0, rows - NA_KROWS))
        qr = NA_QROWS * j + np.arange(NA_QROWS)
        kr = start + np.arange(NA_KROWS)
        r0 = np.clip(qr - NA_WIN_R // 2, 0, rows - NA_WIN_R)
        valid_r = (kr[None, :] >= r0[:, None]) & (kr[None, :] < r0[:, None] + NA_WIN_R)
        drow = np.clip(kr[None, :] - qr[:, None] + NA_WIN_R - 1, 0, 2 * NA_WIN_R - 2).reshape(-1)
        sel = jnp.take(toep, drow, axis=1).reshape(nh, NA_QROWS, NA_KROWS, GRID_W, GRID_W)
        sel = jnp.transpose(sel, (0, 1, 3, 2, 4)).reshape(nh, TOK, NA_KROWS * GRID_W)
        valid = (valid_r[:, None, :, None] & valid_c[None, :, None, :]).reshape(TOK, NA_KROWS * GRID_W)
        loc = jnp.where(valid[None], sel, NEG_BIG)
        tabs.append(jnp.concatenate([loc, jnp.zeros((nh, TOK, TOK), F32)], axis=-1))
    return jnp.stack(tabs)


def _ssm_kernel(u_ref, wb_ref, mt_ref, wc_ref, d_ref, ar_ref, ai_ref, y_ref, sh_ref, *, bsz, nchunk, ncc):
    u = u_ref[0]
    sh_ref[...] = jnp.dot(u, wb_ref[0], preferred_element_type=F32)
    a_r = ar_ref[0]
    a_i = ai_ref[0]
    arf, aif, arr, air = a_r[:, 0:LANES], a_i[:, 0:LANES], a_r[:, LANES:], a_i[:, LANES:]
    nlc = nchunk - ncc

    def body(i, carry):
        hre_f, him_f, hre_r, him_r = carry
        cf = jnp.where(i < ncc, nlc + i, i - ncc)
        cr = nchunk - 1 - i
        rf = pl.multiple_of(cf * bsz, bsz)
        rr = pl.multiple_of(cr * bsz, bsz)
        sre_f = sh_ref[pl.ds(rf, bsz), 0:LANES]
        sim_f = sh_ref[pl.ds(rf, bsz), LANES:2 * LANES]
        sre_r = sh_ref[pl.ds(rr, bsz), 2 * LANES:3 * LANES]
        sim_r = sh_ref[pl.ds(rr, bsz), 3 * LANES:4 * LANES]
        sh_ref[pl.ds(rf, bsz), 0:LANES] = hre_f
        sh_ref[pl.ds(rf, bsz), LANES:2 * LANES] = him_f
        sh_ref[pl.ds(rr, bsz), 2 * LANES:3 * LANES] = hre_r
        sh_ref[pl.ds(rr, bsz), 3 * LANES:4 * LANES] = him_r
        return (arf * hre_f - aif * him_f + sre_f, arf * him_f + aif * hre_f + sim_f,
                arr * hre_r - air * him_r + sre_r, arr * him_r + air * hre_r + sim_r)

    zero = jnp.zeros((bsz, LANES), F32)
    lax.fori_loop(0, nchunk, body, (zero, zero, zero, zero))
    y = jnp.dot(u, mt_ref[0], preferred_element_type=F32)
    y = y + jnp.dot(sh_ref[...].astype(BF16), wc_ref[0], preferred_element_type=F32)
    y_ref[0] = (y + d_ref[0] * u.astype(F32)).astype(y_ref.dtype)


def _ssm(xp, wb2, mt2, wc2, d2, ar2, ai2, bsz, nchunk, ncc):
    ns, r, cw = xp.shape
    pw = SSM_PW
    nq = cw // pw
    wspec = lambda shp: pl.BlockSpec((1,) + shp, lambda g, q: (g * nq + q, 0, 0))
    return pl.pallas_call(
        functools.partial(_ssm_kernel, bsz=bsz, nchunk=nchunk, ncc=ncc),
        grid=(ns, nq),
        in_specs=[pl.BlockSpec((1, r, pw), lambda g, q: (g, 0, q)),
                  wspec((pw, pw)), wspec((pw, pw)), wspec((pw, pw)),
                  wspec((1, pw)), wspec((1, 2 * LANES)), wspec((1, 2 * LANES))],
        out_specs=pl.BlockSpec((1, r, pw), lambda g, q: (g, 0, q)),
        out_shape=jax.ShapeDtypeStruct((ns, r, cw), BF16),
        scratch_shapes=[pltpu.VMEM((r, pw), F32)],
        compiler_params=_cparams(("arbitrary", "arbitrary")),
        name="ssm",
    )(xp, wb2, mt2, wc2, d2, ar2, ai2)


def _ssm_perm():
    r = np.arange(SSM_CW)
    s, g, mm = r // LANES, (r % LANES) // SSM_GROUP, r % SSM_GROUP
    dst = g * (SSM_CHUNK * SSM_GROUP) + s * SSM_GROUP + mm
    return (dst[:, None] == np.arange(SSM_CW)[None, :]).astype(np.float32)


def _ssm_params(lam_re, lam_im, log_dt, b_re, b_im, c_re, c_im, d_skip):
    L = SSM_CHUNK
    lam_re, lam_im = lam_re.astype(F32), lam_im.astype(F32)
    dt = jnp.exp(log_dt.astype(F32))[..., None]
    tau = jnp.arange(L + 1, dtype=F32)[:, None, None, None]
    mag = jnp.exp(lam_re * dt * tau)
    pw_re = mag * jnp.cos(lam_im * dt * tau)
    pw_im = mag * jnp.sin(lam_im * dt * tau)
    a_re, a_im = pw_re[1], pw_im[1]
    den = lam_re * lam_re + lam_im * lam_im
    z_re = ((a_re - 1.0) * lam_re + a_im * lam_im) / den
    z_im = (a_im * lam_re - (a_re - 1.0) * lam_im) / den
    bb_re = z_re[..., None] * b_re.astype(F32) - z_im[..., None] * b_im.astype(F32)
    bb_im = z_re[..., None] * b_im.astype(F32) + z_im[..., None] * b_re.astype(F32)
    cr, ci = c_re.astype(F32), c_im.astype(F32)

    ab_re = pw_re[..., None] * bb_re - pw_im[..., None] * bb_im
    ab_im = pw_re[..., None] * bb_im + pw_im[..., None] * bb_re
    ktap = (jnp.einsum('dgnp,tdgpm->tdgnm', cr, ab_re) - jnp.einsum('dgnp,tdgpm->tdgnm', ci, ab_im))[:L]
    i_idx = np.arange(L)[:, None]
    s_idx = np.arange(L)[None, :]
    lag_f = i_idx - s_idx
    lag_r = s_idx - i_idx
    kf = jnp.where((lag_f >= 0)[:, :, None, None, None], ktap[np.clip(lag_f, 0, L - 1), 0], 0.0)
    kr = jnp.where((lag_r >= 0)[:, :, None, None, None], ktap[np.clip(lag_r, 0, L - 1), 1], 0.0)
    w1 = L * SSM_GROUP
    mt = jnp.transpose(kf + kr, (2, 1, 4, 0, 3)).reshape(SSM_GROUPS, w1, w1)

    exp_f = L - 1 - np.arange(L)
    exp_r = np.arange(L)
    inj = lambda ab, d, exps: jnp.transpose(ab[exps, d], (1, 0, 3, 2)).reshape(SSM_GROUPS, w1, SSM_STATE)
    wb_parts = [inj(ab_re, 0, exp_f), inj(ab_im, 0, exp_f), inj(ab_re, 1, exp_r), inj(ab_im, 1, exp_r)]

    def ro(d, exps):
        pr, pi = pw_re[exps, d], pw_im[exps, d]
        e_re = cr[d][None] * pr[:, :, None, :] - ci[d][None] * pi[:, :, None, :]
        e_im = cr[d][None] * pi[:, :, None, :] + ci[d][None] * pr[:, :, None, :]
        f = lambda e: jnp.transpose(e, (1, 3, 0, 2)).reshape(SSM_GROUPS, SSM_STATE, w1)
        return [f(e_re), f(-e_im)]

    wc_parts = ro(0, np.arange(L) + 1) + ro(1, L - np.arange(L))
    npair = SSM_GROUPS // 2

    def pair_diag(x):
        x = x.reshape(npair, 2, x.shape[1], x.shape[2])
        zed = jnp.zeros_like(x[:, 0])
        return jnp.concatenate([jnp.concatenate([x[:, 0], zed], axis=2),
                                jnp.concatenate([zed, x[:, 1]], axis=2)], axis=1)

    wb2 = jnp.concatenate([pair_diag(p) for p in wb_parts], axis=2)
    wc2 = jnp.concatenate([pair_diag(p) for p in wc_parts], axis=1)
    mt2 = pair_diag(mt)
    d2 = jnp.tile(d_skip.astype(F32).reshape(SSM_GROUPS, 1, SSM_GROUP), (1, L, 1)).reshape(npair, 1, 2 * w1)
    packp = lambda x: x.reshape(2, npair, 2 * SSM_STATE)
    ar2 = jnp.concatenate([packp(pw_re[L])[0], packp(pw_re[L])[1]], axis=-1)[:, None, :]
    ai2 = jnp.concatenate([packp(pw_im[L])[0], packp(pw_im[L])[1]], axis=-1)[:, None, :]
    return wb2.astype(BF16), mt2.astype(BF16), wc2.astype(BF16), d2, ar2, ai2


def _glu_kernel(y_ref, w_ref, b_ref, o_ref):
    y = y_ref[0].astype(F32)
    g = 0.5 * y * (1.0 + jnp.tanh(math.sqrt(2.0 / math.pi) * (y + 0.044715 * (y * y * y))))
    t = jnp.dot(g.astype(BF16), w_ref[...], preferred_element_type=F32) + b_ref[...]
    o_ref[0] = (g * jax.nn.sigmoid(t)).astype(o_ref.dtype)


def _glu(y3, w, b, ntile):
    bsz, t, n = y3.shape
    return pl.pallas_call(
        _glu_kernel,
        grid=(bsz, ntile),
        in_specs=[pl.BlockSpec((1, TOK, n), lambda i, j: (i, j, 0)),
                  pl.BlockSpec((n, n), lambda i, j: (0, 0)),
                  pl.BlockSpec((1, n), lambda i, j: (0, 0))],
        out_specs=pl.BlockSpec((1, TOK, n), lambda i, j: (i, j, 0)),
        out_shape=jax.ShapeDtypeStruct((bsz, ntile * TOK, n), BF16),
        compiler_params=_cparams(("arbitrary", "arbitrary")),
        name="glu",
    )(y3, w, b.reshape(1, n))


def _merge_kernel(oa_ref, ob_ref, os_ref, ga_ref, gb_ref, gs_ref, wa_ref, wb_ref, wc_ref, o_ref):
    acc = jax.nn.sigmoid(ga_ref[0].astype(F32)) * jnp.dot(oa_ref[0], wa_ref[...], preferred_element_type=F32)
    acc += jax.nn.sigmoid(gb_ref[0].astype(F32)) * jnp.dot(ob_ref[0], wb_ref[...], preferred_element_type=F32)
    acc += jax.nn.sigmoid(gs_ref[0].astype(F32)) * jnp.dot(os_ref[0], wc_ref[...], preferred_element_type=F32)
    o_ref[0] = acc.astype(o_ref.dtype)


def _merge(oa, ob, os_, z3, wa, wb, wc, ntile):
    bsz, _, k = oa.shape
    d = wa.shape[1]
    ospec = pl.BlockSpec((1, TOK, k), lambda i, j: (i, j, 0))
    gspec = lambda br: pl.BlockSpec((1, TOK, d), lambda i, j: (i, j, ZC_GATE // d + br))
    wspec = pl.BlockSpec((k, d), lambda i, j: (0, 0))
    return pl.pallas_call(
        _merge_kernel,
        grid=(bsz, ntile),
        in_specs=[ospec, ospec, ospec, gspec(0), gspec(1), gspec(2), wspec, wspec, wspec],
        out_specs=pl.BlockSpec((1, TOK, d), lambda i, j: (i, j, 0)),
        out_shape=jax.ShapeDtypeStruct((bsz, ntile * TOK, d), BF16),
        compiler_params=_cparams(("arbitrary", "arbitrary")),
        name="merge",
    )(oa, ob, os_, z3, z3, z3, wa, wb, wc)


def _outproj_kernel(x_ref, mix_ref, w_ref, m_ref, o_ref):
    gate = m_ref[0, 0][2:3]
    o_ref[0] = x_ref[0] + gate * jnp.dot(mix_ref[0], w_ref[...], preferred_element_type=F32)


def _outproj(x, mix3, w, modsel, nlt, ntile):
    b, t, d = x.shape
    return pl.pallas_call(
        _outproj_kernel,
        grid=(b, ntile),
        in_specs=[pl.BlockSpec((1, TOK, d), lambda i, j: (i, j, 0)),
                  pl.BlockSpec((1, TOK, d), lambda i, j: (i, j, 0)),
                  pl.BlockSpec((d, d), lambda i, j: (0, 0)),
                  pl.BlockSpec((1, 1, 6, d), lambda i, j: (i, _type_idx(j, nlt), 0, 0))],
        out_specs=pl.BlockSpec((1, TOK, d), lambda i, j: (i, j, 0)),
        out_shape=jax.ShapeDtypeStruct((b, ntile * TOK, d), F32),
        compiler_params=_cparams(("arbitrary", "arbitrary")),
        name="outproj",
    )(x, mix3, w, modsel)


def _router_kernel(x_ref, g_ref, m_ref, whi_ref, wlo_ref, b_ref, h_ref, meta_ref, tw_ref, cnt_ref, run_ref, *, n_exp):
    first = jnp.logical_and(pl.program_id(0) == 0, pl.program_id(1) == 0)

    @pl.when(first)
    def _():
        run_ref[...] = jnp.zeros_like(run_ref)

    m = m_ref[0, 0]
    h = _rms(x_ref[0]) * g_ref[...] * (1.0 + m[4:5]) + m[3:4]
    h_ref[0] = _pack_bf16_pairs(h)
    h_hi, h_lo = _split_bf16(h)
    logits = (jnp.dot(h_hi, whi_ref[...], preferred_element_type=F32)
              + jnp.dot(h_lo, whi_ref[...], preferred_element_type=F32)
              + jnp.dot(h_hi, wlo_ref[...], preferred_element_type=F32)) + b_ref[...]
    lane = lax.broadcasted_iota(jnp.int32, (TOK, LANES), 1).astype(F32)
    lg = jnp.where(lane < n_exp, logits, -jnp.inf)
    vals, idxs = [], []
    for _ in range(TOP_K):
        mx = jnp.max(lg, axis=-1, keepdims=True)
        idx = jnp.min(jnp.where(lg == mx, lane, float(LANES)), axis=-1, keepdims=True)
        vals.append(mx)
        idxs.append(idx)
        lg = jnp.where(lane == idx, -jnp.inf, lg)
    es = [jnp.exp(v - vals[0]) for v in vals]
    den = es[0] + es[1] + es[2] + es[3]
    onehot = jnp.zeros((TOK, LANES), F32)
    for idx in idxs:
        onehot = onehot + jnp.where(lane == idx, 1.0, 0.0)
    row = lax.broadcasted_iota(jnp.int32, (TOK, TOK), 0)
    col = lax.broadcasted_iota(jnp.int32, (TOK, TOK), 1)
    tri = jnp.where(col < row, 1.0, 0.0).astype(BF16)
    before = jnp.dot(tri, onehot.astype(BF16), preferred_element_type=F32) + run_ref[...]
    meta = jnp.zeros((TOK, LANES), F32)
    tw = jnp.zeros((TOK, LANES), F32)
    for k in range(TOP_K):
        rank = jnp.sum(jnp.where(lane == idxs[k], before, 0.0), axis=-1, keepdims=True)
        meta = meta + jnp.where(lane == float(k), idxs[k], 0.0) + jnp.where(lane == float(TOP_K + k), rank, 0.0)
        tw = tw + jnp.where(lane == float(k), es[k] / den, 0.0)
    meta_ref[0] = meta.astype(jnp.int32)
    tw_ref[0] = tw
    run = run_ref[...] + jnp.sum(onehot, axis=0, keepdims=True)
    run_ref[...] = run
    cnt_ref[...] = run


def _router(x, g, modsel, w, bvec, nlt, n_exp):
    b, t, d = x.shape
    tokspec = lambda last: pl.BlockSpec((1, TOK, last), lambda i, j: (i, j, 0))
    return pl.pallas_call(
        functools.partial(_router_kernel, n_exp=n_exp),
        grid=(b, t // TOK),
        in_specs=[tokspec(d),
                  pl.BlockSpec((1, d), lambda i, j: (0, 0)),
                  pl.BlockSpec((1, 1, 6, d), lambda i, j: (i, _type_idx(j, nlt), 0, 0)),
                  pl.BlockSpec((d, LANES), lambda i, j: (0, 0)),
                  pl.BlockSpec((d, LANES), lambda i, j: (0, 0)),
                  pl.BlockSpec((1, LANES), lambda i, j: (0, 0))],
        out_specs=[tokspec(d // 2), tokspec(LANES), tokspec(LANES), pl.BlockSpec((1, LANES), lambda i, j: (0, 0))],
        out_shape=[jax.ShapeDtypeStruct((b, t, d // 2), jnp.uint32),
                   jax.ShapeDtypeStruct((b, t, LANES), jnp.int32),
                   jax.ShapeDtypeStruct((b, t, LANES), F32),
                   jax.ShapeDtypeStruct((1, LANES), F32)],
        scratch_shapes=[pltpu.VMEM((1, LANES), F32)],
        compiler_params=_cparams(("arbitrary", "arbitrary")),
        name="router",
    )(x, g, modsel, *_split_bf16(w), bvec)


def _dispatch_kernel(pos_ref, h_ref, xs_in_ref, xs_ref, sem):
    del xs_in_ref

    def row_copy(r, k):
        return pltpu.make_async_copy(h_ref.at[pl.ds(r, 1)], xs_ref.at[pl.ds(pos_ref[r * TOP_K + k], 1)], sem)

    def start(r, c):
        for k in range(TOP_K):
            row_copy(r, k).start(priority=k % 2)
        return c

    def wait(r, c):
        for k in range(TOP_K):
            row_copy(r, k).wait()
        return c

    lax.fori_loop(0, TOK, start, 0)
    lax.fori_loop(0, TOK, wait, 0, unroll=DMA_UNROLL)


def _dispatch(pos, h2, xs0):
    m, d = h2.shape
    return pl.pallas_call(
        _dispatch_kernel,
        grid=(m // TOK,),
        in_specs=[pl.BlockSpec((TOK * TOP_K,), lambda i: (i,), memory_space=pltpu.SMEM),
                  pl.BlockSpec((TOK, d), lambda i: (i, 0)),
                  pl.BlockSpec(memory_space=pl.ANY)],
        out_specs=pl.BlockSpec(memory_space=pl.ANY),
        out_shape=jax.ShapeDtypeStruct(xs0.shape, xs0.dtype),
        scratch_shapes=[pltpu.SemaphoreType.DMA(())],
        input_output_aliases={2: 0},
        compiler_params=_cparams(("arbitrary",)),
        name="dispatch",
    )(pos, h2, xs0)


def _experts_kernel(te_ref, tv_ref, x_ref, wgu_ref, bgu_ref, wd_ref, bd_ref, y_ref, *, d_ff, fchunk):
    i = pl.program_id(0)

    @pl.when(tv_ref[i] == 0)
    def _():
        y_ref[...] = jnp.zeros_like(y_ref)

    @pl.when(tv_ref[i] != 0)
    def _():
        x = jnp.concatenate(_unpack_bf16_pairs(x_ref[...]), axis=1).astype(BF16)
        acc = jnp.zeros(x.shape, F32)
        for c in range(d_ff // fchunk):
            lo = c * fchunk
            g = jnp.dot(x, wgu_ref[0, 0, :, lo:lo + fchunk], preferred_element_type=F32) + bgu_ref[0, :, lo:lo + fchunk]
            lin = (jnp.dot(x, wgu_ref[0, 0, :, d_ff + lo:d_ff + lo + fchunk], preferred_element_type=F32)
                   + bgu_ref[0, :, d_ff + lo:d_ff + lo + fchunk])
            g = jnp.minimum(g, SWIGLU_LIMIT)
            lin = jnp.clip(lin, -SWIGLU_LIMIT, SWIGLU_LIMIT)
            act = g * jax.nn.sigmoid(SWIGLU_ALPHA * g) * (lin + 1.0)
            acc = acc + jnp.dot(act.astype(BF16), wd_ref[0, 0, lo:lo + fchunk, :], preferred_element_type=F32)
        y_ref[...] = _pack_bf16_pairs(acc + bd_ref[0])


def _experts(te, tv, xs, wgu, bgu, wd, bd, layer):
    rows, dh = xs.shape
    _, n_exp, d, ff2 = wgu.shape
    d_ff = ff2 // 2
    grid_spec = pltpu.PrefetchScalarGridSpec(
        num_scalar_prefetch=2,
        grid=(rows // TOK,),
        in_specs=[pl.BlockSpec((TOK, dh), lambda i, te, tv: (i, 0)),
                  pl.BlockSpec((1, 1, d, ff2), lambda i, te, tv: (layer, te[i], 0, 0)),
                  pl.BlockSpec((1, 1, ff2), lambda i, te, tv: (te[i], 0, 0)),
                  pl.BlockSpec((1, 1, d_ff, d), lambda i, te, tv: (layer, te[i], 0, 0)),
                  pl.BlockSpec((1, 1, d), lambda i, te, tv: (te[i], 0, 0))],
        out_specs=pl.BlockSpec((TOK, dh), lambda i, te, tv: (i, 0)),
    )
    return pl.pallas_call(
        functools.partial(_experts_kernel, d_ff=d_ff, fchunk=768),
        grid_spec=grid_spec,
        out_shape=jax.ShapeDtypeStruct((rows, dh), jnp.uint32),
        compiler_params=_cparams(("arbitrary",)),
        name="experts",
    )(te, tv, xs, wgu, bgu.reshape(n_exp, 1, ff2), wd, bd.reshape(n_exp, 1, d))


def _combine_kernel(pos_ref, x_ref, tw_ref, m_ref, ys_ref, o_ref, buf_ref, sem):
    def row_copy(r, k):
        return pltpu.make_async_copy(ys_ref.at[pl.ds(pos_ref[r * TOP_K + k], 1)], buf_ref.at[k, pl.ds(r, 1)], sem)

    def start(r, c):
        for k in range(TOP_K):
            row_copy(r, k).start(priority=k % 2)
        return c

    def wait(r, c):
        for k in range(TOP_K):
            row_copy(r, k).wait()
        return c

    lax.fori_loop(0, TOK, start, 0, unroll=DMA_UNROLL)
    lax.fori_loop(0, TOK, wait, 0, unroll=DMA_UNROLL)
    tw = tw_ref[0]
    acc_lo = acc_hi = None
    for k in range(TOP_K):
        lo, hi = _unpack_bf16_pairs(buf_ref[k])
        wk = tw[:, k:k + 1]
        acc_lo = wk * lo if acc_lo is None else acc_lo + wk * lo
        acc_hi = wk * hi if acc_hi is None else acc_hi + wk * hi
    o_ref[0] = x_ref[0] + m_ref[0, 0][5:6] * jnp.concatenate([acc_lo, acc_hi], axis=1)


def _combine(pos, x, tw, modsel, ys, nlt):
    b, t, d = x.shape
    nt = t // TOK
    return pl.pallas_call(
        _combine_kernel,
        grid=(b, nt),
        in_specs=[pl.BlockSpec((TOK * TOP_K,), lambda i, j: (i * nt + j,), memory_space=pltpu.SMEM),
                  pl.BlockSpec((1, TOK, d), lambda i, j: (i, j, 0)),
                  pl.BlockSpec((1, TOK, LANES), lambda i, j: (i, j, 0)),
                  pl.BlockSpec((1, 1, 6, d), lambda i, j: (i, _type_idx(j, nlt), 0, 0)),
                  pl.BlockSpec(memory_space=pl.ANY)],
        out_specs=pl.BlockSpec((1, TOK, d), lambda i, j: (i, j, 0)),
        out_shape=jax.ShapeDtypeStruct((b, t, d), F32),
        scratch_shapes=[pltpu.VMEM((TOP_K, TOK, d // 2), jnp.uint32), pltpu.SemaphoreType.DMA(())],
        compiler_params=_cparams(("arbitrary", "arbitrary")),
        name="combine",
    )(pos, x, tw, modsel, ys)


def _moe(xa, norm_g, modsel, w_router, b_router, wgu, bgu, wd, bd, nlt, layer):
    b, t, d = xa.shape
    m = b * t
    n_exp = w_router.shape[1]
    wr = jnp.zeros((d, LANES), F32).at[:, :n_exp].set(w_router.astype(F32))
    br = jnp.zeros((1, LANES), F32).at[0, :n_exp].set(b_router.astype(F32))
    h2, meta, tw, cnt = _router(xa, norm_g.reshape(1, d), modsel, wr, br, nlt, n_exp)

    counts = cnt[0, :n_exp].astype(jnp.int32)
    padded = ((counts + TOK - 1) // TOK) * TOK
    gend = jnp.cumsum(padded)
    gstart = gend - padded
    meta2 = meta.reshape(m, LANES)
    pos = (gstart[meta2[:, 0:TOP_K]] + meta2[:, TOP_K:2 * TOP_K]).reshape(m * TOP_K)
    ntiles = (m * TOP_K) // TOK + n_exp
    tile_start = jnp.arange(ntiles, dtype=jnp.int32) * TOK
    te = jnp.sum((tile_start[:, None] >= gend[None, :]).astype(jnp.int32), axis=1)
    te = jnp.minimum(te, n_exp - 1)
    tv = (tile_start < gend[-1]).astype(jnp.int32)

    xs = _dispatch(pos, h2.reshape(m, d // 2), jnp.zeros((ntiles * TOK, d // 2), jnp.uint32))
    ys = _experts(te, tv, xs, wgu, bgu, wd, bd, layer)
    return _combine(pos, xa, tw, modsel, ys, nlt)


def _rope_tables(seq, ctx):
    t = jnp.arange(seq, dtype=jnp.int32)
    row = (t // GRID_W).astype(F32)
    col = (t % GRID_W).astype(F32)
    n_freq = MLA_ROPE // 4
    inv_freq = ROPE_BASE ** (-jnp.arange(n_freq, dtype=F32) / n_freq)
    ang = jnp.concatenate([row[:, None] * inv_freq, col[:, None] * inv_freq], axis=-1)
    cos, sin = jnp.cos(ang), jnp.sin(ang)
    cos2 = jnp.tile(jnp.concatenate([cos, cos], axis=-1), (1, 2))
    sin2 = jnp.tile(jnp.concatenate([-sin, sin], axis=-1), (1, 2))
    cos2 = jnp.concatenate([cos2, jnp.ones((ctx, LANES), F32)], axis=0)
    sin2 = jnp.concatenate([sin2, jnp.zeros((ctx, LANES), F32)], axis=0)
    return cos2, sin2


def _pad_cols(w, width):
    return jnp.pad(w, ((0, 0), (0, width - w.shape[1])))


def _layout_w_in(w):
    o_ckv = MLA_Q_RANK
    o_kpe = o_ckv + MLA_KV_RANK
    o_naq = o_kpe + MLA_ROPE
    o_ssm = o_naq + 3 * NA_WIDTH
    o_gate = o_ssm + SSM_WIDTH
    parts = [w[:, o_gate:],
             w[:, o_naq:o_ssm],
             w[:, :o_ckv],
             _pad_cols(w[:, o_kpe:o_naq], ZC_CKV - ZC_KPE),
             w[:, o_ckv:o_kpe],
             w[:, o_ssm:o_gate]]
    out = jnp.concatenate(parts, axis=1)
    assert out.shape[1] == ZW
    return out.astype(BF16)


def kernel(x, c, ctx, c_ctx, w_ada, b_ada, norm1_g, w_in, mla_q_norm, mla_kv_norm, mla_w_uq, mla_w_ukv, mla_q_gain, mla_k_gain, na_q_gain, na_k_gain, na_rpb, ssm_lam_re, ssm_lam_im, ssm_log_dt, ssm_b_re, ssm_b_im, ssm_c_re, ssm_c_im, ssm_d, ssm_w_glu, ssm_b_glu, w_pa, w_pb, w_pc, w_out, norm2_g, w_router, b_router, w_gate_up, b_gate_up, w_down, b_down):
    bsz, seq, d = x.shape
    nctx = ctx.shape[1]
    depth = w_ada.shape[0]
    assert nctx == TOK and seq % (NA_QROWS * GRID_W) == 0 and seq // GRID_W >= NA_KROWS
    nlt = seq // TOK
    t = seq + nctx
    m = bsz * t
    rows = seq // GRID_W
    nblk = rows // NA_QROWS
    nchunk = t // SSM_CHUNK

    cos2, sin2 = _rope_tables(seq, nctx)
    cvec = jnp.zeros((16, d), F32).at[:bsz].set(c.astype(F32)).at[bsz].set(c_ctx.astype(F32))
    xa = jnp.concatenate([x, ctx], axis=1).astype(F32)
    row128 = lambda v: v.astype(F32).reshape(1, LANES)
    perm = jnp.asarray(_ssm_perm(), BF16)
    wgu_all = w_gate_up.astype(BF16)
    wd_all = w_down.astype(BF16)

    for l in range(depth):
        need_ctx = l < depth - 1
        ntile = t // TOK if need_ctx else nlt
        mod = _ada(cvec, w_ada[l], b_ada[l])
        mod_lat = mod[:bsz].reshape(bsz, 1, 6, d)
        mod_ctx = jnp.broadcast_to(mod[bsz].reshape(1, 1, 6, d), (bsz, 1, 6, d))
        modsel = jnp.concatenate([mod_lat, mod_ctx], axis=1)

        h1 = _norm_mod(xa, norm1_g[l], modsel, nlt)
        z = _matmul(h1.reshape(m, d), _layout_w_in(w_in[l]), BF16, "in_proj", ZW // 4)
        z3 = z.reshape(bsz, t, ZW)

        wq = mla_w_uq[l].reshape(MLA_Q_RANK, MLA_HEADS, MLA_QK)
        wq = jnp.concatenate([wq[:, :, :MLA_NOPE].reshape(MLA_Q_RANK, -1),
                              wq[:, :, MLA_NOPE:].reshape(MLA_Q_RANK, -1)], axis=1).astype(BF16)
        qg_r = jnp.tile(mla_q_gain[l][MLA_NOPE:], 2)
        kg_r = jnp.concatenate([mla_k_gain[l][MLA_NOPE:], jnp.zeros((MLA_ROPE,), mla_k_gain.dtype)])
        qf, kf, vf, qn, kn = _projections(
            z3, mla_q_norm[l].astype(F32).reshape(1, -1), wq, row128(mla_q_gain[l][:MLA_NOPE]), row128(qg_r),
            mla_kv_norm[l].astype(F32).reshape(1, -1), mla_w_ukv[l].astype(BF16), row128(mla_k_gain[l][:MLA_NOPE]),
            row128(kg_r), cos2, sin2, row128(jnp.tile(na_q_gain[l], 2)), row128(jnp.tile(na_k_gain[l], 2)))
        o_a = _mla_attn(qf, kf, vf, seq, need_ctx)

        bias = _na_bias_tables(na_rpb[l], rows)
        o_b = _na_attn(qn, kn, z3, bias, nblk, need_ctx)

        wb2, mt2, wc2, d2, ar2, ai2 = _ssm_params(ssm_lam_re[l], ssm_lam_im[l], ssm_log_dt[l], ssm_b_re[l],
                                                   ssm_b_im[l], ssm_c_re[l], ssm_c_im[l], ssm_d[l])
        r = nchunk * bsz
        u = z3[:, :, ZC_SSM:ZC_SSM + SSM_WIDTH].reshape(bsz, nchunk, SSM_CHUNK, SSM_NSUPER, LANES)
        x8 = jnp.transpose(u, (3, 1, 0, 2, 4)).reshape(SSM_NSUPER * r, SSM_CW)
        xp = _matmul(x8, perm, BF16, "ssm_perm_in", SSM_CW)
        yp = _ssm(xp.reshape(SSM_NSUPER, r, SSM_CW), wb2, mt2, wc2, d2, ar2, ai2, bsz, nchunk, nctx // SSM_CHUNK)
        y8 = _matmul(yp.reshape(SSM_NSUPER * r, SSM_CW), perm.T, BF16, "ssm_perm_out", SSM_CW)
        y = jnp.transpose(y8.reshape(SSM_NSUPER, nchunk, bsz, SSM_CHUNK, LANES), (2, 1, 3, 0, 4))
        o_s = _glu(y.reshape(bsz, t, SSM_WIDTH), ssm_w_glu[l].astype(BF16), ssm_b_glu[l].astype(F32), ntile)

        mix = _merge(o_a, o_b, o_s, z3, w_pa[l].astype(BF16), w_pb[l].astype(BF16), w_pc[l].astype(BF16), ntile)
        xa = _outproj(xa, mix, w_out[l].astype(BF16), modsel, nlt, ntile)
        xa = _moe(xa, norm2_g[l].astype(F32), modsel, w_router[l], b_router[l], wgu_all, b_gate_up[l].astype(F32),
                  wd_all, b_down[l].astype(F32), nlt, l)
    return xa[:, :seq].astype(x.dtype)
```

```python
import functools
import math

import numpy as np
import jax
import jax.numpy as jnp
from jax import lax
from jax.experimental import pallas as pl
from jax.experimental.pallas import tpu as pltpu

F32 = jnp.float32
BF16 = jnp.bfloat16

GRID_W = 64
MLA_HEADS = 8
MLA_NOPE = 128
MLA_ROPE = 64
MLA_V = 128
MLA_QK = MLA_NOPE + MLA_ROPE
MLA_Q_RANK = 768
MLA_KV_RANK = 512
NA_HEADS = 16
NA_HEAD_DIM = 64
NA_WIDTH = NA_HEADS * NA_HEAD_DIM
NA_WIN_R = 8
NA_WIN_C = 16
SSM_WIDTH = 1024
SSM_GROUP = 16
SSM_GROUPS = SSM_WIDTH // SSM_GROUP
SSM_STATE = 64
TOP_K = 4
SWIGLU_LIMIT = 7.0
SWIGLU_ALPHA = 1.702
ROPE_BASE = 10000.0
EPS = 1e-6
MLA_SCALE = MLA_QK ** -0.5
NA_SCALE = NA_HEAD_DIM ** -0.5
NEG_BIG = -1e30

LANES = 128
TOK = 256
MLA_KPAD = 256
MLA_VPAD = 256
MLA_TQ = 1024
MLA_TK = 512
VMEM_LIMIT = 56 * 1024 * 1024
DMA_UNROLL = 4

ZC_GATE = 0
ZC_NAQ = 6144
ZC_NAK = 7168
ZC_NAV = 8192
ZC_CQ = 9216
ZC_KPE = 9984
ZC_CKV = 10240
ZC_SSM = 10752
ZW = 11776

SSM_CHUNK = 16
SSM_NSUPER = SSM_WIDTH // LANES
SSM_CW = SSM_CHUNK * LANES
SSM_PW = 2 * SSM_CHUNK * SSM_GROUP
NA_QROWS = 4
NA_KROWS = 12
NA_BB = 2


def _cparams(sem, vmem=VMEM_LIMIT):
    return pltpu.CompilerParams(dimension_semantics=sem, vmem_limit_bytes=vmem)


def _rms(x, eps=EPS):
    return x * lax.rsqrt(jnp.mean(x * x, axis=-1, keepdims=True) + eps)


def _split_bf16(x):
    hi = x.astype(BF16)
    return hi, (x - hi.astype(F32)).astype(BF16)


def _seg64_rms(t):
    r = lax.broadcasted_iota(jnp.int32, (LANES, LANES), 0)
    c = lax.broadcasted_iota(jnp.int32, (LANES, LANES), 1)
    same_half = jnp.where((r < 64) == (c < 64), 1.0, 0.0).astype(BF16)
    hi, lo = _split_bf16(t * t)
    ssum = (jnp.dot(hi, same_half, preferred_element_type=F32) + jnp.dot(lo, same_half, preferred_element_type=F32))
    return t * lax.rsqrt(ssum * (1.0 / 64.0) + EPS)


def _rope64(t, cos2, sin2):
    lane = lax.broadcasted_iota(jnp.int32, t.shape, 1)
    first = (lane % 64) < 32
    swapped = jnp.where(first, pltpu.roll(t, 96, axis=1), pltpu.roll(t, 32, axis=1))
    return t * cos2 + swapped * sin2


def _ones_col_tile(rows, dtype):
    lane = lax.broadcasted_iota(jnp.int32, (rows, LANES), 1)
    return jnp.where(lane == 0, 1.0, 0.0).astype(dtype)


def _qkt(q, k):
    return lax.dot_general(q, k, (((1,), (1,)), ((), ())), preferred_element_type=F32)


def _softmax_acc(s, v_ext):
    m = jnp.max(s, axis=-1, keepdims=True)
    return jnp.dot(jnp.exp((s - m).astype(BF16)), v_ext, preferred_element_type=F32)


def _softmax_pv(s, v_ext):
    oe = _softmax_acc(s, v_ext)
    return oe[:, 0:LANES] / oe[:, LANES:LANES + 1]


def _pack_bf16_pairs(x):
    n = x.shape[1] // 2
    r = x.astype(BF16).astype(F32)
    lo = lax.bitcast_convert_type(r[:, :n], jnp.uint32) >> 16
    hi = lax.bitcast_convert_type(r[:, n:], jnp.uint32) & jnp.uint32(0xFFFF0000)
    return hi | lo


def _unpack_bf16_pairs(w):
    lo = lax.bitcast_convert_type(w << 16, F32)
    hi = lax.bitcast_convert_type(w & jnp.uint32(0xFFFF0000), F32)
    return lo, hi


def _type_idx(j, nlt):
    return jnp.where(j >= nlt, 1, 0)


def _ada_kernel(c_ref, w_ref, b_ref, o_ref):
    c = c_ref[...]
    a = (c * jax.nn.sigmoid(c)).astype(BF16)
    o_ref[...] = jnp.dot(a, w_ref[...].astype(BF16), preferred_element_type=F32) + b_ref[...]


def _ada(cvec, w, b):
    rows, d = cvec.shape
    n = w.shape[1]
    tn = 1024
    return pl.pallas_call(
        _ada_kernel,
        grid=(n // tn,),
        in_specs=[pl.BlockSpec((rows, d), lambda j: (0, 0)),
                  pl.BlockSpec((d, tn), lambda j: (0, j)),
                  pl.BlockSpec((1, tn), lambda j: (0, j))],
        out_specs=pl.BlockSpec((rows, tn), lambda j: (0, j)),
        out_shape=jax.ShapeDtypeStruct((rows, n), F32),
        compiler_params=_cparams(("arbitrary",)),
        name="ada",
    )(cvec, w, b.reshape(1, n))


def _norm_mod_kernel(x_ref, g_ref, m_ref, o_ref):
    m = m_ref[0, 0]
    y = _rms(x_ref[0]) * g_ref[...]
    o_ref[0] = (y * (1.0 + m[1:2]) + m[0:1]).astype(o_ref.dtype)


def _norm_mod(x, g, modsel, nlt):
    b, t, d = x.shape
    return pl.pallas_call(
        _norm_mod_kernel,
        grid=(b, t // TOK),
        in_specs=[pl.BlockSpec((1, TOK, d), lambda i, j: (i, j, 0)),
                  pl.BlockSpec((1, d), lambda i, j: (0, 0)),
                  pl.BlockSpec((1, 1, 6, d), lambda i, j: (i, _type_idx(j, nlt), 0, 0))],
        out_specs=pl.BlockSpec((1, TOK, d), lambda i, j: (i, j, 0)),
        out_shape=jax.ShapeDtypeStruct((b, t, d), BF16),
        compiler_params=_cparams(("arbitrary", "arbitrary")),
        name="norm_mod",
    )(x, g.reshape(1, d), modsel)


def _mm_kernel(a_ref, w_ref, o_ref):
    o_ref[...] = jnp.dot(a_ref[...], w_ref[...], preferred_element_type=F32).astype(o_ref.dtype)


def _pick(n, cands):
    for c in cands:
        if n % c == 0:
            return c
    return n


def _matmul(a, w, out_dtype, name, tn):
    m, k = a.shape
    n = w.shape[1]
    tm = _pick(m, (512, 256, 128))
    return pl.pallas_call(
        _mm_kernel,
        grid=(n // tn, m // tm),
        in_specs=[pl.BlockSpec((tm, k), lambda j, i: (i, 0)),
                  pl.BlockSpec((k, tn), lambda j, i: (0, j))],
        out_specs=pl.BlockSpec((tm, tn), lambda j, i: (i, j)),
        out_shape=jax.ShapeDtypeStruct((m, n), out_dtype),
        compiler_params=_cparams(("arbitrary", "arbitrary")),
        name=name,
    )(a, w)


def _mla_q_part(z_ref, g_ref, w_ref, gn_ref, gr_ref, cos, sin, o_ref):
    cq = z_ref[0].astype(F32)
    xn = (_rms(cq) * g_ref[...]).astype(BF16)
    q = jnp.dot(xn, w_ref[...], preferred_element_type=F32)
    lane = lax.broadcasted_iota(jnp.int32, (TOK, LANES), 1)
    left = lane < 64
    for h in range(MLA_HEADS):
        qn = q[:, h * MLA_NOPE:(h + 1) * MLA_NOPE]
        o_ref[0, h, :, 0:LANES] = (_rms(qn) * (gn_ref[...] * MLA_SCALE)).astype(o_ref.dtype)
    base = MLA_HEADS * MLA_NOPE
    for hp in range(MLA_HEADS // 2):
        t = q[:, base + hp * LANES: base + (hp + 1) * LANES]
        t = _seg64_rms(t) * (gr_ref[...] * MLA_SCALE)
        r = _rope64(t, cos, sin)
        o_ref[0, 2 * hp, :, LANES:2 * LANES] = jnp.where(left, r, 0.0).astype(o_ref.dtype)
        o_ref[0, 2 * hp + 1, :, LANES:2 * LANES] = jnp.where(left, pltpu.roll(r, 64, axis=1), 0.0).astype(o_ref.dtype)


def _mla_kv_part(z_ref, zk_ref, g_ref, w_ref, gn_ref, gr_ref, cos, sin, k_ref, v_ref):
    ckv = z_ref[0].astype(F32)
    xn = (_rms(ckv) * g_ref[...]).astype(BF16)
    kv = jnp.dot(xn, w_ref[...], preferred_element_type=F32)
    lane = lax.broadcasted_iota(jnp.int32, (TOK, LANES), 1)
    left = lane < 64
    t = jnp.where(left, zk_ref[0].astype(F32), 0.0)
    ms = jnp.sum(t * t, axis=-1, keepdims=True) * (1.0 / MLA_ROPE)
    kr = t * lax.rsqrt(ms + EPS) * gr_ref[...]
    kr = jnp.where(left, _rope64(kr, cos, sin), 0.0).astype(k_ref.dtype)
    ones = _ones_col_tile(TOK, v_ref.dtype)
    hw = MLA_NOPE + MLA_V
    for h in range(MLA_HEADS):
        kn = kv[:, h * hw: h * hw + MLA_NOPE]
        k_ref[0, h, :, 0:LANES] = (_rms(kn) * gn_ref[...]).astype(k_ref.dtype)
        k_ref[0, h, :, LANES:2 * LANES] = kr
        v_ref[0, h, :, 0:LANES] = kv[:, h * hw + MLA_NOPE:(h + 1) * hw].astype(v_ref.dtype)
        v_ref[0, h, :, LANES:2 * LANES] = ones


def _na_norm_part(q_ref, k_ref, gq_ref, gk_ref, qn_ref, kn_ref):
    for c in range(q_ref.shape[2] // LANES):
        sl = slice(c * LANES, (c + 1) * LANES)
        qn_ref[0, :, sl] = (_seg64_rms(q_ref[0, :, sl].astype(F32)) * (gq_ref[...] * NA_SCALE)).astype(qn_ref.dtype)
        kn_ref[0, :, sl] = (_seg64_rms(k_ref[0, :, sl].astype(F32)) * gk_ref[...]).astype(kn_ref.dtype)


def _proj_kernel(zq_ref, zkv_ref, zk_ref, znq_ref, znk_ref, gq_ref, wq_ref, qgn_ref, qgr_ref, gkv_ref, wkv_ref,
                 kgn_ref, kgr_ref, cos_ref, sin_ref, ngq_ref, ngk_ref, q_ref, k_ref, v_ref, qn_ref, kn_ref):
    cos, sin = cos_ref[...], sin_ref[...]
    _mla_q_part(zq_ref, gq_ref, wq_ref, qgn_ref, qgr_ref, cos, sin, q_ref)
    _mla_kv_part(zkv_ref, zk_ref, gkv_ref, wkv_ref, kgn_ref, kgr_ref, cos, sin, k_ref, v_ref)
    _na_norm_part(znq_ref, znk_ref, ngq_ref, ngk_ref, qn_ref, kn_ref)


def _projections(z3, gq, wq, qgn, qgr2, gkv, wkv, kgn, kgr2, cos2, sin2, ngq2, ngk2):
    b, t, _ = z3.shape
    zspec = lambda w, col0: pl.BlockSpec((1, TOK, w), lambda i, j: (i, j, col0 // w))
    const = lambda a: pl.BlockSpec(a.shape, lambda i, j: (0,) * a.ndim)
    tab = pl.BlockSpec((TOK, LANES), lambda i, j: (j, 0))
    head_spec = lambda w: pl.BlockSpec((1, MLA_HEADS, TOK, w), lambda i, j: (i, 0, j, 0))
    na_spec = pl.BlockSpec((1, TOK, NA_WIDTH), lambda i, j: (i, j, 0))
    head_shape = lambda w: jax.ShapeDtypeStruct((b, MLA_HEADS, t, w), BF16)
    na_shape = jax.ShapeDtypeStruct((b, t, NA_WIDTH), BF16)
    return pl.pallas_call(
        _proj_kernel,
        grid=(b, t // TOK),
        in_specs=[zspec(MLA_Q_RANK, ZC_CQ), zspec(MLA_KV_RANK, ZC_CKV), zspec(LANES, ZC_KPE),
                  zspec(NA_WIDTH, ZC_NAQ), zspec(NA_WIDTH, ZC_NAK),
                  const(gq), const(wq), const(qgn), const(qgr2), const(gkv), const(wkv), const(kgn), const(kgr2),
                  tab, tab, const(ngq2), const(ngk2)],
        out_specs=[head_spec(MLA_KPAD), head_spec(MLA_KPAD), head_spec(MLA_VPAD), na_spec, na_spec],
        out_shape=[head_shape(MLA_KPAD), head_shape(MLA_KPAD), head_shape(MLA_VPAD), na_shape, na_shape],
        compiler_params=_cparams(("arbitrary", "arbitrary")),
        name="projections",
    )(z3, z3, z3, z3, z3, gq, wq, qgn, qgr2, gkv, wkv, kgn, kgr2, cos2, sin2, ngq2, ngk2)


def _mla_attn_kernel(q_ref, k_ref, v_ref, o_ref, *, nk, tk):
    q = q_ref[0, 0]
    m = acc = None
    for c0 in range(0, nk, tk):
        sz = min(tk, nk - c0)
        s = _qkt(q, k_ref[0, 0, c0:c0 + sz, :])
        cm = jnp.max(s, axis=-1, keepdims=True)
        m_new = cm if m is None else jnp.maximum(m, cm)
        pv = jnp.dot(jnp.exp((s - m_new).astype(BF16)), v_ref[0, 0, c0:c0 + sz, :], preferred_element_type=F32)
        acc = pv if m is None else acc * jnp.exp(m - m_new) + pv
        m = m_new
    o_ref[0] = (acc[:, 0:LANES] / acc[:, LANES:LANES + 1]).astype(o_ref.dtype)


def _mla_attn_ctx_kernel(q_ref, k_ref, v_ref, o_ref):
    o_ref[0] = _softmax_pv(_qkt(q_ref[0, 0], k_ref[0, 0]), v_ref[0, 0]).astype(o_ref.dtype)


def _mla_attn(q, k, v, seq, need_ctx):
    b, h, t, _ = k.shape
    tq = _pick(seq, (MLA_TQ, TOK))
    sem = ("arbitrary", "arbitrary", "arbitrary")
    o = pl.pallas_call(
        functools.partial(_mla_attn_kernel, nk=t, tk=MLA_TK),
        grid=(b, h, seq // tq),
        in_specs=[pl.BlockSpec((1, 1, tq, MLA_KPAD), lambda i, hh, j: (i, hh, j, 0)),
                  pl.BlockSpec((1, 1, t, MLA_KPAD), lambda i, hh, j: (i, hh, 0, 0)),
                  pl.BlockSpec((1, 1, t, MLA_VPAD), lambda i, hh, j: (i, hh, 0, 0))],
        out_specs=pl.BlockSpec((1, tq, MLA_V), lambda i, hh, j: (i, j, hh)),
        out_shape=jax.ShapeDtypeStruct((b, seq, h * MLA_V), BF16),
        compiler_params=_cparams(sem),
        name="mla_attn",
    )(q, k, v)
    if not need_ctx:
        return o
    nctx = t - seq
    ctx_spec = lambda w: pl.BlockSpec((1, 1, nctx, w), lambda i, hh, j: (i, hh, seq // nctx, 0))
    o_c = pl.pallas_call(
        _mla_attn_ctx_kernel,
        grid=(b, h, 1),
        in_specs=[ctx_spec(MLA_KPAD), ctx_spec(MLA_KPAD), ctx_spec(MLA_VPAD)],
        out_specs=pl.BlockSpec((1, nctx, MLA_V), lambda i, hh, j: (i, 0, hh)),
        out_shape=jax.ShapeDtypeStruct((b, nctx, h * MLA_V), BF16),
        compiler_params=_cparams(sem),
        name="mla_attn_ctx",
    )(q, k, v)
    return jnp.concatenate([o, o_c], axis=1)


def _na_kernel(q_ref, k0_ref, k1_ref, k2_ref, kc_ref, v0_ref, v1_ref, v2_ref, vc_ref, bias_ref, o_ref, *, nblk):
    j = pl.program_id(1)
    lane = lax.broadcasted_iota(jnp.int32, (TOK, LANES), 1)
    left = lane < 64
    ones = _ones_col_tile(TOK, BF16)
    nb = q_ref.shape[0]

    def stacked_q(bi):
        qn = q_ref[bi]
        zero = jnp.zeros_like(qn)
        return jnp.concatenate([jnp.where(left, qn, zero), jnp.where(left, zero, qn)], axis=0)

    def finish(bi, acc):
        out = acc[:, 0:LANES] / acc[:, LANES:LANES + 1]
        o_ref[bi] = jnp.where(left, out[0:TOK], out[TOK:2 * TOK]).astype(o_ref.dtype)

    @pl.when(j >= nblk)
    def _():
        for bi in range(nb):
            v_ext = jnp.concatenate([vc_ref[bi], ones], axis=1)
            finish(bi, _softmax_acc(_qkt(stacked_q(bi), kc_ref[bi]), v_ext))

    @pl.when(j < nblk)
    def _():
        q2 = [stacked_q(bi) for bi in range(nb)]
        m = [None] * nb
        acc = [None] * nb
        for c, (k_ref, v_ref) in enumerate(((k0_ref, v0_ref), (k1_ref, v1_ref), (k2_ref, v2_ref), (kc_ref, vc_ref))):
            bias = bias_ref[0, :, :, c * TOK:(c + 1) * TOK].reshape(2 * TOK, TOK)
            for bi in range(nb):
                v_ext = jnp.concatenate([v_ref[bi], ones], axis=1)
                s = _qkt(q2[bi], k_ref[bi]) + bias
                cm = jnp.max(s, axis=-1, keepdims=True)
                m_new = cm if m[bi] is None else jnp.maximum(m[bi], cm)
                pv = jnp.dot(jnp.exp((s - m_new).astype(BF16)), v_ext, preferred_element_type=F32)
                acc[bi] = pv if m[bi] is None else acc[bi] * jnp.exp(m[bi] - m_new) + pv
                m[bi] = m_new
        for bi in range(nb):
            finish(bi, acc[bi])


def _na_attn(qn, kn, z3, bias, nblk, need_ctx):
    b = z3.shape[0]
    bb = _pick(b, (NA_BB, 1))
    kb = lambda j: jnp.clip(j - 1, 0, nblk - 3)
    nj = nblk + (1 if need_ctx else 0)

    def tspec(col_blk0, tok_idx):
        return pl.BlockSpec((bb, TOK, LANES), lambda hp, j, i: (i, tok_idx(j), col_blk0 + hp))

    var = lambda j: jnp.where(j == 0, 0, jnp.where(j == nblk - 1, 2, 1))
    window = lambda c0: [tspec(c0, (lambda j, d=d: kb(j) + d)) for d in range(3)] + [tspec(c0, lambda j: nblk)]
    in_specs = [tspec(0, lambda j: j)] + window(0) + window(ZC_NAV // LANES)
    in_specs += [pl.BlockSpec((1, 2, TOK, 4 * TOK), lambda hp, j, i: (var(j), hp, 0, 0))]
    return pl.pallas_call(
        functools.partial(_na_kernel, nblk=nblk),
        grid=(NA_HEADS // 2, nj, b // bb),
        in_specs=in_specs,
        out_specs=pl.BlockSpec((bb, TOK, LANES), lambda hp, j, i: (i, j, hp)),
        out_shape=jax.ShapeDtypeStruct((b, nj * TOK, NA_WIDTH), BF16),
        compiler_params=_cparams(("arbitrary", "arbitrary", "arbitrary")),
        name="na_attn",
    )(qn, kn, kn, kn, kn, z3, z3, z3, z3, bias)


def _na_bias_tables(rpb, rows):
    nblk = rows // NA_QROWS
    nh = rpb.shape[0]
    ncol = 2 * NA_WIN_C - 1
    dd = np.arange(ncol)[:, None, None]
    qc1 = np.arange(GRID_W)[None, :, None]
    kc1 = np.arange(GRID_W)[None, None, :]
    onehot = (kc1 - qc1 + NA_WIN_C - 1 == dd).astype(np.float32)
    toep = jnp.einsum('hrd,dqk->hrqk', rpb.astype(F32), onehot, precision=lax.Precision.HIGHEST)
    c0 = np.clip(np.arange(GRID_W) - NA_WIN_C // 2, 0, GRID_W - NA_WIN_C)
    kcs = np.arange(GRID_W)
    valid_c = (kcs[None, :] >= c0[:, None]) & (kcs[None, :] < c0[:, None] + NA_WIN_C)
    tabs = []
    for j in (0, 1, nblk - 1):
        start = int(np.clip(NA_QROWS * j - NA_QROWS, 0, rows - NA_KROWS))
        qr = NA_QROWS * j + np.arange(NA_QROWS)
        kr = start + np.arange(NA_KROWS)
        r0 = np.clip(qr - NA_WIN_R // 2, 0, rows - NA_WIN_R)
        valid_r = (kr[None, :] >= r0[:, None]) & (kr[None, :] < r0[:, None] + NA_WIN_R)
        drow = np.clip(kr[None, :] - qr[:, None] + NA_WIN_R - 1, 0, 2 * NA_WIN_R - 2).reshape(-1)
        sel = jnp.take(toep, drow, axis=1).reshape(nh, NA_QROWS, NA_KROWS, GRID_W, GRID_W)
        sel = jnp.transpose(sel, (0, 1, 3, 2, 4)).reshape(nh, TOK, NA_KROWS * GRID_W)
        valid = (valid_r[:, None, :, None] & valid_c[None, :, None, :]).reshape(TOK, NA_KROWS * GRID_W)
        loc = jnp.where(valid[None], sel, NEG_BIG)
        tabs.append(jnp.concatenate([loc, jnp.zeros((nh, TOK, TOK), F32)], axis=-1))
    return jnp.stack(tabs)


def _ssm_kernel(u_ref, wb_ref, mt_ref, wc_ref, d_ref, ar_ref, ai_ref, y_ref, sh_ref, *, bsz, nchunk, ncc):
    u = u_ref[0]
    sh_ref[...] = jnp.dot(u, wb_ref[0], preferred_element_type=F32)
    a_r = ar_ref[0]
    a_i = ai_ref[0]
    arf, aif, arr, air = a_r[:, 0:LANES], a_i[:, 0:LANES], a_r[:, LANES:], a_i[:, LANES:]
    nlc = nchunk - ncc

    def body(i, carry):
        hre_f, him_f, hre_r, him_r = carry
        cf = jnp.where(i < ncc, nlc + i, i - ncc)
        cr = nchunk - 1 - i
        rf = pl.multiple_of(cf * bsz, bsz)
        rr = pl.multiple_of(cr * bsz, bsz)
        sre_f = sh_ref[pl.ds(rf, bsz), 0:LANES]
        sim_f = sh_ref[pl.ds(rf, bsz), LANES:2 * LANES]
        sre_r = sh_ref[pl.ds(rr, bsz), 2 * LANES:3 * LANES]
        sim_r = sh_ref[pl.ds(rr, bsz), 3 * LANES:4 * LANES]
        sh_ref[pl.ds(rf, bsz), 0:LANES] = hre_f
        sh_ref[pl.ds(rf, bsz), LANES:2 * LANES] = him_f
        sh_ref[pl.ds(rr, bsz), 2 * LANES:3 * LANES] = hre_r
        sh_ref[pl.ds(rr, bsz), 3 * LANES:4 * LANES] = him_r
        return (arf * hre_f - aif * him_f + sre_f, arf * him_f + aif * hre_f + sim_f,
                arr * hre_r - air * him_r + sre_r, arr * him_r + air * hre_r + sim_r)

    zero = jnp.zeros((bsz, LANES), F32)
    lax.fori_loop(0, nchunk, body, (zero, zero, zero, zero))
    y = jnp.dot(u, mt_ref[0], preferred_element_type=F32)
    y = y + jnp.dot(sh_ref[...].astype(BF16), wc_ref[0], preferred_element_type=F32)
    y_ref[0] = (y + d_ref[0] * u.astype(F32)).astype(y_ref.dtype)


def _ssm(xp, wb2, mt2, wc2, d2, ar2, ai2, bsz, nchunk, ncc):
    ns, r, cw = xp.shape
    pw = SSM_PW
    nq = cw // pw
    wspec = lambda shp: pl.BlockSpec((1,) + shp, lambda g, q: (g * nq + q, 0, 0))
    return pl.pallas_call(
        functools.partial(_ssm_kernel, bsz=bsz, nchunk=nchunk, ncc=ncc),
        grid=(ns, nq),
        in_specs=[pl.BlockSpec((1, r, pw), lambda g, q: (g, 0, q)),
                  wspec((pw, pw)), wspec((pw, pw)), wspec((pw, pw)),
                  wspec((1, pw)), wspec((1, 2 * LANES)), wspec((1, 2 * LANES))],
        out_specs=pl.BlockSpec((1, r, pw), lambda g, q: (g, 0, q)),
        out_shape=jax.ShapeDtypeStruct((ns, r, cw), BF16),
        scratch_shapes=[pltpu.VMEM((r, pw), F32)],
        compiler_params=_cparams(("arbitrary", "arbitrary")),
        name="ssm",
    )(xp, wb2, mt2, wc2, d2, ar2, ai2)


def _ssm_perm():
    r = np.arange(SSM_CW)
    s, g, mm = r // LANES, (r % LANES) // SSM_GROUP, r % SSM_GROUP
    dst = g * (SSM_CHUNK * SSM_GROUP) + s * SSM_GROUP + mm
    return (dst[:, None] == np.arange(SSM_CW)[None, :]).astype(np.float32)


def _ssm_params(lam_re, lam_im, log_dt, b_re, b_im, c_re, c_im, d_skip):
    L = SSM_CHUNK
    lam_re, lam_im = lam_re.astype(F32), lam_im.astype(F32)
    dt = jnp.exp(log_dt.astype(F32))[..., None]
    tau = jnp.arange(L + 1, dtype=F32)[:, None, None, None]
    mag = jnp.exp(lam_re * dt * tau)
    pw_re = mag * jnp.cos(lam_im * dt * tau)
    pw_im = mag * jnp.sin(lam_im * dt * tau)
    a_re, a_im = pw_re[1], pw_im[1]
    den = lam_re * lam_re + lam_im * lam_im
    z_re = ((a_re - 1.0) * lam_re + a_im * lam_im) / den
    z_im = (a_im * lam_re - (a_re - 1.0) * lam_im) / den
    bb_re = z_re[..., None] * b_re.astype(F32) - z_im[..., None] * b_im.astype(F32)
    bb_im = z_re[..., None] * b_im.astype(F32) + z_im[..., None] * b_re.astype(F32)
    cr, ci = c_re.astype(F32), c_im.astype(F32)

    ab_re = pw_re[..., None] * bb_re - pw_im[..., None] * bb_im
    ab_im = pw_re[..., None] * bb_im + pw_im[..., None] * bb_re
    ktap = (jnp.einsum('dgnp,tdgpm->tdgnm', cr, ab_re) - jnp.einsum('dgnp,tdgpm->tdgnm', ci, ab_im))[:L]
    i_idx = np.arange(L)[:, None]
    s_idx = np.arange(L)[None, :]
    lag_f = i_idx - s_idx
    lag_r = s_idx - i_idx
    kf = jnp.where((lag_f >= 0)[:, :, None, None, None], ktap[np.clip(lag_f, 0, L - 1), 0], 0.0)
    kr = jnp.where((lag_r >= 0)[:, :, None, None, None], ktap[np.clip(lag_r, 0, L - 1), 1], 0.0)
    w1 = L * SSM_GROUP
    mt = jnp.transpose(kf + kr, (2, 1, 4, 0, 3)).reshape(SSM_GROUPS, w1, w1)

    exp_f = L - 1 - np.arange(L)
    exp_r = np.arange(L)
    inj = lambda ab, d, exps: jnp.transpose(ab[exps, d], (1, 0, 3, 2)).reshape(SSM_GROUPS, w1, SSM_STATE)
    wb_parts = [inj(ab_re, 0, exp_f), inj(ab_im, 0, exp_f), inj(ab_re, 1, exp_r), inj(ab_im, 1, exp_r)]

    def ro(d, exps):
        pr, pi = pw_re[exps, d], pw_im[exps, d]
        e_re = cr[d][None] * pr[:, :, None, :] - ci[d][None] * pi[:, :, None, :]
        e_im = cr[d][None] * pi[:, :, None, :] + ci[d][None] * pr[:, :, None, :]
        f = lambda e: jnp.transpose(e, (1, 3, 0, 2)).reshape(SSM_GROUPS, SSM_STATE, w1)
        return [f(e_re), f(-e_im)]

    wc_parts = ro(0, np.arange(L) + 1) + ro(1, L - np.arange(L))
    npair = SSM_GROUPS // 2

    def pair_diag(x):
        x = x.reshape(npair, 2, x.shape[1], x.shape[2])
        zed = jnp.zeros_like(x[:, 0])
        return jnp.concatenate([jnp.concatenate([x[:, 0], zed], axis=2),
                                jnp.concatenate([zed, x[:, 1]], axis=2)], axis=1)

    wb2 = jnp.concatenate([pair_diag(p) for p in wb_parts], axis=2)
    wc2 = jnp.concatenate([pair_diag(p) for p in wc_parts], axis=1)
    mt2 = pair_diag(mt)
    d2 = jnp.tile(d_skip.astype(F32).reshape(SSM_GROUPS, 1, SSM_GROUP), (1, L, 1)).reshape(npair, 1, 2 * w1)
    packp = lambda x: x.reshape(2, npair, 2 * SSM_STATE)
    ar2 = jnp.concatenate([packp(pw_re[L])[0], packp(pw_re[L])[1]], axis=-1)[:, None, :]
    ai2 = jnp.concatenate([packp(pw_im[L])[0], packp(pw_im[L])[1]], axis=-1)[:, None, :]
    return wb2.astype(BF16), mt2.astype(BF16), wc2.astype(BF16), d2, ar2, ai2


def _glu_kernel(y_ref, w_ref, b_ref, o_ref):
    y = y_ref[0].astype(F32)
    g = 0.5 * y * (1.0 + jnp.tanh(math.sqrt(2.0 / math.pi) * (y + 0.044715 * (y * y * y))))
    t = jnp.dot(g.astype(BF16), w_ref[...], preferred_element_type=F32) + b_ref[...]
    o_ref[0] = (g * jax.nn.sigmoid(t)).astype(o_ref.dtype)


def _glu(y3, w, b, ntile):
    bsz, t, n = y3.shape
    return pl.pallas_call(
        _glu_kernel,
        grid=(bsz, ntile),
        in_specs=[pl.BlockSpec((1, TOK, n), lambda i, j: (i, j, 0)),
                  pl.BlockSpec((n, n), lambda i, j: (0, 0)),
                  pl.BlockSpec((1, n), lambda i, j: (0, 0))],
        out_specs=pl.BlockSpec((1, TOK, n), lambda i, j: (i, j, 0)),
        out_shape=jax.ShapeDtypeStruct((bsz, ntile * TOK, n), BF16),
        compiler_params=_cparams(("arbitrary", "arbitrary")),
        name="glu",
    )(y3, w, b.reshape(1, n))


def _route(h, whi_ref, wlo_ref, b_ref, h_ref, meta_ref, tw_ref, cnt_ref, run_ref, n_exp):
    h_ref[0] = _pack_bf16_pairs(h)
    h_hi, h_lo = _split_bf16(h)
    logits = (jnp.dot(h_hi, whi_ref[...], preferred_element_type=F32)
              + jnp.dot(h_lo, whi_ref[...], preferred_element_type=F32)
              + jnp.dot(h_hi, wlo_ref[...], preferred_element_type=F32)) + b_ref[...]
    lane = lax.broadcasted_iota(jnp.int32, (TOK, LANES), 1).astype(F32)
    lg = jnp.where(lane < n_exp, logits, -jnp.inf)
    vals, idxs = [], []
    for _ in range(TOP_K):
        mx = jnp.max(lg, axis=-1, keepdims=True)
        idx = jnp.min(jnp.where(lg == mx, lane, float(LANES)), axis=-1, keepdims=True)
        vals.append(mx)
        idxs.append(idx)
        lg = jnp.where(lane == idx, -jnp.inf, lg)
    es = [jnp.exp(v - vals[0]) for v in vals]
    den = es[0] + es[1] + es[2] + es[3]
    onehot = jnp.zeros((TOK, LANES), F32)
    for idx in idxs:
        onehot = onehot + jnp.where(lane == idx, 1.0, 0.0)
    row = lax.broadcasted_iota(jnp.int32, (TOK, TOK), 0)
    col = lax.broadcasted_iota(jnp.int32, (TOK, TOK), 1)
    tri = jnp.where(col < row, 1.0, 0.0).astype(BF16)
    before = jnp.dot(tri, onehot.astype(BF16), preferred_element_type=F32) + run_ref[...]
    meta = jnp.zeros((TOK, LANES), F32)
    tw = jnp.zeros((TOK, LANES), F32)
    for k in range(TOP_K):
        rank = jnp.sum(jnp.where(lane == idxs[k], before, 0.0), axis=-1, keepdims=True)
        meta = meta + jnp.where(lane == float(k), idxs[k], 0.0) + jnp.where(lane == float(TOP_K + k), rank, 0.0)
        tw = tw + jnp.where(lane == float(k), es[k] / den, 0.0)
    meta_ref[0] = meta.astype(jnp.int32)
    tw_ref[0] = tw
    run = run_ref[...] + jnp.sum(onehot, axis=0, keepdims=True)
    run_ref[...] = run
    cnt_ref[...] = run


def _mix_kernel(oa_ref, ob_ref, os_ref, ga_ref, gb_ref, gs_ref, wa_ref, wb_ref, wc_ref, x_ref, wo_ref, m_ref, g2_ref,
                whi_ref, wlo_ref, br_ref, x1_ref, h_ref, meta_ref, tw_ref, cnt_ref, run_ref, *, n_exp):
    first = jnp.logical_and(pl.program_id(0) == 0, pl.program_id(1) == 0)

    @pl.when(first)
    def _():
        run_ref[...] = jnp.zeros_like(run_ref)

    mix = jax.nn.sigmoid(ga_ref[0].astype(F32)) * jnp.dot(oa_ref[0], wa_ref[...], preferred_element_type=F32)
    mix += jax.nn.sigmoid(gb_ref[0].astype(F32)) * jnp.dot(ob_ref[0], wb_ref[...], preferred_element_type=F32)
    mix += jax.nn.sigmoid(gs_ref[0].astype(F32)) * jnp.dot(os_ref[0], wc_ref[...], preferred_element_type=F32)
    m = m_ref[0, 0]
    x1 = x_ref[0] + m[2:3] * jnp.dot(mix.astype(BF16), wo_ref[...], preferred_element_type=F32)
    x1_ref[0] = x1
    h = _rms(x1) * g2_ref[...] * (1.0 + m[4:5]) + m[3:4]
    _route(h, whi_ref, wlo_ref, br_ref, h_ref, meta_ref, tw_ref, cnt_ref, run_ref, n_exp)


def _mix_route(oa, ob, os_, z3, wa, wb, wc, x, wo, modsel, g2, w_router, b_router, nlt, ntile):
    bsz, _, k = oa.shape
    d = wa.shape[1]
    n_exp = w_router.shape[1]
    wr = jnp.zeros((d, LANES), F32).at[:, :n_exp].set(w_router.astype(F32))
    br = jnp.zeros((1, LANES), F32).at[0, :n_exp].set(b_router.astype(F32))
    tok = lambda last: pl.BlockSpec((1, TOK, last), lambda i, j: (i, j, 0))
    gspec = lambda br_: pl.BlockSpec((1, TOK, d), lambda i, j: (i, j, ZC_GATE // d + br_))
    once = lambda shp: pl.BlockSpec(shp, lambda i, j: (0,) * len(shp), pipeline_mode=pl.Buffered(1))
    rows = ntile * TOK
    return pl.pallas_call(
        functools.partial(_mix_kernel, n_exp=n_exp),
        grid=(bsz, ntile),
        in_specs=[tok(k), tok(k), tok(k), gspec(0), gspec(1), gspec(2), once((k, d)), once((k, d)), once((k, d)),
                  tok(d), once((d, d)),
                  pl.BlockSpec((1, 1, 6, d), lambda i, j: (i, _type_idx(j, nlt), 0, 0)),
                  once((1, d)), once((d, LANES)), once((d, LANES)), once((1, LANES))],
        out_specs=[tok(d), tok(d // 2), tok(LANES), tok(LANES), pl.BlockSpec((1, LANES), lambda i, j: (0, 0))],
        out_shape=[jax.ShapeDtypeStruct((bsz, rows, d), F32),
                   jax.ShapeDtypeStruct((bsz, rows, d // 2), jnp.uint32),
                   jax.ShapeDtypeStruct((bsz, rows, LANES), jnp.int32),
                   jax.ShapeDtypeStruct((bsz, rows, LANES), F32),
                   jax.ShapeDtypeStruct((1, LANES), F32)],
        scratch_shapes=[pltpu.VMEM((1, LANES), F32)],
        compiler_params=_cparams(("arbitrary", "arbitrary")),
        name="mix_route",
    )(oa, ob, os_, z3, z3, z3, wa, wb, wc, x, wo, modsel, g2.reshape(1, d), *_split_bf16(wr), br)


def _dispatch_kernel(pos_ref, h_ref, xs_in_ref, xs_ref, sem):
    del xs_in_ref

    def row_copy(r, k):
        return pltpu.make_async_copy(h_ref.at[pl.ds(r, 1)], xs_ref.at[pl.ds(pos_ref[r * TOP_K + k], 1)], sem)

    def start(r, c):
        for k in range(TOP_K):
            row_copy(r, k).start(priority=k % 2)
        return c

    def wait(r, c):
        for k in range(TOP_K):
            row_copy(r, k).wait()
        return c

    lax.fori_loop(0, TOK, start, 0)
    lax.fori_loop(0, TOK, wait, 0, unroll=DMA_UNROLL)


def _dispatch(pos, h2, xs0):
    m, d = h2.shape
    return pl.pallas_call(
        _dispatch_kernel,
        grid=(m // TOK,),
        in_specs=[pl.BlockSpec((TOK * TOP_K,), lambda i: (i,), memory_space=pltpu.SMEM),
                  pl.BlockSpec((TOK, d), lambda i: (i, 0)),
                  pl.BlockSpec(memory_space=pl.ANY)],
        out_specs=pl.BlockSpec(memory_space=pl.ANY),
        out_shape=jax.ShapeDtypeStruct(xs0.shape, xs0.dtype),
        scratch_shapes=[pltpu.SemaphoreType.DMA(())],
        input_output_aliases={2: 0},
        compiler_params=_cparams(("arbitrary",)),
        name="dispatch",
    )(pos, h2, xs0)


def _experts_kernel(te_ref, tv_ref, x_ref, wgu_ref, bgu_ref, wd_ref, bd_ref, y_ref, *, d_ff, fchunk):
    i = pl.program_id(0)

    @pl.when(tv_ref[i] == 0)
    def _():
        y_ref[...] = jnp.zeros_like(y_ref)

    @pl.when(tv_ref[i] != 0)
    def _():
        x = jnp.concatenate(_unpack_bf16_pairs(x_ref[...]), axis=1).astype(BF16)
        acc = jnp.zeros(x.shape, F32)
        for c in range(d_ff // fchunk):
            lo = c * fchunk
            g = jnp.dot(x, wgu_ref[0, 0, :, lo:lo + fchunk], preferred_element_type=F32) + bgu_ref[0, :, lo:lo + fchunk]
            lin = (jnp.dot(x, wgu_ref[0, 0, :, d_ff + lo:d_ff + lo + fchunk], preferred_element_type=F32)
                   + bgu_ref[0, :, d_ff + lo:d_ff + lo + fchunk])
            g = jnp.minimum(g, SWIGLU_LIMIT)
            lin = jnp.clip(lin, -SWIGLU_LIMIT, SWIGLU_LIMIT)
            act = g * jax.nn.sigmoid(SWIGLU_ALPHA * g) * (lin + 1.0)
            acc = acc + jnp.dot(act.astype(BF16), wd_ref[0, 0, lo:lo + fchunk, :], preferred_element_type=F32)
        y_ref[...] = _pack_bf16_pairs(acc + bd_ref[0])


def _experts(te, tv, xs, wgu, bgu, wd, bd, layer):
    rows, dh = xs.shape
    _, n_exp, d, ff2 = wgu.shape
    d_ff = ff2 // 2
    grid_spec = pltpu.PrefetchScalarGridSpec(
        num_scalar_prefetch=2,
        grid=(rows // TOK,),
        in_specs=[pl.BlockSpec((TOK, dh), lambda i, te, tv: (i, 0)),
                  pl.BlockSpec((1, 1, d, ff2), lambda i, te, tv: (layer, te[i], 0, 0)),
                  pl.BlockSpec((1, 1, ff2), lambda i, te, tv: (te[i], 0, 0)),
                  pl.BlockSpec((1, 1, d_ff, d), lambda i, te, tv: (layer, te[i], 0, 0)),
                  pl.BlockSpec((1, 1, d), lambda i, te, tv: (te[i], 0, 0))],
        out_specs=pl.BlockSpec((TOK, dh), lambda i, te, tv: (i, 0)),
    )
    return pl.pallas_call(
        functools.partial(_experts_kernel, d_ff=d_ff, fchunk=768),
        grid_spec=grid_spec,
        out_shape=jax.ShapeDtypeStruct((rows, dh), jnp.uint32),
        compiler_params=_cparams(("arbitrary",)),
        name="experts",
    )(te, tv, xs, wgu, bgu.reshape(n_exp, 1, ff2), wd, bd.reshape(n_exp, 1, d))


def _combine_kernel(pos_ref, x_ref, tw_ref, m_ref, ys_ref, o_ref, buf_ref, sem):
    def row_copy(r, k):
        return pltpu.make_async_copy(ys_ref.at[pl.ds(pos_ref[r * TOP_K + k], 1)], buf_ref.at[k, pl.ds(r, 1)], sem)

    def start(r, c):
        for k in range(TOP_K):
            row_copy(r, k).start(priority=k % 2)
        return c

    def wait(r, c):
        for k in range(TOP_K):
            row_copy(r, k).wait()
        return c

    lax.fori_loop(0, TOK, start, 0, unroll=DMA_UNROLL)
    lax.fori_loop(0, TOK, wait, 0, unroll=DMA_UNROLL)
    tw = tw_ref[0]
    acc_lo = acc_hi = None
    for k in range(TOP_K):
        lo, hi = _unpack_bf16_pairs(buf_ref[k])
        wk = tw[:, k:k + 1]
        acc_lo = wk * lo if acc_lo is None else acc_lo + wk * lo
        acc_hi = wk * hi if acc_hi is None else acc_hi + wk * hi
    o_ref[0] = x_ref[0] + m_ref[0, 0][5:6] * jnp.concatenate([acc_lo, acc_hi], axis=1)


def _combine(pos, x, tw, modsel, ys, nlt):
    b, t, d = x.shape
    nt = t // TOK
    return pl.pallas_call(
        _combine_kernel,
        grid=(b, nt),
        in_specs=[pl.BlockSpec((TOK * TOP_K,), lambda i, j: (i * nt + j,), memory_space=pltpu.SMEM),
                  pl.BlockSpec((1, TOK, d), lambda i, j: (i, j, 0)),
                  pl.BlockSpec((1, TOK, LANES), lambda i, j: (i, j, 0)),
                  pl.BlockSpec((1, 1, 6, d), lambda i, j: (i, _type_idx(j, nlt), 0, 0)),
                  pl.BlockSpec(memory_space=pl.ANY)],
        out_specs=pl.BlockSpec((1, TOK, d), lambda i, j: (i, j, 0)),
        out_shape=jax.ShapeDtypeStruct((b, t, d), F32),
        scratch_shapes=[pltpu.VMEM((TOP_K, TOK, d // 2), jnp.uint32), pltpu.SemaphoreType.DMA(())],
        compiler_params=_cparams(("arbitrary", "arbitrary")),
        name="combine",
    )(pos, x, tw, modsel, ys)


def _moe(xa, h2, meta, tw, cnt, modsel, n_exp, wgu, bgu, wd, bd, nlt, layer):
    b, t, d = xa.shape
    m = b * t

    counts = cnt[0, :n_exp].astype(jnp.int32)
    padded = ((counts + TOK - 1) // TOK) * TOK
    gend = jnp.cumsum(padded)
    gstart = gend - padded
    meta2 = meta.reshape(m, LANES)
    pos = (gstart[meta2[:, 0:TOP_K]] + meta2[:, TOP_K:2 * TOP_K]).reshape(m * TOP_K)
    ntiles = (m * TOP_K) // TOK + n_exp
    tile_start = jnp.arange(ntiles, dtype=jnp.int32) * TOK
    te = jnp.sum((tile_start[:, None] >= gend[None, :]).astype(jnp.int32), axis=1)
    te = jnp.minimum(te, n_exp - 1)
    tv = (tile_start < gend[-1]).astype(jnp.int32)

    xs = _dispatch(pos, h2.reshape(m, d // 2), jnp.zeros((ntiles * TOK, d // 2), jnp.uint32))
    ys = _experts(te, tv, xs, wgu, bgu, wd, bd, layer)
    return _combine(pos, xa, tw, modsel, ys, nlt)


def _rope_tables(seq, ctx):
    t = jnp.arange(seq, dtype=jnp.int32)
    row = (t // GRID_W).astype(F32)
    col = (t % GRID_W).astype(F32)
    n_freq = MLA_ROPE // 4
    inv_freq = ROPE_BASE ** (-jnp.arange(n_freq, dtype=F32) / n_freq)
    ang = jnp.concatenate([row[:, None] * inv_freq, col[:, None] * inv_freq], axis=-1)
    cos, sin = jnp.cos(ang), jnp.sin(ang)
    cos2 = jnp.tile(jnp.concatenate([cos, cos], axis=-1), (1, 2))
    sin2 = jnp.tile(jnp.concatenate([-sin, sin], axis=-1), (1, 2))
    cos2 = jnp.concatenate([cos2, jnp.ones((ctx, LANES), F32)], axis=0)
    sin2 = jnp.concatenate([sin2, jnp.zeros((ctx, LANES), F32)], axis=0)
    return cos2, sin2


def _pad_cols(w, width):
    return jnp.pad(w, ((0, 0), (0, width - w.shape[1])))


def _layout_w_in(w):
    o_ckv = MLA_Q_RANK
    o_kpe = o_ckv + MLA_KV_RANK
    o_naq = o_kpe + MLA_ROPE
    o_ssm = o_naq + 3 * NA_WIDTH
    o_gate = o_ssm + SSM_WIDTH
    parts = [w[:, o_gate:],
             w[:, o_naq:o_ssm],
             w[:, :o_ckv],
             _pad_cols(w[:, o_kpe:o_naq], ZC_CKV - ZC_KPE),
             w[:, o_ckv:o_kpe],
             w[:, o_ssm:o_gate]]
    out = jnp.concatenate(parts, axis=1)
    assert out.shape[1] == ZW
    return out.astype(BF16)


def kernel(x, c, ctx, c_ctx, w_ada, b_ada, norm1_g, w_in, mla_q_norm, mla_kv_norm, mla_w_uq, mla_w_ukv, mla_q_gain, mla_k_gain, na_q_gain, na_k_gain, na_rpb, ssm_lam_re, ssm_lam_im, ssm_log_dt, ssm_b_re, ssm_b_im, ssm_c_re, ssm_c_im, ssm_d, ssm_w_glu, ssm_b_glu, w_pa, w_pb, w_pc, w_out, norm2_g, w_router, b_router, w_gate_up, b_gate_up, w_down, b_down):
    bsz, seq, d = x.shape
    nctx = ctx.shape[1]
    depth = w_ada.shape[0]
    assert nctx == TOK and seq % (NA_QROWS * GRID_W) == 0 and seq // GRID_W >= NA_KROWS
    nlt = seq // TOK
    t = seq + nctx
    m = bsz * t
    rows = seq // GRID_W
    nblk = rows // NA_QROWS
    nchunk = t // SSM_CHUNK

    cos2, sin2 = _rope_tables(seq, nctx)
    cvec = jnp.zeros((16, d), F32).at[:bsz].set(c.astype(F32)).at[bsz].set(c_ctx.astype(F32))
    xa = jnp.concatenate([x, ctx], axis=1).astype(F32)
    row128 = lambda v: v.astype(F32).reshape(1, LANES)
    perm = jnp.asarray(_ssm_perm(), BF16)
    wgu_all = w_gate_up.astype(BF16)
    wd_all = w_down.astype(BF16)

    for l in range(depth):
        need_ctx = l < depth - 1
        ntile = t // TOK if need_ctx else nlt
        mod = _ada(cvec, w_ada[l], b_ada[l])
        mod_lat = mod[:bsz].reshape(bsz, 1, 6, d)
        mod_ctx = jnp.broadcast_to(mod[bsz].reshape(1, 1, 6, d), (bsz, 1, 6, d))
        modsel = jnp.concatenate([mod_lat, mod_ctx], axis=1)

        h1 = _norm_mod(xa, norm1_g[l], modsel, nlt)
        z = _matmul(h1.reshape(m, d), _layout_w_in(w_in[l]), BF16, "in_proj", ZW // 4)
        z3 = z.reshape(bsz, t, ZW)

        wq = mla_w_uq[l].reshape(MLA_Q_RANK, MLA_HEADS, MLA_QK)
        wq = jnp.concatenate([wq[:, :, :MLA_NOPE].reshape(MLA_Q_RANK, -1),
                              wq[:, :, MLA_NOPE:].reshape(MLA_Q_RANK, -1)], axis=1).astype(BF16)
        qg_r = jnp.tile(mla_q_gain[l][MLA_NOPE:], 2)
        kg_r = jnp.concatenate([mla_k_gain[l][MLA_NOPE:], jnp.zeros((MLA_ROPE,), mla_k_gain.dtype)])
        qf, kf, vf, qn, kn = _projections(
            z3, mla_q_norm[l].astype(F32).reshape(1, -1), wq, row128(mla_q_gain[l][:MLA_NOPE]), row128(qg_r),
            mla_kv_norm[l].astype(F32).reshape(1, -1), mla_w_ukv[l].astype(BF16), row128(mla_k_gain[l][:MLA_NOPE]),
            row128(kg_r), cos2, sin2, row128(jnp.tile(na_q_gain[l], 2)), row128(jnp.tile(na_k_gain[l], 2)))
        o_a = _mla_attn(qf, kf, vf, seq, need_ctx)

        bias = _na_bias_tables(na_rpb[l], rows)
        o_b = _na_attn(qn, kn, z3, bias, nblk, need_ctx)

        wb2, mt2, wc2, d2, ar2, ai2 = _ssm_params(ssm_lam_re[l], ssm_lam_im[l], ssm_log_dt[l], ssm_b_re[l],
                                                   ssm_b_im[l], ssm_c_re[l], ssm_c_im[l], ssm_d[l])
        r = nchunk * bsz
        u = z3[:, :, ZC_SSM:ZC_SSM + SSM_WIDTH].reshape(bsz, nchunk, SSM_CHUNK, SSM_NSUPER, LANES)
        x8 = jnp.transpose(u, (3, 1, 0, 2, 4)).reshape(SSM_NSUPER * r, SSM_CW)
        xp = _matmul(x8, perm, BF16, "ssm_perm_in", SSM_CW)
        yp = _ssm(xp.reshape(SSM_NSUPER, r, SSM_CW), wb2, mt2, wc2, d2, ar2, ai2, bsz, nchunk, nctx // SSM_CHUNK)
        y8 = _matmul(yp.reshape(SSM_NSUPER * r, SSM_CW), perm.T, BF16, "ssm_perm_out", SSM_CW)
        y = jnp.transpose(y8.reshape(SSM_NSUPER, nchunk, bsz, SSM_CHUNK, LANES), (2, 1, 3, 0, 4))
        o_s = _glu(y.reshape(bsz, t, SSM_WIDTH), ssm_w_glu[l].astype(BF16), ssm_b_glu[l].astype(F32), ntile)

        xa, h2, meta, tw, cnt = _mix_route(o_a, o_b, o_s, z3, w_pa[l].astype(BF16), w_pb[l].astype(BF16),
                                           w_pc[l].astype(BF16), xa, w_out[l].astype(BF16), modsel,
                                           norm2_g[l].astype(F32), w_router[l], b_router[l], nlt, ntile)
        xa = _moe(xa, h2, meta, tw, cnt, modsel, w_router.shape[2], wgu_all, b_gate_up[l].astype(F32), wd_all,
                  b_down[l].astype(F32), nlt, l)
    return xa[:, :seq].astype(x.dtype)
```

```python
import functools
import math

import numpy as np
import jax
import jax.numpy as jnp
from jax import lax
from jax.experimental import pallas as pl
from jax.experimental.pallas import tpu as pltpu

F32 = jnp.float32
BF16 = jnp.bfloat16

GRID_W = 64
MLA_HEADS = 8
MLA_NOPE = 128
MLA_ROPE = 64
MLA_V = 128
MLA_QK = MLA_NOPE + MLA_ROPE
MLA_Q_RANK = 768
MLA_KV_RANK = 512
NA_HEADS = 16
NA_HEAD_DIM = 64
NA_WIDTH = NA_HEADS * NA_HEAD_DIM
NA_WIN_R = 8
NA_WIN_C = 16
SSM_WIDTH = 1024
SSM_GROUP = 16
SSM_GROUPS = SSM_WIDTH // SSM_GROUP
SSM_STATE = 64
TOP_K = 4
SWIGLU_LIMIT = 7.0
SWIGLU_ALPHA = 1.702
ROPE_BASE = 10000.0
EPS = 1e-6
MLA_SCALE = MLA_QK ** -0.5
NA_SCALE = NA_HEAD_DIM ** -0.5
NEG_BIG = -1e30

LANES = 128
TOK = 256
MLA_KPAD = 256
MLA_VPAD = 256
MLA_TQ = 1024
MLA_TK = 512
VMEM_LIMIT = 56 * 1024 * 1024
DMA_UNROLL = 4

ZC_GATE = 0
ZC_NAQ = 6144
ZC_NAK = 7168
ZC_NAV = 8192
ZC_CQ = 9216
ZC_KPE = 9984
ZC_CKV = 10240
ZC_SSM = 10752
ZW = 11776

SSM_CHUNK = 16
SSM_NSUPER = SSM_WIDTH // LANES
SSM_CW = SSM_CHUNK * LANES
SSM_PW = 2 * SSM_CHUNK * SSM_GROUP
NA_QROWS = 4
NA_KROWS = 12
NA_BB = 2


def _cparams(sem, vmem=VMEM_LIMIT):
    return pltpu.CompilerParams(dimension_semantics=sem, vmem_limit_bytes=vmem)


def _rms(x, eps=EPS):
    return x * lax.rsqrt(jnp.mean(x * x, axis=-1, keepdims=True) + eps)


def _split_bf16(x):
    hi = x.astype(BF16)
    return hi, (x - hi.astype(F32)).astype(BF16)


def _seg64_rms(t):
    r = lax.broadcasted_iota(jnp.int32, (LANES, LANES), 0)
    c = lax.broadcasted_iota(jnp.int32, (LANES, LANES), 1)
    same_half = jnp.where((r < 64) == (c < 64), 1.0, 0.0).astype(BF16)
    hi, lo = _split_bf16(t * t)
    ssum = (jnp.dot(hi, same_half, preferred_element_type=F32) + jnp.dot(lo, same_half, preferred_element_type=F32))
    return t * lax.rsqrt(ssum * (1.0 / 64.0) + EPS)


def _rope64(t, cos2, sin2):
    lane = lax.broadcasted_iota(jnp.int32, t.shape, 1)
    first = (lane % 64) < 32
    swapped = jnp.where(first, pltpu.roll(t, 96, axis=1), pltpu.roll(t, 32, axis=1))
    return t * cos2 + swapped * sin2


def _ones_col_tile(rows, dtype):
    lane = lax.broadcasted_iota(jnp.int32, (rows, LANES), 1)
    return jnp.where(lane == 0, 1.0, 0.0).astype(dtype)


def _qkt(q, k):
    return lax.dot_general(q, k, (((1,), (1,)), ((), ())), preferred_element_type=F32)


def _softmax_acc(s, v_ext):
    m = jnp.max(s, axis=-1, keepdims=True)
    return jnp.dot(jnp.exp((s - m).astype(BF16)), v_ext, preferred_element_type=F32)


def _softmax_pv(s, v_ext):
    oe = _softmax_acc(s, v_ext)
    return oe[:, 0:LANES] / oe[:, LANES:LANES + 1]


def _pack_bf16_pairs(x):
    n = x.shape[1] // 2
    r = x.astype(BF16).astype(F32)
    lo = lax.bitcast_convert_type(r[:, :n], jnp.uint32) >> 16
    hi = lax.bitcast_convert_type(r[:, n:], jnp.uint32) & jnp.uint32(0xFFFF0000)
    return hi | lo


def _unpack_bf16_pairs(w):
    lo = lax.bitcast_convert_type(w << 16, F32)
    hi = lax.bitcast_convert_type(w & jnp.uint32(0xFFFF0000), F32)
    return lo, hi


def _type_idx(j, nlt):
    return jnp.where(j >= nlt, 1, 0)


def _ada_kernel(c_ref, w_ref, b_ref, o_ref):
    c = c_ref[...]
    a = (c * jax.nn.sigmoid(c)).astype(BF16)
    o_ref[...] = jnp.dot(a, w_ref[...].astype(BF16), preferred_element_type=F32) + b_ref[...]


def _ada(cvec, w, b):
    rows, d = cvec.shape
    n = w.shape[1]
    tn = 1024
    return pl.pallas_call(
        _ada_kernel,
        grid=(n // tn,),
        in_specs=[pl.BlockSpec((rows, d), lambda j: (0, 0)),
                  pl.BlockSpec((d, tn), lambda j: (0, j)),
                  pl.BlockSpec((1, tn), lambda j: (0, j))],
        out_specs=pl.BlockSpec((rows, tn), lambda j: (0, j)),
        out_shape=jax.ShapeDtypeStruct((rows, n), F32),
        compiler_params=_cparams(("arbitrary",)),
        name="ada",
    )(cvec, w, b.reshape(1, n))


def _norm_mod_kernel(x_ref, g_ref, m_ref, o_ref):
    m = m_ref[0, 0]
    y = _rms(x_ref[0]) * g_ref[...]
    o_ref[0] = (y * (1.0 + m[1:2]) + m[0:1]).astype(o_ref.dtype)


def _norm_mod(x, g, modsel, nlt):
    b, t, d = x.shape
    return pl.pallas_call(
        _norm_mod_kernel,
        grid=(b, t // TOK),
        in_specs=[pl.BlockSpec((1, TOK, d), lambda i, j: (i, j, 0)),
                  pl.BlockSpec((1, d), lambda i, j: (0, 0)),
                  pl.BlockSpec((1, 1, 6, d), lambda i, j: (i, _type_idx(j, nlt), 0, 0))],
        out_specs=pl.BlockSpec((1, TOK, d), lambda i, j: (i, j, 0)),
        out_shape=jax.ShapeDtypeStruct((b, t, d), BF16),
        compiler_params=_cparams(("arbitrary", "arbitrary")),
        name="norm_mod",
    )(x, g.reshape(1, d), modsel)


def _mm_kernel(a_ref, w_ref, o_ref):
    o_ref[...] = jnp.dot(a_ref[...], w_ref[...], preferred_element_type=F32).astype(o_ref.dtype)


def _pick(n, cands):
    for c in cands:
        if n % c == 0:
            return c
    return n


def _matmul(a, w, out_dtype, name, tn):
    m, k = a.shape
    n = w.shape[1]
    tm = _pick(m, (512, 256, 128))
    return pl.pallas_call(
        _mm_kernel,
        grid=(n // tn, m // tm),
        in_specs=[pl.BlockSpec((tm, k), lambda j, i: (i, 0)),
                  pl.BlockSpec((k, tn), lambda j, i: (0, j))],
        out_specs=pl.BlockSpec((tm, tn), lambda j, i: (i, j)),
        out_shape=jax.ShapeDtypeStruct((m, n), out_dtype),
        compiler_params=_cparams(("arbitrary", "arbitrary")),
        name=name,
    )(a, w)


def _mla_q_part(z_ref, g_ref, w_ref, gn_ref, gr_ref, cos, sin, o_ref):
    cq = z_ref[0].astype(F32)
    xn = (_rms(cq) * g_ref[...]).astype(BF16)
    q = jnp.dot(xn, w_ref[...], preferred_element_type=F32)
    lane = lax.broadcasted_iota(jnp.int32, (TOK, LANES), 1)
    left = lane < 64
    for h in range(MLA_HEADS):
        qn = q[:, h * MLA_NOPE:(h + 1) * MLA_NOPE]
        o_ref[0, h, :, 0:LANES] = (_rms(qn) * (gn_ref[...] * MLA_SCALE)).astype(o_ref.dtype)
    base = MLA_HEADS * MLA_NOPE
    for hp in range(MLA_HEADS // 2):
        t = q[:, base + hp * LANES: base + (hp + 1) * LANES]
        t = _seg64_rms(t) * (gr_ref[...] * MLA_SCALE)
        r = _rope64(t, cos, sin)
        o_ref[0, 2 * hp, :, LANES:2 * LANES] = jnp.where(left, r, 0.0).astype(o_ref.dtype)
        o_ref[0, 2 * hp + 1, :, LANES:2 * LANES] = jnp.where(left, pltpu.roll(r, 64, axis=1), 0.0).astype(o_ref.dtype)


def _mla_kv_part(z_ref, zk_ref, g_ref, w_ref, gn_ref, gr_ref, cos, sin, k_ref, v_ref):
    ckv = z_ref[0].astype(F32)
    xn = (_rms(ckv) * g_ref[...]).astype(BF16)
    kv = jnp.dot(xn, w_ref[...], preferred_element_type=F32)
    lane = lax.broadcasted_iota(jnp.int32, (TOK, LANES), 1)
    left = lane < 64
    t = jnp.where(left, zk_ref[0].astype(F32), 0.0)
    ms = jnp.sum(t * t, axis=-1, keepdims=True) * (1.0 / MLA_ROPE)
    kr = t * lax.rsqrt(ms + EPS) * gr_ref[...]
    kr = jnp.where(left, _rope64(kr, cos, sin), 0.0).astype(k_ref.dtype)
    ones = _ones_col_tile(TOK, v_ref.dtype)
    hw = MLA_NOPE + MLA_V
    for h in range(MLA_HEADS):
        kn = kv[:, h * hw: h * hw + MLA_NOPE]
        k_ref[0, h, :, 0:LANES] = (_rms(kn) * gn_ref[...]).astype(k_ref.dtype)
        k_ref[0, h, :, LANES:2 * LANES] = kr
        v_ref[0, h, :, 0:LANES] = kv[:, h * hw + MLA_NOPE:(h + 1) * hw].astype(v_ref.dtype)
        v_ref[0, h, :, LANES:2 * LANES] = ones


def _na_norm_part(q_ref, k_ref, gq_ref, gk_ref, qn_ref, kn_ref):
    for c in range(q_ref.shape[2] // LANES):
        sl = slice(c * LANES, (c + 1) * LANES)
        qn_ref[0, :, sl] = (_seg64_rms(q_ref[0, :, sl].astype(F32)) * (gq_ref[...] * NA_SCALE)).astype(qn_ref.dtype)
        kn_ref[0, :, sl] = (_seg64_rms(k_ref[0, :, sl].astype(F32)) * gk_ref[...]).astype(kn_ref.dtype)


def _proj_kernel(zq_ref, zkv_ref, zk_ref, znq_ref, znk_ref, gq_ref, wq_ref, qgn_ref, qgr_ref, gkv_ref, wkv_ref,
                 kgn_ref, kgr_ref, cos_ref, sin_ref, ngq_ref, ngk_ref, q_ref, k_ref, v_ref, qn_ref, kn_ref):
    cos, sin = cos_ref[...], sin_ref[...]
    _mla_q_part(zq_ref, gq_ref, wq_ref, qgn_ref, qgr_ref, cos, sin, q_ref)
    _mla_kv_part(zkv_ref, zk_ref, gkv_ref, wkv_ref, kgn_ref, kgr_ref, cos, sin, k_ref, v_ref)
    _na_norm_part(znq_ref, znk_ref, ngq_ref, ngk_ref, qn_ref, kn_ref)


def _projections(z3, gq, wq, qgn, qgr2, gkv, wkv, kgn, kgr2, cos2, sin2, ngq2, ngk2):
    b, t, _ = z3.shape
    zspec = lambda w, col0: pl.BlockSpec((1, TOK, w), lambda i, j: (i, j, col0 // w))
    const = lambda a: pl.BlockSpec(a.shape, lambda i, j: (0,) * a.ndim)
    tab = pl.BlockSpec((TOK, LANES), lambda i, j: (j, 0))
    head_spec = lambda w: pl.BlockSpec((1, MLA_HEADS, TOK, w), lambda i, j: (i, 0, j, 0))
    na_spec = pl.BlockSpec((1, TOK, NA_WIDTH), lambda i, j: (i, j, 0))
    head_shape = lambda w: jax.ShapeDtypeStruct((b, MLA_HEADS, t, w), BF16)
    na_shape = jax.ShapeDtypeStruct((b, t, NA_WIDTH), BF16)
    return pl.pallas_call(
        _proj_kernel,
        grid=(b, t // TOK),
        in_specs=[zspec(MLA_Q_RANK, ZC_CQ), zspec(MLA_KV_RANK, ZC_CKV), zspec(LANES, ZC_KPE),
                  zspec(NA_WIDTH, ZC_NAQ), zspec(NA_WIDTH, ZC_NAK),
                  const(gq), const(wq), const(qgn), const(qgr2), const(gkv), const(wkv), const(kgn), const(kgr2),
                  tab, tab, const(ngq2), const(ngk2)],
        out_specs=[head_spec(MLA_KPAD), head_spec(MLA_KPAD), head_spec(MLA_VPAD), na_spec, na_spec],
        out_shape=[head_shape(MLA_KPAD), head_shape(MLA_KPAD), head_shape(MLA_VPAD), na_shape, na_shape],
        compiler_params=_cparams(("arbitrary", "arbitrary")),
        name="projections",
    )(z3, z3, z3, z3, z3, gq, wq, qgn, qgr2, gkv, wkv, kgn, kgr2, cos2, sin2, ngq2, ngk2)


def _mla_attn_kernel(q_ref, k_ref, v_ref, o_ref, *, nk, tk):
    q = q_ref[0, 0]
    m = acc = None
    for c0 in range(0, nk, tk):
        sz = min(tk, nk - c0)
        s = _qkt(q, k_ref[0, 0, c0:c0 + sz, :])
        cm = jnp.max(s, axis=-1, keepdims=True)
        m_new = cm if m is None else jnp.maximum(m, cm)
        pv = jnp.dot(jnp.exp((s - m_new).astype(BF16)), v_ref[0, 0, c0:c0 + sz, :], preferred_element_type=F32)
        acc = pv if m is None else acc * jnp.exp(m - m_new) + pv
        m = m_new
    o_ref[0] = (acc[:, 0:LANES] / acc[:, LANES:LANES + 1]).astype(o_ref.dtype)


def _mla_attn_ctx_kernel(q_ref, k_ref, v_ref, o_ref):
    o_ref[0] = _softmax_pv(_qkt(q_ref[0, 0], k_ref[0, 0]), v_ref[0, 0]).astype(o_ref.dtype)


def _mla_attn(q, k, v, seq, need_ctx):
    b, h, t, _ = k.shape
    tq = _pick(seq, (MLA_TQ, TOK))
    sem = ("arbitrary", "arbitrary", "arbitrary")
    o = pl.pallas_call(
        functools.partial(_mla_attn_kernel, nk=t, tk=MLA_TK),
        grid=(b, h, seq // tq),
        in_specs=[pl.BlockSpec((1, 1, tq, MLA_KPAD), lambda i, hh, j: (i, hh, j, 0)),
                  pl.BlockSpec((1, 1, t, MLA_KPAD), lambda i, hh, j: (i, hh, 0, 0)),
                  pl.BlockSpec((1, 1, t, MLA_VPAD), lambda i, hh, j: (i, hh, 0, 0))],
        out_specs=pl.BlockSpec((1, tq, MLA_V), lambda i, hh, j: (i, j, hh)),
        out_shape=jax.ShapeDtypeStruct((b, seq, h * MLA_V), BF16),
        compiler_params=_cparams(sem),
        name="mla_attn",
    )(q, k, v)
    if not need_ctx:
        return o
    nctx = t - seq
    ctx_spec = lambda w: pl.BlockSpec((1, 1, nctx, w), lambda i, hh, j: (i, hh, seq // nctx, 0))
    o_c = pl.pallas_call(
        _mla_attn_ctx_kernel,
        grid=(b, h, 1),
        in_specs=[ctx_spec(MLA_KPAD), ctx_spec(MLA_KPAD), ctx_spec(MLA_VPAD)],
        out_specs=pl.BlockSpec((1, nctx, MLA_V), lambda i, hh, j: (i, 0, hh)),
        out_shape=jax.ShapeDtypeStruct((b, nctx, h * MLA_V), BF16),
        compiler_params=_cparams(sem),
        name="mla_attn_ctx",
    )(q, k, v)
    return jnp.concatenate([o, o_c], axis=1)


def _na_kernel(q_ref, k0_ref, k1_ref, k2_ref, kc_ref, v0_ref, v1_ref, v2_ref, vc_ref, bias_ref, o_ref, *, nblk):
    j = pl.program_id(1)
    lane = lax.broadcasted_iota(jnp.int32, (TOK, LANES), 1)
    left = lane < 64
    ones = _ones_col_tile(TOK, BF16)
    nb = q_ref.shape[0]

    def stacked_q(bi):
        qn = q_ref[bi]
        zero = jnp.zeros_like(qn)
        return jnp.concatenate([jnp.where(left, qn, zero), jnp.where(left, zero, qn)], axis=0)

    def finish(bi, acc):
        out = acc[:, 0:LANES] / acc[:, LANES:LANES + 1]
        o_ref[bi] = jnp.where(left, out[0:TOK], out[TOK:2 * TOK]).astype(o_ref.dtype)

    @pl.when(j >= nblk)
    def _():
        for bi in range(nb):
            v_ext = jnp.concatenate([vc_ref[bi], ones], axis=1)
            finish(bi, _softmax_acc(_qkt(stacked_q(bi), kc_ref[bi]), v_ext))

    @pl.when(j < nblk)
    def _():
        q2 = [stacked_q(bi) for bi in range(nb)]
        m = [None] * nb
        acc = [None] * nb
        for c, (k_ref, v_ref) in enumerate(((k0_ref, v0_ref), (k1_ref, v1_ref), (k2_ref, v2_ref), (kc_ref, vc_ref))):
            bias = bias_ref[0, :, :, c * TOK:(c + 1) * TOK].reshape(2 * TOK, TOK)
            for bi in range(nb):
                v_ext = jnp.concatenate([v_ref[bi], ones], axis=1)
                s = _qkt(q2[bi], k_ref[bi]) + bias
                cm = jnp.max(s, axis=-1, keepdims=True)
                m_new = cm if m[bi] is None else jnp.maximum(m[bi], cm)
                pv = jnp.dot(jnp.exp((s - m_new).astype(BF16)), v_ext, preferred_element_type=F32)
                acc[bi] = pv if m[bi] is None else acc[bi] * jnp.exp(m[bi] - m_new) + pv
                m[bi] = m_new
        for bi in range(nb):
            finish(bi, acc[bi])


def _na_attn(qn, kn, z3, bias, nblk, need_ctx):
    b = z3.shape[0]
    bb = _pick(b, (NA_BB, 1))
    kb = lambda j: jnp.clip(j - 1, 0, nblk - 3)
    nj = nblk + (1 if need_ctx else 0)

    def tspec(col_blk0, tok_idx):
        return pl.BlockSpec((bb, TOK, LANES), lambda hp, j, i: (i, tok_idx(j), col_blk0 + hp))

    var = lambda j: jnp.where(j == 0, 0, jnp.where(j == nblk - 1, 2, 1))
    window = lambda c0: [tspec(c0, (lambda j, d=d: kb(j) + d)) for d in range(3)] + [tspec(c0, lambda j: nblk)]
    in_specs = [tspec(0, lambda j: j)] + window(0) + window(ZC_NAV // LANES)
    in_specs += [pl.BlockSpec((1, 2, TOK, 4 * TOK), lambda hp, j, i: (var(j), hp, 0, 0))]
    return pl.pallas_call(
        functools.partial(_na_kernel, nblk=nblk),
        grid=(NA_HEADS // 2, nj, b // bb),
        in_specs=in_specs,
        out_specs=pl.BlockSpec((bb, TOK, LANES), lambda hp, j, i: (i, j, hp)),
        out_shape=jax.ShapeDtypeStruct((b, nj * TOK, NA_WIDTH), BF16),
        compiler_params=_cparams(("arbitrary", "arbitrary", "arbitrary")),
        name="na_attn",
    )(qn, kn, kn, kn, kn, z3, z3, z3, z3, bias)


def _na_bias_tables(rpb, rows):
    nblk = rows // NA_QROWS
    nh = rpb.shape[0]
    ncol = 2 * NA_WIN_C - 1
    dd = np.arange(ncol)[:, None, None]
    qc1 = np.arange(GRID_W)[None, :, None]
    kc1 = np.arange(GRID_W)[None, None, :]
    onehot = (kc1 - qc1 + NA_WIN_C - 1 == dd).astype(np.float32)
    toep = jnp.einsum('hrd,dqk->hrqk', rpb.astype(F32), onehot, precision=lax.Precision.HIGHEST)
    c0 = np.clip(np.arange(GRID_W) - NA_WIN_C // 2, 0, GRID_W - NA_WIN_C)
    kcs = np.arange(GRID_W)
    valid_c = (kcs[None, :] >= c0[:, None]) & (kcs[None, :] < c0[:, None] + NA_WIN_C)
    tabs = []
    for j in (0, 1, nblk - 1):
        start = int(np.clip(NA_QROWS * j - NA_QROWS, 0, rows - NA_KROWS))
        qr = NA_QROWS * j + np.arange(NA_QROWS)
        kr = start + np.arange(NA_KROWS)
        r0 = np.clip(qr - NA_WIN_R // 2, 0, rows - NA_WIN_R)
        valid_r = (kr[None, :] >= r0[:, None]) & (kr[None, :] < r0[:, None] + NA_WIN_R)
        drow = np.clip(kr[None, :] - qr[:, None] + NA_WIN_R - 1, 0, 2 * NA_WIN_R - 2).reshape(-1)
        sel = jnp.take(toep, drow, axis=1).reshape(nh, NA_QROWS, NA_KROWS, GRID_W, GRID_W)
        sel = jnp.transpose(sel, (0, 1, 3, 2, 4)).reshape(nh, TOK, NA_KROWS * GRID_W)
        valid = (valid_r[:, None, :, None] & valid_c[None, :, None, :]).reshape(TOK, NA_KROWS * GRID_W)
        loc = jnp.where(valid[None], sel, NEG_BIG)
        tabs.append(jnp.concatenate([loc, jnp.zeros((nh, TOK, TOK), F32)], axis=-1))
    return jnp.stack(tabs)


def _ssm_kernel(u_ref, wb_ref, mt_ref, wc_ref, d_ref, ar_ref, ai_ref, y_ref, sh_ref, *, bsz, nchunk, ncc):
    u = u_ref[0]
    sh_ref[...] = jnp.dot(u, wb_ref[0], preferred_element_type=F32)
    a_r = ar_ref[0]
    a_i = ai_ref[0]
    arf, aif, arr, air = a_r[:, 0:LANES], a_i[:, 0:LANES], a_r[:, LANES:], a_i[:, LANES:]
    nlc = nchunk - ncc

    def body(i, carry):
        hre_f, him_f, hre_r, him_r = carry
        cf = jnp.where(i < ncc, nlc + i, i - ncc)
        cr = nchunk - 1 - i
        rf = pl.multiple_of(cf * bsz, bsz)
        rr = pl.multiple_of(cr * bsz, bsz)
        sre_f = sh_ref[pl.ds(rf, bsz), 0:LANES]
        sim_f = sh_ref[pl.ds(rf, bsz), LANES:2 * LANES]
        sre_r = sh_ref[pl.ds(rr, bsz), 2 * LANES:3 * LANES]
        sim_r = sh_ref[pl.ds(rr, bsz), 3 * LANES:4 * LANES]
        sh_ref[pl.ds(rf, bsz), 0:LANES] = hre_f
        sh_ref[pl.ds(rf, bsz), LANES:2 * LANES] = him_f
        sh_ref[pl.ds(rr, bsz), 2 * LANES:3 * LANES] = hre_r
        sh_ref[pl.ds(rr, bsz), 3 * LANES:4 * LANES] = him_r
        return (arf * hre_f - aif * him_f + sre_f, arf * him_f + aif * hre_f + sim_f,
                arr * hre_r - air * him_r + sre_r, arr * him_r + air * hre_r + sim_r)

    zero = jnp.zeros((bsz, LANES), F32)
    lax.fori_loop(0, nchunk, body, (zero, zero, zero, zero))
    y = jnp.dot(u, mt_ref[0], preferred_element_type=F32)
    y = y + jnp.dot(sh_ref[...].astype(BF16), wc_ref[0], preferred_element_type=F32)
    y_ref[0] = (y + d_ref[0] * u.astype(F32)).astype(y_ref.dtype)


def _ssm(xp, wb2, mt2, wc2, d2, ar2, ai2, bsz, nchunk, ncc):
    ns, r, cw = xp.shape
    pw = SSM_PW
    nq = cw // pw
    wspec = lambda shp: pl.BlockSpec((1,) + shp, lambda g, q: (g * nq + q, 0, 0))
    return pl.pallas_call(
        functools.partial(_ssm_kernel, bsz=bsz, nchunk=nchunk, ncc=ncc),
        grid=(ns, nq),
        in_specs=[pl.BlockSpec((1, r, pw), lambda g, q: (g, 0, q)),
                  wspec((pw, pw)), wspec((pw, pw)), wspec((pw, pw)),
                  wspec((1, pw)), wspec((1, 2 * LANES)), wspec((1, 2 * LANES))],
        out_specs=pl.BlockSpec((1, r, pw), lambda g, q: (g, 0, q)),
        out_shape=jax.ShapeDtypeStruct((ns, r, cw), BF16),
        scratch_shapes=[pltpu.VMEM((r, pw), F32)],
        compiler_params=_cparams(("arbitrary", "arbitrary")),
        name="ssm",
    )(xp, wb2, mt2, wc2, d2, ar2, ai2)


def _ssm_perm():
    r = np.arange(SSM_CW)
    s, g, mm = r // LANES, (r % LANES) // SSM_GROUP, r % SSM_GROUP
    dst = g * (SSM_CHUNK * SSM_GROUP) + s * SSM_GROUP + mm
    return (dst[:, None] == np.arange(SSM_CW)[None, :]).astype(np.float32)


def _ssm_params(lam_re, lam_im, log_dt, b_re, b_im, c_re, c_im, d_skip):
    L = SSM_CHUNK
    lam_re, lam_im = lam_re.astype(F32), lam_im.astype(F32)
    dt = jnp.exp(log_dt.astype(F32))[..., None]
    tau = jnp.arange(L + 1, dtype=F32)[:, None, None, None]
    mag = jnp.exp(lam_re * dt * tau)
    pw_re = mag * jnp.cos(lam_im * dt * tau)
    pw_im = mag * jnp.sin(lam_im * dt * tau)
    a_re, a_im = pw_re[1], pw_im[1]
    den = lam_re * lam_re + lam_im * lam_im
    z_re = ((a_re - 1.0) * lam_re + a_im * lam_im) / den
    z_im = (a_im * lam_re - (a_re - 1.0) * lam_im) / den
    bb_re = z_re[..., None] * b_re.astype(F32) - z_im[..., None] * b_im.astype(F32)
    bb_im = z_re[..., None] * b_im.astype(F32) + z_im[..., None] * b_re.astype(F32)
    cr, ci = c_re.astype(F32), c_im.astype(F32)

    ab_re = pw_re[..., None] * bb_re - pw_im[..., None] * bb_im
    ab_im = pw_re[..., None] * bb_im + pw_im[..., None] * bb_re
    ktap = (jnp.einsum('dgnp,tdgpm->tdgnm', cr, ab_re) - jnp.einsum('dgnp,tdgpm->tdgnm', ci, ab_im))[:L]
    i_idx = np.arange(L)[:, None]
    s_idx = np.arange(L)[None, :]
    lag_f = i_idx - s_idx
    lag_r = s_idx - i_idx
    kf = jnp.where((lag_f >= 0)[:, :, None, None, None], ktap[np.clip(lag_f, 0, L - 1), 0], 0.0)
    kr = jnp.where((lag_r >= 0)[:, :, None, None, None], ktap[np.clip(lag_r, 0, L - 1), 1], 0.0)
    w1 = L * SSM_GROUP
    mt = jnp.transpose(kf + kr, (2, 1, 4, 0, 3)).reshape(SSM_GROUPS, w1, w1)

    exp_f = L - 1 - np.arange(L)
    exp_r = np.arange(L)
    inj = lambda ab, d, exps: jnp.transpose(ab[exps, d], (1, 0, 3, 2)).reshape(SSM_GROUPS, w1, SSM_STATE)
    wb_parts = [inj(ab_re, 0, exp_f), inj(ab_im, 0, exp_f), inj(ab_re, 1, exp_r), inj(ab_im, 1, exp_r)]

    def ro(d, exps):
        pr, pi = pw_re[exps, d], pw_im[exps, d]
        e_re = cr[d][None] * pr[:, :, None, :] - ci[d][None] * pi[:, :, None, :]
        e_im = cr[d][None] * pi[:, :, None, :] + ci[d][None] * pr[:, :, None, :]
        f = lambda e: jnp.transpose(e, (1, 3, 0, 2)).reshape(SSM_GROUPS, SSM_STATE, w1)
        return [f(e_re), f(-e_im)]

    wc_parts = ro(0, np.arange(L) + 1) + ro(1, L - np.arange(L))
    npair = SSM_GROUPS // 2

    def pair_diag(x):
        x = x.reshape(npair, 2, x.shape[1], x.shape[2])
        zed = jnp.zeros_like(x[:, 0])
        return jnp.concatenate([jnp.concatenate([x[:, 0], zed], axis=2),
                                jnp.concatenate([zed, x[:, 1]], axis=2)], axis=1)

    wb2 = jnp.concatenate([pair_diag(p) for p in wb_parts], axis=2)
    wc2 = jnp.concatenate([pair_diag(p) for p in wc_parts], axis=1)
    mt2 = pair_diag(mt)
    d2 = jnp.tile(d_skip.astype(F32).reshape(SSM_GROUPS, 1, SSM_GROUP), (1, L, 1)).reshape(npair, 1, 2 * w1)
    packp = lambda x: x.reshape(2, npair, 2 * SSM_STATE)
    ar2 = jnp.concatenate([packp(pw_re[L])[0], packp(pw_re[L])[1]], axis=-1)[:, None, :]
    ai2 = jnp.concatenate([packp(pw_im[L])[0], packp(pw_im[L])[1]], axis=-1)[:, None, :]
    return wb2.astype(BF16), mt2.astype(BF16), wc2.astype(BF16), d2, ar2, ai2


def _glu_kernel(y_ref, w_ref, b_ref, o_ref):
    y = y_ref[0].astype(F32)
    g = 0.5 * y * (1.0 + jnp.tanh(math.sqrt(2.0 / math.pi) * (y + 0.044715 * (y * y * y))))
    t = jnp.dot(g.astype(BF16), w_ref[...], preferred_element_type=F32) + b_ref[...]
    o_ref[0] = (g * jax.nn.sigmoid(t)).astype(o_ref.dtype)


def _glu(y3, w, b, ntile):
    bsz, t, n = y3.shape
    return pl.pallas_call(
        _glu_kernel,
        grid=(bsz, ntile),
        in_specs=[pl.BlockSpec((1, TOK, n), lambda i, j: (i, j, 0)),
                  pl.BlockSpec((n, n), lambda i, j: (0, 0)),
                  pl.BlockSpec((1, n), lambda i, j: (0, 0))],
        out_specs=pl.BlockSpec((1, TOK, n), lambda i, j: (i, j, 0)),
        out_shape=jax.ShapeDtypeStruct((bsz, ntile * TOK, n), BF16),
        compiler_params=_cparams(("arbitrary", "arbitrary")),
        name="glu",
    )(y3, w, b.reshape(1, n))


def _route(h, whi_ref, wlo_ref, b_ref, h_ref, meta_ref, tw_ref, cnt_ref, run_ref, n_exp):
    h_ref[0] = _pack_bf16_pairs(h)
    h_hi, h_lo = _split_bf16(h)
    logits = (jnp.dot(h_hi, whi_ref[...], preferred_element_type=F32)
              + jnp.dot(h_lo, whi_ref[...], preferred_element_type=F32)
              + jnp.dot(h_hi, wlo_ref[...], preferred_element_type=F32)) + b_ref[...]
    lane = lax.broadcasted_iota(jnp.int32, (TOK, LANES), 1).astype(F32)
    lg = jnp.where(lane < n_exp, logits, -jnp.inf)
    vals, idxs = [], []
    for _ in range(TOP_K):
        mx = jnp.max(lg, axis=-1, keepdims=True)
        idx = jnp.min(jnp.where(lg == mx, lane, float(LANES)), axis=-1, keepdims=True)
        vals.append(mx)
        idxs.append(idx)
        lg = jnp.where(lane == idx, -jnp.inf, lg)
    es = [jnp.exp(v - vals[0]) for v in vals]
    den = es[0] + es[1] + es[2] + es[3]
    onehot = jnp.zeros((TOK, LANES), F32)
    for idx in idxs:
        onehot = onehot + jnp.where(lane == idx, 1.0, 0.0)
    row = lax.broadcasted_iota(jnp.int32, (TOK, TOK), 0)
    col = lax.broadcasted_iota(jnp.int32, (TOK, TOK), 1)
    tri = jnp.where(col < row, 1.0, 0.0).astype(BF16)
    before = jnp.dot(tri, onehot.astype(BF16), preferred_element_type=F32) + run_ref[...]
    meta = jnp.zeros((TOK, LANES), F32)
    tw = jnp.zeros((TOK, LANES), F32)
    for k in range(TOP_K):
        rank = jnp.sum(jnp.where(lane == idxs[k], before, 0.0), axis=-1, keepdims=True)
        meta = meta + jnp.where(lane == float(k), idxs[k], 0.0) + jnp.where(lane == float(TOP_K + k), rank, 0.0)
        tw = tw + jnp.where(lane == float(k), es[k] / den, 0.0)
    meta_ref[0] = meta.astype(jnp.int32)
    tw_ref[0] = tw
    run = run_ref[...] + jnp.sum(onehot, axis=0, keepdims=True)
    run_ref[...] = run
    cnt_ref[...] = run


def _mix_kernel(oa_ref, ob_ref, os_ref, ga_ref, gb_ref, gs_ref, wa_ref, wb_ref, wc_ref, x_ref, wo_ref, m_ref, g2_ref,
                whi_ref, wlo_ref, br_ref, x1_ref, h_ref, meta_ref, tw_ref, cnt_ref, run_ref, *, n_exp):
    first = jnp.logical_and(pl.program_id(0) == 0, pl.program_id(1) == 0)

    @pl.when(first)
    def _():
        run_ref[...] = jnp.zeros_like(run_ref)

    mix = jax.nn.sigmoid(ga_ref[0].astype(F32)) * jnp.dot(oa_ref[0], wa_ref[...], preferred_element_type=F32)
    mix += jax.nn.sigmoid(gb_ref[0].astype(F32)) * jnp.dot(ob_ref[0], wb_ref[...], preferred_element_type=F32)
    mix += jax.nn.sigmoid(gs_ref[0].astype(F32)) * jnp.dot(os_ref[0], wc_ref[...], preferred_element_type=F32)
    m = m_ref[0, 0]
    x1 = x_ref[0] + m[2:3] * jnp.dot(mix.astype(BF16), wo_ref[...], preferred_element_type=F32)
    x1_ref[0] = x1
    h = _rms(x1) * g2_ref[...] * (1.0 + m[4:5]) + m[3:4]
    _route(h, whi_ref, wlo_ref, br_ref, h_ref, meta_ref, tw_ref, cnt_ref, run_ref, n_exp)


def _mix_route(oa, ob, os_, z3, wa, wb, wc, x, wo, modsel, g2, w_router, b_router, nlt, ntile):
    bsz, _, k = oa.shape
    d = wa.shape[1]
    n_exp = w_router.shape[1]
    wr = jnp.zeros((d, LANES), F32).at[:, :n_exp].set(w_router.astype(F32))
    br = jnp.zeros((1, LANES), F32).at[0, :n_exp].set(b_router.astype(F32))
    tok = lambda last: pl.BlockSpec((1, TOK, last), lambda i, j: (i, j, 0))
    gspec = lambda br_: pl.BlockSpec((1, TOK, d), lambda i, j: (i, j, ZC_GATE // d + br_))
    once = lambda shp: pl.BlockSpec(shp, lambda i, j: (0,) * len(shp), pipeline_mode=pl.Buffered(1))
    rows = ntile * TOK
    return pl.pallas_call(
        functools.partial(_mix_kernel, n_exp=n_exp),
        grid=(bsz, ntile),
        in_specs=[tok(k), tok(k), tok(k), gspec(0), gspec(1), gspec(2), once((k, d)), once((k, d)), once((k, d)),
                  tok(d), once((d, d)),
                  pl.BlockSpec((1, 1, 6, d), lambda i, j: (i, _type_idx(j, nlt), 0, 0)),
                  once((1, d)), once((d, LANES)), once((d, LANES)), once((1, LANES))],
        out_specs=[tok(d), tok(d // 2), tok(LANES), tok(LANES), pl.BlockSpec((1, LANES), lambda i, j: (0, 0))],
        out_shape=[jax.ShapeDtypeStruct((bsz, rows, d), F32),
                   jax.ShapeDtypeStruct((bsz, rows, d // 2), jnp.uint32),
                   jax.ShapeDtypeStruct((bsz, rows, LANES), jnp.int32),
                   jax.ShapeDtypeStruct((bsz, rows, LANES), F32),
                   jax.ShapeDtypeStruct((1, LANES), F32)],
        scratch_shapes=[pltpu.VMEM((1, LANES), F32)],
        compiler_params=_cparams(("arbitrary", "arbitrary")),
        name="mix_route",
    )(oa, ob, os_, z3, z3, z3, wa, wb, wc, x, wo, modsel, g2.reshape(1, d), *_split_bf16(wr), br)


def _dispatch_kernel(pos_ref, h_ref, xs_in_ref, xs_ref, sem):
    del xs_in_ref

    def row_copy(r, k):
        return pltpu.make_async_copy(h_ref.at[pl.ds(r, 1)], xs_ref.at[pl.ds(pos_ref[r * TOP_K + k], 1)], sem)

    def start(r, c):
        for k in range(TOP_K):
            row_copy(r, k).start(priority=k % 2)
        return c

    def wait(r, c):
        for k in range(TOP_K):
            row_copy(r, k).wait()
        return c

    lax.fori_loop(0, TOK, start, 0)
    lax.fori_loop(0, TOK, wait, 0, unroll=DMA_UNROLL)


def _dispatch(pos, h2, xs0):
    m, d = h2.shape
    return pl.pallas_call(
        _dispatch_kernel,
        grid=(m // TOK,),
        in_specs=[pl.BlockSpec((TOK * TOP_K,), lambda i: (i,), memory_space=pltpu.SMEM),
                  pl.BlockSpec((TOK, d), lambda i: (i, 0)),
                  pl.BlockSpec(memory_space=pl.ANY)],
        out_specs=pl.BlockSpec(memory_space=pl.ANY),
        out_shape=jax.ShapeDtypeStruct(xs0.shape, xs0.dtype),
        scratch_shapes=[pltpu.SemaphoreType.DMA(())],
        input_output_aliases={2: 0},
        compiler_params=_cparams(("arbitrary",)),
        name="dispatch",
    )(pos, h2, xs0)


def _experts_kernel(te_ref, tv_ref, x_ref, wgu_ref, bgu_ref, wd_ref, bd_ref, y_ref, *, d_ff, fchunk):
    i = pl.program_id(0)

    @pl.when(tv_ref[i] == 0)
    def _():
        y_ref[...] = jnp.zeros_like(y_ref)

    @pl.when(tv_ref[i] != 0)
    def _():
        x = jnp.concatenate(_unpack_bf16_pairs(x_ref[...]), axis=1).astype(BF16)
        acc = jnp.zeros(x.shape, F32)
        for c in range(d_ff // fchunk):
            lo = c * fchunk
            g = jnp.dot(x, wgu_ref[0, 0, :, lo:lo + fchunk], preferred_element_type=F32) + bgu_ref[0, :, lo:lo + fchunk]
            lin = (jnp.dot(x, wgu_ref[0, 0, :, d_ff + lo:d_ff + lo + fchunk], preferred_element_type=F32)
                   + bgu_ref[0, :, d_ff + lo:d_ff + lo + fchunk])
            g = jnp.minimum(g, SWIGLU_LIMIT)
            lin = jnp.clip(lin, -SWIGLU_LIMIT, SWIGLU_LIMIT)
            act = g * jax.nn.sigmoid(SWIGLU_ALPHA * g) * (lin + 1.0)
            acc = acc + jnp.dot(act.astype(BF16), wd_ref[0, 0, lo:lo + fchunk, :], preferred_element_type=F32)
        y_ref[...] = _pack_bf16_pairs(acc + bd_ref[0])


def _experts(te, tv, xs, wgu, bgu, wd, bd, layer):
    rows, dh = xs.shape
    _, n_exp, d, ff2 = wgu.shape
    d_ff = ff2 // 2
    grid_spec = pltpu.PrefetchScalarGridSpec(
        num_scalar_prefetch=2,
        grid=(rows // TOK,),
        in_specs=[pl.BlockSpec((TOK, dh), lambda i, te, tv: (i, 0)),
                  pl.BlockSpec((1, 1, d, ff2), lambda i, te, tv: (layer, te[i], 0, 0)),
                  pl.BlockSpec((1, 1, ff2), lambda i, te, tv: (te[i], 0, 0)),
                  pl.BlockSpec((1, 1, d_ff, d), lambda i, te, tv: (layer, te[i], 0, 0)),
                  pl.BlockSpec((1, 1, d), lambda i, te, tv: (te[i], 0, 0))],
        out_specs=pl.BlockSpec((TOK, dh), lambda i, te, tv: (i, 0)),
    )
    return pl.pallas_call(
        functools.partial(_experts_kernel, d_ff=d_ff, fchunk=768),
        grid_spec=grid_spec,
        out_shape=jax.ShapeDtypeStruct((rows, dh), jnp.uint32),
        compiler_params=_cparams(("arbitrary",)),
        name="experts",
    )(te, tv, xs, wgu, bgu.reshape(n_exp, 1, ff2), wd, bd.reshape(n_exp, 1, d))


def _combine_kernel(pos_ref, x_ref, tw_ref, m_ref, ys_ref, o_ref, *scratch):
    bufs, sem = scratch[:TOP_K], scratch[TOP_K]

    def row_copy(r, k):
        return pltpu.make_async_copy(ys_ref.at[pl.ds(pos_ref[r * TOP_K + k], 1)], bufs[k].at[pl.ds(r, 1)], sem)

    def start(r, c):
        for k in range(TOP_K):
            row_copy(r, k).start(priority=k % 2)
        return c

    def wait(r, c):
        for k in range(TOP_K):
            row_copy(r, k).wait()
        return c

    lax.fori_loop(0, TOK, start, 0, unroll=DMA_UNROLL)
    lax.fori_loop(0, TOK, wait, 0, unroll=DMA_UNROLL)
    tw = tw_ref[0]
    acc_lo = acc_hi = None
    for k in range(TOP_K):
        lo, hi = _unpack_bf16_pairs(bufs[k][...])
        wk = tw[:, k:k + 1]
        acc_lo = wk * lo if acc_lo is None else acc_lo + wk * lo
        acc_hi = wk * hi if acc_hi is None else acc_hi + wk * hi
    o_ref[0] = x_ref[0] + m_ref[0, 0][5:6] * jnp.concatenate([acc_lo, acc_hi], axis=1)


def _combine(pos, x, tw, modsel, ys, nlt):
    b, t, d = x.shape
    nt = t // TOK
    return pl.pallas_call(
        _combine_kernel,
        grid=(b, nt),
        in_specs=[pl.BlockSpec((TOK * TOP_K,), lambda i, j: (i * nt + j,), memory_space=pltpu.SMEM),
                  pl.BlockSpec((1, TOK, d), lambda i, j: (i, j, 0)),
                  pl.BlockSpec((1, TOK, LANES), lambda i, j: (i, j, 0)),
                  pl.BlockSpec((1, 1, 6, d), lambda i, j: (i, _type_idx(j, nlt), 0, 0)),
                  pl.BlockSpec(memory_space=pl.ANY)],
        out_specs=pl.BlockSpec((1, TOK, d), lambda i, j: (i, j, 0)),
        out_shape=jax.ShapeDtypeStruct((b, t, d), F32),
        scratch_shapes=[pltpu.VMEM((TOK, d // 2), jnp.uint32)] * TOP_K + [pltpu.SemaphoreType.DMA(())],
        compiler_params=_cparams(("arbitrary", "arbitrary")),
        name="combine",
    )(pos, x, tw, modsel, ys)


def _moe(xa, h2, meta, tw, cnt, modsel, n_exp, wgu, bgu, wd, bd, nlt, layer):
    b, t, d = xa.shape
    m = b * t

    counts = cnt[0, :n_exp].astype(jnp.int32)
    padded = ((counts + TOK - 1) // TOK) * TOK
    gend = jnp.cumsum(padded)
    gstart = gend - padded
    meta2 = meta.reshape(m, LANES)
    pos = (gstart[meta2[:, 0:TOP_K]] + meta2[:, TOP_K:2 * TOP_K]).reshape(m * TOP_K)
    ntiles = (m * TOP_K) // TOK + n_exp
    tile_start = jnp.arange(ntiles, dtype=jnp.int32) * TOK
    te = jnp.sum((tile_start[:, None] >= gend[None, :]).astype(jnp.int32), axis=1)
    te = jnp.minimum(te, n_exp - 1)
    tv = (tile_start < gend[-1]).astype(jnp.int32)

    xs = _dispatch(pos, h2.reshape(m, d // 2), jnp.zeros((ntiles * TOK, d // 2), jnp.uint32))
    ys = _experts(te, tv, xs, wgu, bgu, wd, bd, layer)
    return _combine(pos, xa, tw, modsel, ys, nlt)


def _rope_tables(seq, ctx):
    t = jnp.arange(seq, dtype=jnp.int32)
    row = (t // GRID_W).astype(F32)
    col = (t % GRID_W).astype(F32)
    n_freq = MLA_ROPE // 4
    inv_freq = ROPE_BASE ** (-jnp.arange(n_freq, dtype=F32) / n_freq)
    ang = jnp.concatenate([row[:, None] * inv_freq, col[:, None] * inv_freq], axis=-1)
    cos, sin = jnp.cos(ang), jnp.sin(ang)
    cos2 = jnp.tile(jnp.concatenate([cos, cos], axis=-1), (1, 2))
    sin2 = jnp.tile(jnp.concatenate([-sin, sin], axis=-1), (1, 2))
    cos2 = jnp.concatenate([cos2, jnp.ones((ctx, LANES), F32)], axis=0)
    sin2 = jnp.concatenate([sin2, jnp.zeros((ctx, LANES), F32)], axis=0)
    return cos2, sin2


def _pad_cols(w, width):
    return jnp.pad(w, ((0, 0), (0, width - w.shape[1])))


def _layout_w_in(w):
    o_ckv = MLA_Q_RANK
    o_kpe = o_ckv + MLA_KV_RANK
    o_naq = o_kpe + MLA_ROPE
    o_ssm = o_naq + 3 * NA_WIDTH
    o_gate = o_ssm + SSM_WIDTH
    parts = [w[:, o_gate:],
             w[:, o_naq:o_ssm],
             w[:, :o_ckv],
             _pad_cols(w[:, o_kpe:o_naq], ZC_CKV - ZC_KPE),
             w[:, o_ckv:o_kpe],
             w[:, o_ssm:o_gate]]
    out = jnp.concatenate(parts, axis=1)
    assert out.shape[1] == ZW
    return out.astype(BF16)


def kernel(x, c, ctx, c_ctx, w_ada, b_ada, norm1_g, w_in, mla_q_norm, mla_kv_norm, mla_w_uq, mla_w_ukv, mla_q_gain, mla_k_gain, na_q_gain, na_k_gain, na_rpb, ssm_lam_re, ssm_lam_im, ssm_log_dt, ssm_b_re, ssm_b_im, ssm_c_re, ssm_c_im, ssm_d, ssm_w_glu, ssm_b_glu, w_pa, w_pb, w_pc, w_out, norm2_g, w_router, b_router, w_gate_up, b_gate_up, w_down, b_down):
    bsz, seq, d = x.shape
    nctx = ctx.shape[1]
    depth = w_ada.shape[0]
    assert nctx == TOK and seq % (NA_QROWS * GRID_W) == 0 and seq // GRID_W >= NA_KROWS
    nlt = seq // TOK
    t = seq + nctx
    m = bsz * t
    rows = seq // GRID_W
    nblk = rows // NA_QROWS
    nchunk = t // SSM_CHUNK

    cos2, sin2 = _rope_tables(seq, nctx)
    cvec = jnp.zeros((16, d), F32).at[:bsz].set(c.astype(F32)).at[bsz].set(c_ctx.astype(F32))
    xa = jnp.concatenate([x, ctx], axis=1).astype(F32)
    row128 = lambda v: v.astype(F32).reshape(1, LANES)
    perm = jnp.asarray(_ssm_perm(), BF16)
    wgu_all = w_gate_up.astype(BF16)
    wd_all = w_down.astype(BF16)

    for l in range(depth):
        need_ctx = l < depth - 1
        ntile = t // TOK if need_ctx else nlt
        mod = _ada(cvec, w_ada[l], b_ada[l])
        mod_lat = mod[:bsz].reshape(bsz, 1, 6, d)
        mod_ctx = jnp.broadcast_to(mod[bsz].reshape(1, 1, 6, d), (bsz, 1, 6, d))
        modsel = jnp.concatenate([mod_lat, mod_ctx], axis=1)

        h1 = _norm_mod(xa, norm1_g[l], modsel, nlt)
        z = _matmul(h1.reshape(m, d), _layout_w_in(w_in[l]), BF16, "in_proj", ZW // 4)
        z3 = z.reshape(bsz, t, ZW)

        wq = mla_w_uq[l].reshape(MLA_Q_RANK, MLA_HEADS, MLA_QK)
        wq = jnp.concatenate([wq[:, :, :MLA_NOPE].reshape(MLA_Q_RANK, -1),
                              wq[:, :, MLA_NOPE:].reshape(MLA_Q_RANK, -1)], axis=1).astype(BF16)
        qg_r = jnp.tile(mla_q_gain[l][MLA_NOPE:], 2)
        kg_r = jnp.concatenate([mla_k_gain[l][MLA_NOPE:], jnp.zeros((MLA_ROPE,), mla_k_gain.dtype)])
        qf, kf, vf, qn, kn = _projections(
            z3, mla_q_norm[l].astype(F32).reshape(1, -1), wq, row128(mla_q_gain[l][:MLA_NOPE]), row128(qg_r),
            mla_kv_norm[l].astype(F32).reshape(1, -1), mla_w_ukv[l].astype(BF16), row128(mla_k_gain[l][:MLA_NOPE]),
            row128(kg_r), cos2, sin2, row128(jnp.tile(na_q_gain[l], 2)), row128(jnp.tile(na_k_gain[l], 2)))
        o_a = _mla_attn(qf, kf, vf, seq, need_ctx)

        bias = _na_bias_tables(na_rpb[l], rows)
        o_b = _na_attn(qn, kn, z3, bias, nblk, need_ctx)

        wb2, mt2, wc2, d2, ar2, ai2 = _ssm_params(ssm_lam_re[l], ssm_lam_im[l], ssm_log_dt[l], ssm_b_re[l],
                                                   ssm_b_im[l], ssm_c_re[l], ssm_c_im[l], ssm_d[l])
        r = nchunk * bsz
        u = z3[:, :, ZC_SSM:ZC_SSM + SSM_WIDTH].reshape(bsz, nchunk, SSM_CHUNK, SSM_NSUPER, LANES)
        x8 = jnp.transpose(u, (3, 1, 0, 2, 4)).reshape(SSM_NSUPER * r, SSM_CW)
        xp = _matmul(x8, perm, BF16, "ssm_perm_in", SSM_CW)
        yp = _ssm(xp.reshape(SSM_NSUPER, r, SSM_CW), wb2, mt2, wc2, d2, ar2, ai2, bsz, nchunk, nctx // SSM_CHUNK)
        y8 = _matmul(yp.reshape(SSM_NSUPER * r, SSM_CW), perm.T, BF16, "ssm_perm_out", SSM_CW)
        y = jnp.transpose(y8.reshape(SSM_NSUPER, nchunk, bsz, SSM_CHUNK, LANES), (2, 1, 3, 0, 4))
        o_s = _glu(y.reshape(bsz, t, SSM_WIDTH), ssm_w_glu[l].astype(BF16), ssm_b_glu[l].astype(F32), ntile)

        xa, h2, meta, tw, cnt = _mix_route(o_a, o_b, o_s, z3, w_pa[l].astype(BF16), w_pb[l].astype(BF16),
                                           w_pc[l].astype(BF16), xa, w_out[l].astype(BF16), modsel,
                                           norm2_g[l].astype(F32), w_router[l], b_router[l], nlt, ntile)
        xa = _moe(xa, h2, meta, tw, cnt, modsel, w_router.shape[2], wgu_all, b_gate_up[l].astype(F32), wd_all,
                  b_down[l].astype(F32), nlt, l)
    return xa[:, :seq].astype(x.dtype)
```
